```python
import math
import jax, jax.numpy as jnp
from jax import lax
import numpy as np

D_MODEL = 1024
BATCH = 8
SEQ = 4096
DEPTH = 1

CHUNK = 64
Q_BLOCK = 128
EPS = 1e-6
GLA_HEADS = 4
GLA_DK = D_MODEL // 2 // GLA_HEADS
GLA_DV = D_MODEL // GLA_HEADS
GLA_RANK = 16
GLA_TAU = 16.0
DIFF_HEADS = 8
DIFF_DH = 64
DIFF_DV = 2 * DIFF_DH
D_FF = 2816
CONV_W = 3
IN_SIZES = (
    GLA_HEADS * GLA_DK,
    GLA_HEADS * GLA_DK,
    GLA_HEADS * GLA_DV,
    GLA_HEADS * GLA_DV,
    GLA_RANK,
    DIFF_HEADS * 2 * DIFF_DH,
    DIFF_HEADS * 2 * DIFF_DH,
    DIFF_HEADS * DIFF_DV,
    D_MODEL,
    D_MODEL,
)
IN_DIM = sum(IN_SIZES)

kernel_name = "hybrid_gla_diffattn_convffn_block"


def rms_norm(x, g):
    xf = x.astype(jnp.float32)
    y = xf * lax.rsqrt(jnp.mean(xf * xf, axis=-1, keepdims=True) + EPS)
    return (y * g.astype(jnp.float32)).astype(x.dtype)


def gla_mixer(q, k, v, g, a_code, w_alpha_up, b_alpha, norm_g):
    dt = q.dtype
    f32 = jnp.float32
    B, S, _ = q.shape
    nc = S // CHUNK
    log_a = jax.nn.log_sigmoid((a_code @ w_alpha_up + b_alpha).astype(f32)) / GLA_TAU

    def heads(t, d):
        return t.astype(f32).reshape(B, nc, CHUNK, GLA_HEADS, d).transpose(0, 3, 1, 2, 4)

    qh = heads(q, GLA_DK) * (GLA_DK ** -0.5)
    kh = heads(k, GLA_DK)
    vh = heads(v, GLA_DV)
    bcum = jnp.cumsum(heads(log_a, GLA_DK), axis=3)
    b_last = bcum[..., -1:, :]

    q_fwd = qh * jnp.exp(bcum)
    k_fwd = kh * jnp.exp(-bcum)
    q_bwd = qh * jnp.exp(-bcum)
    k_bwd = kh * jnp.exp(bcum)
    s_fwd = jnp.einsum('bhncd,bhnsd->bhncs', q_fwd, k_fwd)
    s_bwd = jnp.einsum('bhncd,bhnsd->bhncs', q_bwd, k_bwd)
    lower = jnp.tril(jnp.ones((CHUNK, CHUNK), dtype=bool))
    scores = jnp.where(lower, s_fwd, s_bwd)
    o_intra = jnp.einsum('bhncs,bhnse->bhnce', scores, vh)

    delta = jnp.einsum('bhncd,bhnce->bhnde', kh * jnp.exp(b_last - bcum), vh)
    decay = jnp.exp(b_last[..., 0, :])

    def step(state, inp):
        dec, dlt = inp
        return dec[..., None] * state + dlt, state

    init = jnp.zeros((B, GLA_HEADS, GLA_DK, GLA_DV), f32)
    _, s_prev = lax.scan(step, init, (jnp.moveaxis(decay, 2, 0), jnp.moveaxis(delta, 2, 0)))
    s_prev = jnp.moveaxis(s_prev, 0, 2)
    o_inter = jnp.einsum('bhncd,bhnde->bhnce', q_fwd, s_prev)

    o = rms_norm(o_intra + o_inter, norm_g)
    o = o.transpose(0, 2, 3, 1, 4).reshape(B, S, GLA_HEADS * GLA_DV).astype(dt)
    return o * jax.nn.silu(g)


def diff_attention(q, k, v, q_norm_g, k_norm_g, lam_q1, lam_k1, lam_q2, lam_k2,
                   subln_g, lambda_init):
    B, S, _ = q.shape
    f32 = jnp.float32
    qh = rms_norm(q.reshape(B, S, DIFF_HEADS, 2, DIFF_DH), q_norm_g)
    kh = rms_norm(k.reshape(B, S, DIFF_HEADS, 2, DIFF_DH), k_norm_g)
    vh = v.reshape(B, S, DIFF_HEADS, DIFF_DV)
    lam = (jnp.exp(jnp.sum(lam_q1 * lam_k1).astype(f32))
           - jnp.exp(jnp.sum(lam_q2 * lam_k2).astype(f32)) + lambda_init)
    slopes = 2.0 ** (-8.0 * (jnp.arange(DIFF_HEADS, dtype=f32) + 1.0) / DIFF_HEADS)
    kpos = jnp.arange(S)
    kchunk = kpos // CHUNK
    nb = S // Q_BLOCK
    qb = jnp.moveaxis(qh.reshape(B, nb, Q_BLOCK, DIFF_HEADS, 2, DIFF_DH), 1, 0)
    scale = DIFF_DH ** -0.5

    def block(args):
        qblk, i = args
        qpos = i * Q_BLOCK + jnp.arange(Q_BLOCK)
        logits = jnp.einsum('bqhjd,bshjd->bhjqs', qblk, kh).astype(f32) * scale
        dist = jnp.abs(qpos[:, None] - kpos[None, :]).astype(f32)
        logits = logits - (slopes[:, None, None] * dist)[None, :, None]
        allowed = kchunk[None, :] <= (qpos // CHUNK)[:, None]
        logits = jnp.where(allowed, logits, -jnp.inf)
        p = jax.nn.softmax(logits, axis=-1)
        attn = p[:, :, 0] - lam * p[:, :, 1]
        return jnp.einsum('bhqs,bshe->bqhe', attn.astype(vh.dtype), vh)

    o = lax.map(block, (qb, jnp.arange(nb)))
    o = jnp.moveaxis(o, 0, 1).reshape(B, S, DIFF_HEADS, DIFF_DV)
    o = rms_norm(o, subln_g) * (1.0 - lambda_init)
    return o.reshape(B, S, DIFF_HEADS * DIFF_DV)


def conv_ffn(h, w_up, conv_w, conv_b, w_down):
    S = h.shape[1]
    u = h @ w_up
    up = jnp.pad(u, ((0, 0), (CONV_W - 1, 0), (0, 0)))
    u = sum(conv_w[j] * up[:, j:j + S] for j in range(CONV_W)) + conv_b
    a, b = jnp.split(u, 2, axis=-1)
    return (jax.nn.silu(a) * b) @ w_down


def setup_inputs(seed: int = 0) -> dict:
    key = jax.random.key(seed)
    ks = jax.random.split(key, 32)
    L, D, F = DEPTH, D_MODEL, D_FF
    nrm = lambda k, shape, fan: jax.random.normal(k, shape, jnp.float32) * (fan ** -0.5)
    gain = lambda k, shape: 1.0 + 0.05 * jax.random.normal(k, shape, jnp.float32)
    return {
        "x": jax.random.normal(ks[0], (BATCH, SEQ, D), jnp.float32),
        "c": jax.random.normal(ks[1], (BATCH, D), jnp.float32),
        "w_ada": nrm(ks[2], (L, D, 6 * D), D),
        "b_ada": 0.02 * jax.random.normal(ks[3], (L, 6 * D), jnp.float32),
        "norm1_g": gain(ks[4], (L, D)),
        "w_in": nrm(ks[5], (L, D, IN_DIM), D),
        "w_alpha_up": nrm(ks[6], (L, GLA_RANK, GLA_HEADS * GLA_DK), GLA_RANK),
        "b_alpha": 0.5 + 0.1 * jax.random.normal(ks[7], (L, GLA_HEADS * GLA_DK), jnp.float32),
        "gla_norm_g": gain(ks[8], (L, GLA_DV)),
        "q_norm_g": gain(ks[9], (L, DIFF_DH)),
        "k_norm_g": gain(ks[10], (L, DIFF_DH)),
        "lam_q1": 0.1 * jax.random.normal(ks[11], (L, DIFF_DH), jnp.float32),
        "lam_k1": 0.1 * jax.random.normal(ks[12], (L, DIFF_DH), jnp.float32),
        "lam_q2": 0.1 * jax.random.normal(ks[13], (L, DIFF_DH), jnp.float32),
        "lam_k2": 0.1 * jax.random.normal(ks[14], (L, DIFF_DH), jnp.float32),
        "diff_norm_g": gain(ks[15], (L, DIFF_DV)),
        "w_gla_o": nrm(ks[16], (L, GLA_HEADS * GLA_DV, D), GLA_HEADS * GLA_DV),
        "w_diff_o": nrm(ks[17], (L, DIFF_HEADS * DIFF_DV, D), DIFF_HEADS * DIFF_DV),
        "w_out": nrm(ks[18], (L, D, D), D),
        "norm2_g": gain(ks[19], (L, D)),
        "w_up": nrm(ks[20], (L, D, 2 * F), D),
        "conv_w": nrm(ks[21], (L, CONV_W, 2 * F), CONV_W),
        "conv_b": 0.02 * jax.random.normal(ks[22], (L, 2 * F), jnp.float32),
        "w_down": nrm(ks[23], (L, F, D), F),
    }


def reference(x, c, w_ada, b_ada, norm1_g, w_in, w_alpha_up, b_alpha, gla_norm_g,
              q_norm_g, k_norm_g, lam_q1, lam_k1, lam_q2, lam_k2, diff_norm_g,
              w_gla_o, w_diff_o, w_out, norm2_g, w_up, conv_w, conv_b, w_down):
    split_idx = [int(v) for v in np.cumsum(IN_SIZES)[:-1]]
    for layer in range(DEPTH):
        lambda_init = 0.8 - 0.6 * math.exp(-0.3 * layer)
        mod = jax.nn.silu(c) @ w_ada[layer] + b_ada[layer]
        sh1, sc1, gt1, sh2, sc2, gt2 = [m[:, None, :] for m in jnp.split(mod, 6, axis=-1)]

        h = rms_norm(x, norm1_g[layer]) * (1.0 + sc1) + sh1
        proj = h @ w_in[layer]
        (g_q, g_k, g_v, g_g, g_a, d_q, d_k, d_v,
         gate_a, gate_b) = jnp.split(proj, split_idx, axis=-1)
        o_a = gla_mixer(g_q, g_k, g_v, g_g, g_a, w_alpha_up[layer], b_alpha[layer],
                        gla_norm_g[layer])
        o_b = diff_attention(d_q, d_k, d_v, q_norm_g[layer], k_norm_g[layer],
                             lam_q1[layer], lam_k1[layer], lam_q2[layer], lam_k2[layer],
                             diff_norm_g[layer], lambda_init)
        merged = (jax.nn.sigmoid(gate_a) * (o_a @ w_gla_o[layer])
                  + jax.nn.sigmoid(gate_b) * (o_b @ w_diff_o[layer]))
        x = x + gt1 * (merged @ w_out[layer])

        h2 = rms_norm(x, norm2_g[layer]) * (1.0 + sc2) + sh2
        x = x + gt2 * conv_ffn(h2, w_up[layer], conv_w[layer], conv_b[layer], w_down[layer])
    return x
```

```python
import functools
import math

import jax
import jax.numpy as jnp
from jax import lax
from jax.experimental import pallas as pl
from jax.experimental.pallas import tpu as pltpu

F32 = jnp.float32
BF16 = jnp.bfloat16

D_MODEL = 1024
CHUNK = 64
EPS = 1e-6
GLA_HEADS = 4
GLA_DK = 128
GLA_DV = 256
GLA_RANK = 16
GLA_TAU = 16.0
DIFF_HEADS = 8
DIFF_DH = 64
DIFF_DV = 128
D_FF = 2816
CONV_W = 3
LAMBDA_INIT = 0.8 - 0.6 * math.exp(-0.3 * 0)

LANES = 128
SUBLANES = 8
MASK_VALUE = -1e30

COL_GQ, COL_GK, COL_GV, COL_GG = 0, 512, 1024, 2048
COL_DQ, COL_DK, COL_DV = 3072, 4096, 5120
COL_GA, COL_GB, COL_ACODE = 6144, 7168, 8192
PROJ_COLS = 8192 + LANES
PROJ_TN = PROJ_COLS // 5

VMEM_LIMIT = 56 * 1024 * 1024


def _dot(a, b):
    return jnp.dot(a, b, preferred_element_type=F32)


def _dot_nt(a, b):
    return lax.dot_general(a, b, (((1,), (1,)), ((), ())), preferred_element_type=F32)


def _dot_tn(a, b):
    return lax.dot_general(a, b, (((0,), (0,)), ((), ())), preferred_element_type=F32)


def _split_bf16(v):
    hi = v.astype(BF16)
    lo = (v - hi.astype(F32)).astype(BF16)
    return hi, lo


def _sigmoid(v):
    return 1.0 / (1.0 + jnp.exp(-v))


def _chunk_id(pos):
    return lax.shift_right_logical(pos, CHUNK.bit_length() - 1)


def _ada_kernel(c_ref, w_ref, b_ref, lq1_ref, lk1_ref, lq2_ref, lk2_ref, mod_ref, lam_ref):
    c = c_ref[...]
    a = c * _sigmoid(c)
    a_hi, a_lo = _split_bf16(a)
    w_hi, w_lo = _split_bf16(w_ref[...])
    mod_ref[...] = _dot(a_hi, w_hi) + _dot(a_lo, w_hi) + _dot(a_hi, w_lo) + b_ref[...]
    s1 = jnp.sum(lq1_ref[...] * lk1_ref[...], axis=-1, keepdims=True)
    s2 = jnp.sum(lq2_ref[...] * lk2_ref[...], axis=-1, keepdims=True)
    lam = jnp.exp(s1) - jnp.exp(s2) + LAMBDA_INIT
    lam_ref[...] = jnp.broadcast_to(lam, lam_ref.shape)


def _ada_mod(c, w_ada, b_ada, lq1, lk1, lq2, lk2):
    B, D = c.shape
    n_out = w_ada.shape[1]
    tn = D
    small = pl.BlockSpec((1, DIFF_DH), lambda j: (0, 0))
    return pl.pallas_call(
        _ada_kernel,
        grid=(n_out // tn,),
        in_specs=[
            pl.BlockSpec((B, D), lambda j: (0, 0)),
            pl.BlockSpec((D, tn), lambda j: (0, j)),
            pl.BlockSpec((1, tn), lambda j: (0, j)),
            small, small, small, small,
        ],
        out_specs=[
            pl.BlockSpec((B, tn), lambda j: (0, j)),
            pl.BlockSpec((1, LANES), lambda j: (0, 0)),
        ],
        out_shape=[
            jax.ShapeDtypeStruct((B, n_out), F32),
            jax.ShapeDtypeStruct((1, LANES), F32),
        ],
        compiler_params=pltpu.CompilerParams(dimension_semantics=("arbitrary",)),
        name="ada_mod",
    )(c, w_ada, b_ada, lq1, lk1, lq2, lk2)


def _in_proj_kernel(x_ref, mod_ref, g_ref, w_ref, o_ref, h_scr):
    @pl.when(pl.program_id(1) == 0)
    def _():
        x = x_ref[...]
        ms = jnp.mean(x * x, axis=-1, keepdims=True)
        y = x * lax.rsqrt(ms + EPS) * g_ref[...]
        h = y * (1.0 + mod_ref[1:2, :]) + mod_ref[0:1, :]
        h_scr[...] = h.astype(BF16)

    o_ref[...] = _dot(h_scr[...], w_ref[...]).astype(BF16)


def _in_proj(x2, mod3, g1, w_in_r, S):
    N, D = x2.shape
    tm = min(1024, S)
    tiles_per_seq = S // tm
    return pl.pallas_call(
        _in_proj_kernel,
        grid=(N // tm, PROJ_COLS // PROJ_TN),
        in_specs=[
            pl.BlockSpec((tm, D), lambda i, j: (i, 0)),
            pl.BlockSpec((None, 6, D), lambda i, j: (i // tiles_per_seq, 0, 0)),
            pl.BlockSpec((1, D), lambda i, j: (0, 0)),
            pl.BlockSpec((D, PROJ_TN), lambda i, j: (0, j)),
        ],
        out_specs=pl.BlockSpec((tm, PROJ_TN), lambda i, j: (i, j)),
        out_shape=jax.ShapeDtypeStruct((N, PROJ_COLS), BF16),
        scratch_shapes=[pltpu.VMEM((tm, D), BF16)],
        compiler_params=pltpu.CompilerParams(
            dimension_semantics=("parallel", "arbitrary"), vmem_limit_bytes=VMEM_LIMIT),
        name="in_proj",
    )(x2, mod3, g1, w_in_r)


def _gla_kernel(q_ref, k_ref, v_ref, g_ref, a_ref, wa_ref, ba_ref, ng_ref, o_ref, st_ref, *, tc):
    nchunk = tc // CHUNK

    @pl.when(pl.program_id(2) == 0)
    def _():
        st_ref[...] = jnp.zeros_like(st_ref)

    z = _dot(a_ref[...], wa_ref[...]) + ba_ref[...]
    log_a = (jnp.minimum(z, 0.0) - jnp.log1p(jnp.exp(-jnp.abs(z)))) * (1.0 / GLA_TAU)

    row = lax.broadcasted_iota(jnp.int32, (tc, tc), 0)
    col = lax.broadcasted_iota(jnp.int32, (tc, tc), 1)
    same = _chunk_id(row) == _chunk_id(col)
    lower = same & (col <= row)
    upper = same & (col > row)
    tri = jnp.where(lower, 1.0, 0.0).astype(BF16)

    la_hi, la_lo = _split_bf16(log_a)
    bcum = _dot(tri, la_hi) + _dot(tri, la_lo)
    b_last = [bcum[c * CHUNK + CHUNK - 1:(c + 1) * CHUNK, :] for c in range(nchunk)]
    b_last_full = jnp.concatenate(
        [jnp.broadcast_to(bl, (CHUNK, GLA_DK)) for bl in b_last], axis=0)

    eb = jnp.exp(bcum)
    enb = jnp.exp(-bcum)
    qs = q_ref[...].astype(F32) * (GLA_DK ** -0.5)
    k = k_ref[...].astype(F32)
    v = v_ref[...]
    q_f = (qs * eb).astype(BF16)
    k_f = (k * enb).astype(BF16)
    q_b = (qs * enb).astype(BF16)
    k_b = (k * eb).astype(BF16)
    k_d = (k * jnp.exp(b_last_full - bcum)).astype(BF16)

    s_f = _dot_nt(q_f, k_f)
    s_b = _dot_nt(q_b, k_b)
    scores = jnp.where(lower, s_f, jnp.where(upper, s_b, 0.0))
    o_intra = _dot(scores.astype(BF16), v)

    st = st_ref[...]
    o_inter = []
    for c in range(nchunk):
        sl = slice(c * CHUNK, (c + 1) * CHUNK)
        o_inter.append(_dot_nt(q_f[sl], st.astype(BF16)))
        delta_t = _dot_tn(v[sl], k_d[sl])
        st = st * jnp.exp(b_last[c]) + delta_t
    st_ref[...] = st

    o = o_intra + jnp.concatenate(o_inter, axis=0)
    ms = jnp.mean(o * o, axis=-1, keepdims=True)
    y = o * lax.rsqrt(ms + EPS) * ng_ref[...]
    g = g_ref[...].astype(F32)
    o_ref[...] = (y * (g * _sigmoid(g))).astype(BF16)


def _gla(proj, wa_pad, b_alpha, gla_norm_g, B, S):
    N = proj.shape[0]
    tc = min(256, S)
    nt = S // tc
    row = lambda b, h, i: b * nt + i
    return pl.pallas_call(
        functools.partial(_gla_kernel, tc=tc),
        grid=(B, GLA_HEADS, nt),
        in_specs=[
            pl.BlockSpec((tc, GLA_DK), lambda b, h, i: (row(b, h, i), COL_GQ // GLA_DK + h)),
            pl.BlockSpec((tc, GLA_DK), lambda b, h, i: (row(b, h, i), COL_GK // GLA_DK + h)),
            pl.BlockSpec((tc, GLA_DV), lambda b, h, i: (row(b, h, i), COL_GV // GLA_DV + h)),
            pl.BlockSpec((tc, GLA_DV), lambda b, h, i: (row(b, h, i), COL_GG // GLA_DV + h)),
            pl.BlockSpec((tc, LANES), lambda b, h, i: (row(b, h, i), COL_ACODE // LANES)),
            pl.BlockSpec((LANES, GLA_DK), lambda b, h, i: (0, h)),
            pl.BlockSpec((1, GLA_DK), lambda b, h, i: (0, h)),
            pl.BlockSpec((1, GLA_DV), lambda b, h, i: (0, 0)),
        ],
        out_specs=pl.BlockSpec((tc, GLA_DV), lambda b, h, i: (row(b, h, i), h)),
        out_shape=jax.ShapeDtypeStruct((N, GLA_HEADS * GLA_DV), BF16),
        scratch_shapes=[pltpu.VMEM((GLA_DV, GLA_DK), F32)],
        compiler_params=pltpu.CompilerParams(
            dimension_semantics=("parallel", "parallel", "arbitrary"), vmem_limit_bytes=VMEM_LIMIT),
        name="gla",
    )(proj, proj, proj, proj, proj, wa_pad, b_alpha, gla_norm_g)


def _group_rms(xf, gsum, gain):
    hi, lo = _split_bf16(xf * xf)
    ss = _dot(hi, gsum) + _dot(lo, gsum)
    return xf * lax.rsqrt(ss * (1.0 / DIFF_DH) + EPS) * gain


def _diff_attn_kernel(q_ref, k_ref, v_ref, qg_ref, kg_ref, slope_ref, lam_ref, sg_ref, o_ref,
                      k0_scr, k1_scr, m_scr, l_scr, acc_scr, *, t, seq):
    i = pl.program_id(2)
    lane = lax.broadcasted_iota(jnp.int32, (1, LANES), 1)
    first_map = lane < DIFF_DH

    gr = lax.broadcasted_iota(jnp.int32, (LANES, LANES), 0) // DIFF_DH
    gc = lax.broadcasted_iota(jnp.int32, (LANES, LANES), 1) // DIFF_DH
    gsum = jnp.where(gr == gc, 1.0, 0.0).astype(BF16)

    @pl.when(i == 0)
    def _():
        def body(r, carry):
            rows = pl.ds(pl.multiple_of(r * t, t), t)
            kn = _group_rms(k_ref[rows, :].astype(F32), gsum, kg_ref[...])
            k0_scr[rows, :] = jnp.where(first_map, kn, 0.0).astype(BF16)
            k1_scr[rows, :] = jnp.where(first_map, 0.0, kn).astype(BF16)
            return carry
        lax.fori_loop(0, seq // t, body, 0)

    qn = _group_rms(q_ref[...].astype(F32), gsum, qg_ref[...]) * (DIFF_DH ** -0.5)
    qn = qn.astype(BF16)

    slope = slope_ref[...]
    slope_t = jnp.tile(slope, (1, t // LANES))
    jl = lax.broadcasted_iota(jnp.int32, (1, t), 1).astype(F32)
    bias_row = slope_t * jl

    m_scr[...] = jnp.full_like(m_scr, MASK_VALUE)
    l_scr[...] = jnp.zeros_like(l_scr)
    acc_scr[...] = jnp.zeros_like(acc_scr)

    def update(mp, s, v):
        m_prev = m_scr[mp]
        m_cur = jnp.max(s, axis=1, keepdims=True)
        m_new = jnp.maximum(m_prev, m_cur)
        p = jnp.exp(s - jnp.tile(m_new, (1, t // LANES)))
        alpha = jnp.exp(m_prev - m_new)
        l_scr[mp] = alpha * l_scr[mp] + jnp.sum(p, axis=1, keepdims=True)
        acc_scr[mp] = alpha * acc_scr[mp] + _dot(p.astype(BF16), v)
        m_scr[mp] = m_new

    def off_diag(j, carry):
        rows = pl.ds(pl.multiple_of(j * t, t), t)
        v = v_ref[rows, :]
        shift = slope_t * ((j - i) * t).astype(F32)
        bias = bias_row + shift
        update(0, _dot_nt(qn, k0_scr[rows, :]) + bias, v)
        update(1, _dot_nt(qn, k1_scr[rows, :]) + bias, v)
        return carry

    lax.fori_loop(0, i, off_diag, 0)

    rows = pl.ds(pl.multiple_of(i * t, t), t)
    il = lax.broadcasted_iota(jnp.int32, (t, t), 0)
    jj = lax.broadcasted_iota(jnp.int32, (t, t), 1)
    dist = (il - jnp.abs(il - jj)).astype(F32)
    bias_d = jnp.where(_chunk_id(jj) <= _chunk_id(il), slope_t * dist, MASK_VALUE)
    v = v_ref[rows, :]
    update(0, _dot_nt(qn, k0_scr[rows, :]) + bias_d, v)
    update(1, _dot_nt(qn, k1_scr[rows, :]) + bias_d, v)

    o0 = acc_scr[0] / l_scr[0]
    o1 = acc_scr[1] / l_scr[1]
    o = o0 - lam_ref[...] * o1
    ms = jnp.mean(o * o, axis=-1, keepdims=True)
    y = o * lax.rsqrt(ms + EPS) * sg_ref[...] * (1.0 - LAMBDA_INIT)
    o_ref[...] = y.astype(BF16)


def _diff_attn(proj, qg2, kg2, slopes, lam, subln_g, B, S):
    N = proj.shape[0]
    t = min(256, S)
    nq = S // t
    vec = pl.BlockSpec((1, LANES), lambda b, h, i: (0, 0))
    return pl.pallas_call(
        functools.partial(_diff_attn_kernel, t=t, seq=S),
        grid=(B, DIFF_HEADS, nq),
        in_specs=[
            pl.BlockSpec((t, LANES), lambda b, h, i: (b * nq + i, COL_DQ // LANES + h)),
            pl.BlockSpec((S, LANES), lambda b, h, i: (b, COL_DK // LANES + h)),
            pl.BlockSpec((S, LANES), lambda b, h, i: (b, COL_DV // LANES + h)),
            vec, vec,
            pl.BlockSpec((None, 1, LANES), lambda b, h, i: (h, 0, 0)),
            vec, vec,
        ],
        out_specs=pl.BlockSpec((t, DIFF_DV), lambda b, h, i: (b * nq + i, h)),
        out_shape=jax.ShapeDtypeStruct((N, DIFF_HEADS * DIFF_DV), BF16),
        scratch_shapes=[
            pltpu.VMEM((S, LANES), BF16),
            pltpu.VMEM((S, LANES), BF16),
            pltpu.VMEM((2, t, LANES), F32),
            pltpu.VMEM((2, t, LANES), F32),
            pltpu.VMEM((2, t, DIFF_DV), F32),
        ],
        compiler_params=pltpu.CompilerParams(
            dimension_semantics=("parallel", "parallel", "arbitrary"), vmem_limit_bytes=VMEM_LIMIT),
        name="diff_attn",
    )(proj, proj, proj, qg2, kg2, slopes, lam, subln_g)


def _merge_kernel(oa_ref, ob_ref, ga_ref, gb_ref, x_ref, mod_ref, wa_ref, wb_ref, wo_ref, g2_ref,
                  x1_ref, h2_ref):
    ya = _dot(oa_ref[...], wa_ref[...])
    yb = _dot(ob_ref[...], wb_ref[...])
    merged = _sigmoid(ga_ref[...].astype(F32)) * ya + _sigmoid(gb_ref[...].astype(F32)) * yb
    x1 = x_ref[...] + mod_ref[2:3, :] * _dot(merged.astype(BF16), wo_ref[...])
    x1_ref[...] = x1
    ms = jnp.mean(x1 * x1, axis=-1, keepdims=True)
    y = x1 * lax.rsqrt(ms + EPS) * g2_ref[...]
    h2_ref[...] = (y * (1.0 + mod_ref[4:5, :]) + mod_ref[3:4, :]).astype(BF16)


def _merge(o_a, o_b, proj, x2, mod3, w_gla_o, w_diff_o, w_out, g2, S):
    N, D = x2.shape
    tm = min(512, S)
    tiles_per_seq = S // tm
    tok = lambda col: pl.BlockSpec((tm, D), lambda i: (i, col))
    wspec = pl.BlockSpec((D, D), lambda i: (0, 0))
    return pl.pallas_call(
        _merge_kernel,
        grid=(N // tm,),
        in_specs=[
            tok(0), tok(0), tok(COL_GA // D), tok(COL_GB // D), tok(0),
            pl.BlockSpec((None, 6, D), lambda i: (i // tiles_per_seq, 0, 0)),
            wspec, wspec, wspec,
            pl.BlockSpec((1, D), lambda i: (0, 0)),
        ],
        out_specs=[tok(0), tok(0)],
        out_shape=[jax.ShapeDtypeStruct((N, D), F32), jax.ShapeDtypeStruct((N, D), BF16)],
        compiler_params=pltpu.CompilerParams(
            dimension_semantics=("parallel",), vmem_limit_bytes=VMEM_LIMIT),
        name="merge",
    )(o_a, o_b, proj, proj, x2, mod3, w_gla_o, w_diff_o, w_out, g2)


def _conv_ffn_kernel(h2_ref, x1_ref, mod_ref, wup_ref, cw_ref, cb_ref, wdn_ref, o_ref,
                     carry_ref, acc_ref, *, tm, tf, tiles_per_seq):
    @pl.when(pl.program_id(0) % tiles_per_seq == 0)
    def _():
        carry_ref[...] = jnp.zeros_like(carry_ref)

    h2 = h2_ref[...]
    row = lax.broadcasted_iota(jnp.int32, (tm, tf), 0)

    def conv(col0):
        u = _dot(h2, wup_ref[:, col0:col0 + tf])
        prev = carry_ref[:, col0:col0 + tf]
        p1 = prev[SUBLANES - 1:SUBLANES, :]
        p2 = prev[SUBLANES - 2:SUBLANES - 1, :]
        r1 = jnp.where(row == 0, p1, pltpu.roll(u, 1, axis=0))
        r2 = jnp.where(row == 0, p2, jnp.where(row == 1, p1, pltpu.roll(u, 2, axis=0)))
        carry_ref[:, col0:col0 + tf] = u[tm - SUBLANES:, :]
        cw = cw_ref[:, col0:col0 + tf]
        return cw[0:1, :] * r2 + cw[1:2, :] * r1 + cw[2:3, :] * u + cb_ref[:, col0:col0 + tf]

    for f in range(D_FF // tf):
        a = conv(f * tf)
        b = conv(D_FF + f * tf)
        act = (a * _sigmoid(a) * b).astype(BF16)
        contrib = _dot(act, wdn_ref[f * tf:(f + 1) * tf, :])
        if f == 0:
            acc_ref[...] = contrib
        else:
            acc_ref[...] += contrib

    o_ref[...] = x1_ref[...] + mod_ref[5:6, :] * acc_ref[...]


def _conv_ffn(h2, x1, mod3, w_up, conv_w, conv_b, w_down, S):
    N, D = x1.shape
    tm = min(512, S)
    tf = 256
    tiles_per_seq = S // tm
    tok = pl.BlockSpec((tm, D), lambda i: (i, 0))
    full = lambda a: pl.BlockSpec(a.shape, lambda i: (0, 0))
    return pl.pallas_call(
        functools.partial(_conv_ffn_kernel, tm=tm, tf=tf, tiles_per_seq=tiles_per_seq),
        grid=(N // tm,),
        in_specs=[
            tok, tok,
            pl.BlockSpec((None, 6, D), lambda i: (i // tiles_per_seq, 0, 0)),
            full(w_up), full(conv_w), full(conv_b), full(w_down),
        ],
        out_specs=tok,
        out_shape=jax.ShapeDtypeStruct((N, D), F32),
        scratch_shapes=[pltpu.VMEM((SUBLANES, 2 * D_FF), F32), pltpu.VMEM((tm, D), F32)],
        compiler_params=pltpu.CompilerParams(
            dimension_semantics=("arbitrary",), vmem_limit_bytes=VMEM_LIMIT),
        name="conv_ffn",
    )(h2, x1, mod3, w_up, conv_w, conv_b, w_down)


def kernel(x, c, w_ada, b_ada, norm1_g, w_in, w_alpha_up, b_alpha, gla_norm_g, q_norm_g, k_norm_g,
           lam_q1, lam_k1, lam_q2, lam_k2, diff_norm_g, w_gla_o, w_diff_o, w_out, norm2_g, w_up,
           conv_w, conv_b, w_down):
    B, S, D = x.shape
    N = B * S
    assert D == D_MODEL and S % CHUNK == 0 and w_ada.shape[0] == 1

    w0 = w_in[0]
    a0 = COL_GG + GLA_HEADS * GLA_DV
    w_in_r = jnp.concatenate(
        [w0[:, :a0], w0[:, a0 + GLA_RANK:],
         jnp.pad(w0[:, a0:a0 + GLA_RANK], ((0, 0), (0, LANES - GLA_RANK)))], axis=1).astype(BF16)
    wa_pad = jnp.pad(w_alpha_up[0], ((0, LANES - GLA_RANK), (0, 0))).astype(BF16)
    slopes = 2.0 ** (-8.0 * (jnp.arange(DIFF_HEADS, dtype=F32) + 1.0) / DIFF_HEADS)
    slopes = jnp.broadcast_to(slopes[:, None, None], (DIFF_HEADS, 1, LANES))
    qg2 = jnp.tile(q_norm_g, (1, 2))
    kg2 = jnp.tile(k_norm_g, (1, 2))

    mod, lam = _ada_mod(c, w_ada[0], b_ada, lam_q1, lam_k1, lam_q2, lam_k2)
    mod3 = mod.reshape(B, 6, D)
    x2 = x.reshape(N, D)

    proj = _in_proj(x2, mod3, norm1_g, w_in_r, S)
    o_a = _gla(proj, wa_pad, b_alpha, gla_norm_g, B, S)
    o_b = _diff_attn(proj, qg2, kg2, slopes, lam, diff_norm_g, B, S)
    x1, h2 = _merge(o_a, o_b, proj, x2, mod3, w_gla_o[0].astype(BF16), w_diff_o[0].astype(BF16),
                    w_out[0].astype(BF16), norm2_g, S)
    out = _conv_ffn(h2, x1, mod3, w_up[0].astype(BF16), conv_w[0], conv_b, w_down[0].astype(BF16), S)
    return out.reshape(B, S, D)
```

```python
import functools
import math

import jax
import jax.numpy as jnp
from jax import lax
from jax.experimental import pallas as pl
from jax.experimental.pallas import tpu as pltpu

F32 = jnp.float32
BF16 = jnp.bfloat16

D_MODEL = 1024
CHUNK = 64
EPS = 1e-6
GLA_HEADS = 4
GLA_DK = 128
GLA_DV = 256
GLA_RANK = 16
GLA_TAU = 16.0
DIFF_HEADS = 8
DIFF_DH = 64
DIFF_DV = 128
D_FF = 2816
CONV_W = 3
LAMBDA_INIT = 0.8 - 0.6 * math.exp(-0.3 * 0)

LANES = 128
SUBLANES = 8
MASK_VALUE = -1e30
LOG2E = math.log2(math.e)
BIAS_TERMS = 3

COL_GQ, COL_GK, COL_GV, COL_GG = 0, 512, 1024, 2048
COL_DQ, COL_DK, COL_DV = 3072, 4096, 5120
COL_GA, COL_GB, COL_ACODE = 6144, 7168, 8192
PROJ_COLS = 8192 + LANES
PROJ_TN = PROJ_COLS // 5

VMEM_LIMIT = 56 * 1024 * 1024


def _dot(a, b):
    return jnp.dot(a, b, preferred_element_type=F32)


def _dot_nt(a, b):
    return lax.dot_general(a, b, (((1,), (1,)), ((), ())), preferred_element_type=F32)


def _dot_tn(a, b):
    return lax.dot_general(a, b, (((0,), (0,)), ((), ())), preferred_element_type=F32)


def _split_bf16(v):
    hi = v.astype(BF16)
    lo = (v - hi.astype(F32)).astype(BF16)
    return hi, lo


def _sigmoid(v):
    return 1.0 / (1.0 + jnp.exp(-v))


def _chunk_id(pos):
    return lax.shift_right_logical(pos, CHUNK.bit_length() - 1)


def _ada_kernel(c_ref, w_ref, b_ref, lq1_ref, lk1_ref, lq2_ref, lk2_ref, mod_ref, lam_ref):
    c = c_ref[...]
    a = c * _sigmoid(c)
    a_hi, a_lo = _split_bf16(a)
    w_hi, w_lo = _split_bf16(w_ref[...])
    mod_ref[...] = _dot(a_hi, w_hi) + _dot(a_lo, w_hi) + _dot(a_hi, w_lo) + b_ref[...]
    s1 = jnp.sum(lq1_ref[...] * lk1_ref[...], axis=-1, keepdims=True)
    s2 = jnp.sum(lq2_ref[...] * lk2_ref[...], axis=-1, keepdims=True)
    lam = jnp.exp(s1) - jnp.exp(s2) + LAMBDA_INIT
    lam_ref[...] = jnp.broadcast_to(lam, lam_ref.shape)


def _ada_mod(c, w_ada, b_ada, lq1, lk1, lq2, lk2):
    B, D = c.shape
    n_out = w_ada.shape[1]
    tn = D
    small = pl.BlockSpec((1, DIFF_DH), lambda j: (0, 0))
    return pl.pallas_call(
        _ada_kernel,
        grid=(n_out // tn,),
        in_specs=[
            pl.BlockSpec((B, D), lambda j: (0, 0)),
            pl.BlockSpec((D, tn), lambda j: (0, j)),
            pl.BlockSpec((1, tn), lambda j: (0, j)),
            small, small, small, small,
        ],
        out_specs=[
            pl.BlockSpec((B, tn), lambda j: (0, j)),
            pl.BlockSpec((1, LANES), lambda j: (0, 0)),
        ],
        out_shape=[
            jax.ShapeDtypeStruct((B, n_out), F32),
            jax.ShapeDtypeStruct((1, LANES), F32),
        ],
        compiler_params=pltpu.CompilerParams(dimension_semantics=("arbitrary",)),
        name="ada_mod",
    )(c, w_ada, b_ada, lq1, lk1, lq2, lk2)


def _in_proj_kernel(x_ref, mod_ref, g_ref, w_ref, o_ref, h_scr):
    @pl.when(pl.program_id(1) == 0)
    def _():
        x = x_ref[...]
        ms = jnp.mean(x * x, axis=-1, keepdims=True)
        y = x * lax.rsqrt(ms + EPS) * g_ref[...]
        h = y * (1.0 + mod_ref[1:2, :]) + mod_ref[0:1, :]
        h_scr[...] = h.astype(BF16)

    o_ref[...] = _dot(h_scr[...], w_ref[...]).astype(BF16)


def _in_proj(x2, mod3, g1, w_in_r, S):
    N, D = x2.shape
    tm = min(1024, S)
    tiles_per_seq = S // tm
    return pl.pallas_call(
        _in_proj_kernel,
        grid=(N // tm, PROJ_COLS // PROJ_TN),
        in_specs=[
            pl.BlockSpec((tm, D), lambda i, j: (i, 0)),
            pl.BlockSpec((None, 6, D), lambda i, j: (i // tiles_per_seq, 0, 0)),
            pl.BlockSpec((1, D), lambda i, j: (0, 0)),
            pl.BlockSpec((D, PROJ_TN), lambda i, j: (0, j)),
        ],
        out_specs=pl.BlockSpec((tm, PROJ_TN), lambda i, j: (i, j)),
        out_shape=jax.ShapeDtypeStruct((N, PROJ_COLS), BF16),
        scratch_shapes=[pltpu.VMEM((tm, D), BF16)],
        compiler_params=pltpu.CompilerParams(
            dimension_semantics=("parallel", "arbitrary"), vmem_limit_bytes=VMEM_LIMIT),
        name="in_proj",
    )(x2, mod3, g1, w_in_r)


def _gla_kernel(q_ref, k_ref, v_ref, g_ref, a_ref, wa_ref, ba_ref, ng_ref, o_ref, st_ref, *, tc):
    nchunk = tc // CHUNK

    @pl.when(pl.program_id(2) == 0)
    def _():
        st_ref[...] = jnp.zeros_like(st_ref)

    z = _dot(a_ref[...], wa_ref[...]) + ba_ref[...]
    log_a = (jnp.minimum(z, 0.0) - jnp.log1p(jnp.exp(-jnp.abs(z)))) * (1.0 / GLA_TAU)

    row = lax.broadcasted_iota(jnp.int32, (tc, tc), 0)
    col = lax.broadcasted_iota(jnp.int32, (tc, tc), 1)
    same = _chunk_id(row) == _chunk_id(col)
    lower = same & (col <= row)
    upper = same & (col > row)
    tri = jnp.where(lower, 1.0, 0.0).astype(BF16)

    la_hi, la_lo = _split_bf16(log_a)
    bcum = _dot(tri, la_hi) + _dot(tri, la_lo)
    b_last = [bcum[c * CHUNK + CHUNK - 1:(c + 1) * CHUNK, :] for c in range(nchunk)]
    b_last_full = jnp.concatenate(
        [jnp.broadcast_to(bl, (CHUNK, GLA_DK)) for bl in b_last], axis=0)

    eb = jnp.exp(bcum)
    enb = jnp.exp(-bcum)
    qs = q_ref[...].astype(F32) * (GLA_DK ** -0.5)
    k = k_ref[...].astype(F32)
    v = v_ref[...]
    q_f = (qs * eb).astype(BF16)
    k_f = (k * enb).astype(BF16)
    q_b = (qs * enb).astype(BF16)
    k_b = (k * eb).astype(BF16)
    k_d = (k * jnp.exp(b_last_full - bcum)).astype(BF16)

    s_f = _dot_nt(q_f, k_f)
    s_b = _dot_nt(q_b, k_b)
    scores = jnp.where(lower, s_f, jnp.where(upper, s_b, 0.0))
    o_intra = _dot(scores.astype(BF16), v)

    st = st_ref[...]
    o_inter = []
    for c in range(nchunk):
        sl = slice(c * CHUNK, (c + 1) * CHUNK)
        o_inter.append(_dot_nt(q_f[sl], st.astype(BF16)))
        delta_t = _dot_tn(v[sl], k_d[sl])
        st = st * jnp.exp(b_last[c]) + delta_t
    st_ref[...] = st

    o = o_intra + jnp.concatenate(o_inter, axis=0)
    ms = jnp.mean(o * o, axis=-1, keepdims=True)
    y = o * lax.rsqrt(ms + EPS) * ng_ref[...]
    g = g_ref[...].astype(F32)
    o_ref[...] = (y * (g * _sigmoid(g))).astype(BF16)


def _gla(proj, wa_pad, b_alpha, gla_norm_g, B, S):
    N = proj.shape[0]
    tc = min(256, S)
    nt = S // tc
    row = lambda b, h, i: b * nt + i
    return pl.pallas_call(
        functools.partial(_gla_kernel, tc=tc),
        grid=(B, GLA_HEADS, nt),
        in_specs=[
            pl.BlockSpec((tc, GLA_DK), lambda b, h, i: (row(b, h, i), COL_GQ // GLA_DK + h)),
            pl.BlockSpec((tc, GLA_DK), lambda b, h, i: (row(b, h, i), COL_GK // GLA_DK + h)),
            pl.BlockSpec((tc, GLA_DV), lambda b, h, i: (row(b, h, i), COL_GV // GLA_DV + h)),
            pl.BlockSpec((tc, GLA_DV), lambda b, h, i: (row(b, h, i), COL_GG // GLA_DV + h)),
            pl.BlockSpec((tc, LANES), lambda b, h, i: (row(b, h, i), COL_ACODE // LANES)),
            pl.BlockSpec((LANES, GLA_DK), lambda b, h, i: (0, h)),
            pl.BlockSpec((1, GLA_DK), lambda b, h, i: (0, h)),
            pl.BlockSpec((1, GLA_DV), lambda b, h, i: (0, 0)),
        ],
        out_specs=pl.BlockSpec((tc, GLA_DV), lambda b, h, i: (row(b, h, i), h)),
        out_shape=jax.ShapeDtypeStruct((N, GLA_HEADS * GLA_DV), BF16),
        scratch_shapes=[pltpu.VMEM((GLA_DV, GLA_DK), F32)],
        compiler_params=pltpu.CompilerParams(
            dimension_semantics=("parallel", "parallel", "arbitrary"), vmem_limit_bytes=VMEM_LIMIT),
        name="gla",
    )(proj, proj, proj, proj, proj, wa_pad, b_alpha, gla_norm_g)


def _group_rms(xf, gsum, gain):
    hi, lo = _split_bf16(xf * xf)
    ss = _dot(hi, gsum) + _dot(lo, gsum)
    return xf * lax.rsqrt(ss * (1.0 / DIFF_DH) + EPS) * gain


def _diff_attn_kernel(q_ref, k_ref, v_ref, qg_ref, kg_ref, slope_ref, lam_ref, sg_ref, o_ref,
                      ka_scr, vt_scr, wt_scr, acc_scr, s_scr, p_scr, dm_scr, *, t, seq):
    i = pl.program_id(2)
    nqc = t // LANES
    lane = lax.broadcasted_iota(jnp.int32, (1, LANES), 1)
    first_map = lane < DIFF_DH

    gr = lax.broadcasted_iota(jnp.int32, (LANES, LANES), 0) // DIFF_DH
    gc = lax.broadcasted_iota(jnp.int32, (LANES, LANES), 1) // DIFF_DH
    gsum = jnp.where(gr == gc, 1.0, 0.0).astype(BF16)
    slope2 = slope_ref[...] * LOG2E
    width = nqc * 2 * LANES

    @pl.when(i == 0)
    def _():
        jl = lax.broadcasted_iota(jnp.int32, (t, LANES), 0).astype(F32)
        lane_t = lax.broadcasted_iota(jnp.int32, (t, LANES), 1)
        rem = slope2 * jl
        aux = jnp.zeros((t, LANES), F32)
        for term in range(BIAS_TERMS):
            part = rem.astype(BF16).astype(F32)
            aux = jnp.where(lane_t == term, part, aux)
            rem = rem - part
        aux = aux.astype(BF16)

        jd = lax.broadcasted_iota(jnp.int32, (t, width), 0)
        col = lax.broadcasted_iota(jnp.int32, (t, width), 1)
        il = (col & (LANES - 1)) + lax.shift_right_logical(col, 8) * LANES
        ahead = jnp.minimum(il - jd, 0).astype(F32)
        dm_scr[...] = jnp.where(_chunk_id(jd) <= _chunk_id(il),
                                (2.0 * jnp.tile(slope2, (1, width // LANES))) * ahead, MASK_VALUE)

        def body(r, carry):
            rows = pl.ds(pl.multiple_of(r * t, t), t)
            kn = _group_rms(k_ref[rows, :].astype(F32), gsum, kg_ref[...])
            ka_scr[rows, 0:LANES] = kn.astype(BF16)
            ka_scr[rows, LANES:2 * LANES] = aux
            vt_scr[r] = v_ref[rows, :].astype(F32).T.astype(BF16)
            return carry
        lax.fori_loop(0, seq // t, body, 0)

    qn = _group_rms(q_ref[...].astype(F32), gsum, qg_ref[...]) * (DIFF_DH ** -0.5 * LOG2E)
    ones = jnp.broadcast_to(jnp.where(lane < BIAS_TERMS, 1.0, 0.0), (LANES, LANES))
    for qc in range(nqc):
        qq = qn[qc * LANES:(qc + 1) * LANES, :]
        w = jnp.concatenate(
            [jnp.concatenate([jnp.where(first_map, qq, 0.0), ones], axis=1),
             jnp.concatenate([jnp.where(first_map, 0.0, qq), ones], axis=1)], axis=0)
        wt_scr[:, qc * 2 * LANES:(qc + 1) * 2 * LANES] = w.T.astype(BF16)
    acc_scr[...] = jnp.zeros_like(acc_scr)
    p_scr[1] = jnp.zeros(p_scr.shape[1:], BF16)
    slope2_w = jnp.tile(slope2, (1, width // LANES))

    def scores(j, slot):
        rows = pl.ds(pl.multiple_of(j * t, t), t)
        st = _dot(ka_scr[rows, :], wt_scr[...])
        s_scr[slot] = st
        return jnp.max(st, axis=0, keepdims=True)

    def accumulate(j, slot, alpha):
        acc_scr[...] = alpha * acc_scr[...] + _dot(vt_scr[jnp.maximum(j, 0)], p_scr[slot])

    def softmax(j, st, m_cur, m_prev, l_prev):
        shift = slope2_w * ((j - i) * t).astype(F32)
        m_new = jnp.maximum(m_prev, m_cur + shift)
        p = jnp.exp2(st - (m_new - shift))
        alpha = jnp.exp2(m_prev - m_new)
        l_new = alpha * l_prev + jnp.sum(p, axis=0, keepdims=True)
        return p.astype(BF16), alpha, m_new, l_new

    def step(j, slot, carry):
        m, l, m_cur, alpha_prev = carry
        m_next = scores(j + 1, 1 - slot)
        accumulate(j - 1, 1 - slot, alpha_prev)
        p, alpha, m, l = softmax(j, s_scr[slot], m_cur, m, l)
        p_scr[slot] = p
        return m, l, m_next, alpha

    def finish(slot, carry):
        m, l, _, alpha_prev = carry
        accumulate(i - 1, 1 - slot, alpha_prev)
        st = s_scr[slot] + dm_scr[...]
        p, alpha, m, l = softmax(i, st, jnp.max(st, axis=0, keepdims=True), m, l)
        on = (alpha * acc_scr[...] + _dot(vt_scr[i], p)) * (1.0 / l)
        for qc in range(nqc):
            c0 = qc * 2 * LANES
            ot = on[:, c0:c0 + LANES] - lam_ref[...] * on[:, c0 + LANES:c0 + 2 * LANES]
            o = ot.T
            msq = jnp.mean(o * o, axis=-1, keepdims=True)
            y = o * lax.rsqrt(msq + EPS) * sg_ref[...] * (1.0 - LAMBDA_INIT)
            o_ref[qc * LANES:(qc + 1) * LANES, :] = y.astype(BF16)

    init = (jnp.full((1, width), MASK_VALUE, F32), jnp.zeros((1, width), F32),
            scores(0, 0), jnp.ones((1, width), F32))
    carry = lax.fori_loop(0, i // 2, lambda jj, c: step(2 * jj + 1, 1, step(2 * jj, 0, c)), init)

    @pl.when(i % 2 == 0)
    def _():
        finish(0, carry)

    @pl.when(i % 2 == 1)
    def _():
        finish(1, step(i - 1, 0, carry))


def _diff_attn(proj, qg2, kg2, slopes, lam, subln_g, B, S):
    N = proj.shape[0]
    t = min(256, S)
    nq = S // t
    vec = pl.BlockSpec((1, LANES), lambda b, h, i: (0, 0))
    return pl.pallas_call(
        functools.partial(_diff_attn_kernel, t=t, seq=S),
        grid=(B, DIFF_HEADS, nq),
        in_specs=[
            pl.BlockSpec((t, LANES), lambda b, h, i: (b * nq + i, COL_DQ // LANES + h)),
            pl.BlockSpec((S, LANES), lambda b, h, i: (b, COL_DK // LANES + h)),
            pl.BlockSpec((S, LANES), lambda b, h, i: (b, COL_DV // LANES + h)),
            vec, vec,
            pl.BlockSpec((None, 1, LANES), lambda b, h, i: (h, 0, 0)),
            vec, vec,
        ],
        out_specs=pl.BlockSpec((t, DIFF_DV), lambda b, h, i: (b * nq + i, h)),
        out_shape=jax.ShapeDtypeStruct((N, DIFF_HEADS * DIFF_DV), BF16),
        scratch_shapes=[
            pltpu.VMEM((S, 2 * LANES), BF16),
            pltpu.VMEM((S // t, DIFF_DV, t), BF16),
            pltpu.VMEM((2 * LANES, 2 * t), BF16),
            pltpu.VMEM((DIFF_DV, 2 * t), F32),
            pltpu.VMEM((2, t, 2 * t), F32),
            pltpu.VMEM((2, t, 2 * t), BF16),
            pltpu.VMEM((t, 2 * t), F32),
        ],
        compiler_params=pltpu.CompilerParams(
            dimension_semantics=("parallel", "parallel", "arbitrary"), vmem_limit_bytes=VMEM_LIMIT),
        name="diff_attn",
    )(proj, proj, proj, qg2, kg2, slopes, lam, subln_g)


def _merge_kernel(oa_ref, ob_ref, ga_ref, gb_ref, x_ref, mod_ref, wa_ref, wb_ref, wo_ref, g2_ref,
                  x1_ref, h2_ref):
    ya = _dot(oa_ref[...], wa_ref[...])
    yb = _dot(ob_ref[...], wb_ref[...])
    merged = _sigmoid(ga_ref[...].astype(F32)) * ya + _sigmoid(gb_ref[...].astype(F32)) * yb
    x1 = x_ref[...] + mod_ref[2:3, :] * _dot(merged.astype(BF16), wo_ref[...])
    x1_ref[...] = x1
    ms = jnp.mean(x1 * x1, axis=-1, keepdims=True)
    y = x1 * lax.rsqrt(ms + EPS) * g2_ref[...]
    h2_ref[...] = (y * (1.0 + mod_ref[4:5, :]) + mod_ref[3:4, :]).astype(BF16)


def _merge(o_a, o_b, proj, x2, mod3, w_gla_o, w_diff_o, w_out, g2, S):
    N, D = x2.shape
    tm = min(512, S)
    tiles_per_seq = S // tm
    tok = lambda col: pl.BlockSpec((tm, D), lambda i: (i, col))
    wspec = pl.BlockSpec((D, D), lambda i: (0, 0))
    return pl.pallas_call(
        _merge_kernel,
        grid=(N // tm,),
        in_specs=[
            tok(0), tok(0), tok(COL_GA // D), tok(COL_GB // D), tok(0),
            pl.BlockSpec((None, 6, D), lambda i: (i // tiles_per_seq, 0, 0)),
            wspec, wspec, wspec,
            pl.BlockSpec((1, D), lambda i: (0, 0)),
        ],
        out_specs=[tok(0), tok(0)],
        out_shape=[jax.ShapeDtypeStruct((N, D), F32), jax.ShapeDtypeStruct((N, D), BF16)],
        compiler_params=pltpu.CompilerParams(
            dimension_semantics=("parallel",), vmem_limit_bytes=VMEM_LIMIT),
        name="merge",
    )(o_a, o_b, proj, proj, x2, mod3, w_gla_o, w_diff_o, w_out, g2)


def _conv_ffn_kernel(h2_ref, x1_ref, mod_ref, wup_ref, cw_ref, cb_ref, wdn_ref, o_ref,
                     carry_ref, acc_ref, *, tm, tf, tiles_per_seq):
    @pl.when(pl.program_id(0) % tiles_per_seq == 0)
    def _():
        carry_ref[...] = jnp.zeros_like(carry_ref)

    h2 = h2_ref[...]
    row = lax.broadcasted_iota(jnp.int32, (tm, tf), 0)

    def conv(col0):
        u = _dot(h2, wup_ref[:, col0:col0 + tf])
        prev = carry_ref[:, col0:col0 + tf]
        p1 = prev[SUBLANES - 1:SUBLANES, :]
        p2 = prev[SUBLANES - 2:SUBLANES - 1, :]
        r1 = jnp.where(row == 0, p1, pltpu.roll(u, 1, axis=0))
        r2 = jnp.where(row == 0, p2, jnp.where(row == 1, p1, pltpu.roll(u, 2, axis=0)))
        carry_ref[:, col0:col0 + tf] = u[tm - SUBLANES:, :]
        cw = cw_ref[:, col0:col0 + tf]
        return cw[0:1, :] * r2 + cw[1:2, :] * r1 + cw[2:3, :] * u + cb_ref[:, col0:col0 + tf]

    for f in range(D_FF // tf):
        a = conv(f * tf)
        b = conv(D_FF + f * tf)
        act = (a * _sigmoid(a) * b).astype(BF16)
        contrib = _dot(act, wdn_ref[f * tf:(f + 1) * tf, :])
        if f == 0:
            acc_ref[...] = contrib
        else:
            acc_ref[...] += contrib

    o_ref[...] = x1_ref[...] + mod_ref[5:6, :] * acc_ref[...]


def _conv_ffn(h2, x1, mod3, w_up, conv_w, conv_b, w_down, S):
    N, D = x1.shape
    tm = min(512, S)
    tf = 256
    tiles_per_seq = S // tm
    tok = pl.BlockSpec((tm, D), lambda i: (i, 0))
    full = lambda a: pl.BlockSpec(a.shape, lambda i: (0, 0))
    return pl.pallas_call(
        functools.partial(_conv_ffn_kernel, tm=tm, tf=tf, tiles_per_seq=tiles_per_seq),
        grid=(N // tm,),
        in_specs=[
            tok, tok,
            pl.BlockSpec((None, 6, D), lambda i: (i // tiles_per_seq, 0, 0)),
            full(w_up), full(conv_w), full(conv_b), full(w_down),
        ],
        out_specs=tok,
        out_shape=jax.ShapeDtypeStruct((N, D), F32),
        scratch_shapes=[pltpu.VMEM((SUBLANES, 2 * D_FF), F32), pltpu.VMEM((tm, D), F32)],
        compiler_params=pltpu.CompilerParams(
            dimension_semantics=("arbitrary",), vmem_limit_bytes=VMEM_LIMIT),
        name="conv_ffn",
    )(h2, x1, mod3, w_up, conv_w, conv_b, w_down)


def kernel(x, c, w_ada, b_ada, norm1_g, w_in, w_alpha_up, b_alpha, gla_norm_g, q_norm_g, k_norm_g,
           lam_q1, lam_k1, lam_q2, lam_k2, diff_norm_g, w_gla_o, w_diff_o, w_out, norm2_g, w_up,
           conv_w, conv_b, w_down):
    B, S, D = x.shape
    N = B * S
    assert D == D_MODEL and S % CHUNK == 0 and w_ada.shape[0] == 1

    w0 = w_in[0]
    a0 = COL_GG + GLA_HEADS * GLA_DV
    w_in_r = jnp.concatenate(
        [w0[:, :a0], w0[:, a0 + GLA_RANK:],
         jnp.pad(w0[:, a0:a0 + GLA_RANK], ((0, 0), (0, LANES - GLA_RANK)))], axis=1).astype(BF16)
    wa_pad = jnp.pad(w_alpha_up[0], ((0, LANES - GLA_RANK), (0, 0))).astype(BF16)
    slopes = 2.0 ** (-8.0 * (jnp.arange(DIFF_HEADS, dtype=F32) + 1.0) / DIFF_HEADS)
    slopes = jnp.broadcast_to(slopes[:, None, None], (DIFF_HEADS, 1, LANES))
    qg2 = jnp.tile(q_norm_g, (1, 2))
    kg2 = jnp.tile(k_norm_g, (1, 2))

    mod, lam = _ada_mod(c, w_ada[0], b_ada, lam_q1, lam_k1, lam_q2, lam_k2)
    mod3 = mod.reshape(B, 6, D)
    x2 = x.reshape(N, D)

    proj = _in_proj(x2, mod3, norm1_g, w_in_r, S)
    o_a = _gla(proj, wa_pad, b_alpha, gla_norm_g, B, S)
    o_b = _diff_attn(proj, qg2, kg2, slopes, lam, diff_norm_g, B, S)
    x1, h2 = _merge(o_a, o_b, proj, x2, mod3, w_gla_o[0].astype(BF16), w_diff_o[0].astype(BF16),
                    w_out[0].astype(BF16), norm2_g, S)
    out = _conv_ffn(h2, x1, mod3, w_up[0].astype(BF16), conv_w[0], conv_b, w_down[0].astype(BF16), S)
    return out.reshape(B, S, D)
```

```python
import functools
import math

import jax
import jax.numpy as jnp
from jax import lax
from jax.experimental import pallas as pl
from jax.experimental.pallas import tpu as pltpu

F32 = jnp.float32
BF16 = jnp.bfloat16

D_MODEL = 1024
CHUNK = 64
EPS = 1e-6
GLA_HEADS = 4
GLA_DK = 128
GLA_DV = 256
GLA_RANK = 16
GLA_TAU = 16.0
DIFF_HEADS = 8
DIFF_DH = 64
DIFF_DV = 128
D_FF = 2816
CONV_W = 3
LAMBDA_INIT = 0.8 - 0.6 * math.exp(-0.3 * 0)

LANES = 128
SUBLANES = 8
MASK_VALUE = -1e30
LOG2E = math.log2(math.e)
BIAS_TERMS = 3
ATTN_CHAINS = 4
LOGIT_BOUND_SLACK = 1.02
MAX_FIXED_BOUND = 50.0

COL_GQ, COL_GK, COL_GV, COL_GG = 0, 512, 1024, 2048
COL_DQ, COL_DK, COL_DV = 3072, 4096, 5120
COL_GA, COL_GB, COL_ACODE = 6144, 7168, 8192
PROJ_COLS = 8192 + LANES
PROJ_TN = PROJ_COLS // 5

VMEM_LIMIT = 56 * 1024 * 1024


def _dot(a, b):
    return jnp.dot(a, b, preferred_element_type=F32)


def _dot_nt(a, b):
    return lax.dot_general(a, b, (((1,), (1,)), ((), ())), preferred_element_type=F32)


def _dot_tn(a, b):
    return lax.dot_general(a, b, (((0,), (0,)), ((), ())), preferred_element_type=F32)


def _split_bf16(v):
    hi = v.astype(BF16)
    lo = (v - hi.astype(F32)).astype(BF16)
    return hi, lo


def _sigmoid(v):
    return 1.0 / (1.0 + jnp.exp(-v))


def _chunk_id(pos):
    return lax.shift_right_logical(pos, CHUNK.bit_length() - 1)


def _ada_kernel(c_ref, w_ref, b_ref, lq1_ref, lk1_ref, lq2_ref, lk2_ref, mod_ref, lam_ref):
    c = c_ref[...]
    a = c * _sigmoid(c)
    a_hi, a_lo = _split_bf16(a)
    w_hi, w_lo = _split_bf16(w_ref[...])
    mod_ref[...] = _dot(a_hi, w_hi) + _dot(a_lo, w_hi) + _dot(a_hi, w_lo) + b_ref[...]
    s1 = jnp.sum(lq1_ref[...] * lk1_ref[...], axis=-1, keepdims=True)
    s2 = jnp.sum(lq2_ref[...] * lk2_ref[...], axis=-1, keepdims=True)
    lam = jnp.exp(s1) - jnp.exp(s2) + LAMBDA_INIT
    lam_ref[...] = jnp.broadcast_to(lam, lam_ref.shape)


def _ada_mod(c, w_ada, b_ada, lq1, lk1, lq2, lk2):
    B, D = c.shape
    n_out = w_ada.shape[1]
    tn = D
    small = pl.BlockSpec((1, DIFF_DH), lambda j: (0, 0))
    return pl.pallas_call(
        _ada_kernel,
        grid=(n_out // tn,),
        in_specs=[
            pl.BlockSpec((B, D), lambda j: (0, 0)),
            pl.BlockSpec((D, tn), lambda j: (0, j)),
            pl.BlockSpec((1, tn), lambda j: (0, j)),
            small, small, small, small,
        ],
        out_specs=[
            pl.BlockSpec((B, tn), lambda j: (0, j)),
            pl.BlockSpec((1, LANES), lambda j: (0, 0)),
        ],
        out_shape=[
            jax.ShapeDtypeStruct((B, n_out), F32),
            jax.ShapeDtypeStruct((1, LANES), F32),
        ],
        compiler_params=pltpu.CompilerParams(dimension_semantics=("arbitrary",)),
        name="ada_mod",
    )(c, w_ada, b_ada, lq1, lk1, lq2, lk2)


def _in_proj_kernel(x_ref, mod_ref, g_ref, w_ref, o_ref, h_scr):
    @pl.when(pl.program_id(1) == 0)
    def _():
        x = x_ref[...]
        ms = jnp.mean(x * x, axis=-1, keepdims=True)
        y = x * lax.rsqrt(ms + EPS) * g_ref[...]
        h = y * (1.0 + mod_ref[1:2, :]) + mod_ref[0:1, :]
        h_scr[...] = h.astype(BF16)

    o_ref[...] = _dot(h_scr[...], w_ref[...]).astype(BF16)


def _in_proj(x2, mod3, g1, w_in_r, S):
    N, D = x2.shape
    tm = min(1024, S)
    tiles_per_seq = S // tm
    return pl.pallas_call(
        _in_proj_kernel,
        grid=(N // tm, PROJ_COLS // PROJ_TN),
        in_specs=[
            pl.BlockSpec((tm, D), lambda i, j: (i, 0)),
            pl.BlockSpec((None, 6, D), lambda i, j: (i // tiles_per_seq, 0, 0)),
            pl.BlockSpec((1, D), lambda i, j: (0, 0)),
            pl.BlockSpec((D, PROJ_TN), lambda i, j: (0, j)),
        ],
        out_specs=pl.BlockSpec((tm, PROJ_TN), lambda i, j: (i, j)),
        out_shape=jax.ShapeDtypeStruct((N, PROJ_COLS), BF16),
        scratch_shapes=[pltpu.VMEM((tm, D), BF16)],
        compiler_params=pltpu.CompilerParams(
            dimension_semantics=("parallel", "arbitrary"), vmem_limit_bytes=VMEM_LIMIT),
        name="in_proj",
    )(x2, mod3, g1, w_in_r)


def _gla_kernel(q_ref, k_ref, v_ref, g_ref, a_ref, wa_ref, ba_ref, ng_ref, o_ref, st_ref, *, tc):
    nchunk = tc // CHUNK

    @pl.when(pl.program_id(2) == 0)
    def _():
        st_ref[...] = jnp.zeros_like(st_ref)

    z = _dot(a_ref[...], wa_ref[...]) + ba_ref[...]
    log_a = (jnp.minimum(z, 0.0) - jnp.log1p(jnp.exp(-jnp.abs(z)))) * (1.0 / GLA_TAU)

    row = lax.broadcasted_iota(jnp.int32, (tc, tc), 0)
    col = lax.broadcasted_iota(jnp.int32, (tc, tc), 1)
    same = _chunk_id(row) == _chunk_id(col)
    lower = same & (col <= row)
    upper = same & (col > row)
    tri = jnp.where(lower, 1.0, 0.0).astype(BF16)

    la_hi, la_lo = _split_bf16(log_a)
    bcum = _dot(tri, la_hi) + _dot(tri, la_lo)
    b_last = [bcum[c * CHUNK + CHUNK - 1:(c + 1) * CHUNK, :] for c in range(nchunk)]
    b_last_full = jnp.concatenate(
        [jnp.broadcast_to(bl, (CHUNK, GLA_DK)) for bl in b_last], axis=0)

    eb = jnp.exp(bcum)
    enb = jnp.exp(-bcum)
    qs = q_ref[...].astype(F32) * (GLA_DK ** -0.5)
    k = k_ref[...].astype(F32)
    v = v_ref[...]
    q_f = (qs * eb).astype(BF16)
    k_f = (k * enb).astype(BF16)
    q_b = (qs * enb).astype(BF16)
    k_b = (k * eb).astype(BF16)
    k_d = (k * jnp.exp(b_last_full - bcum)).astype(BF16)

    s_f = _dot_nt(q_f, k_f)
    s_b = _dot_nt(q_b, k_b)
    scores = jnp.where(lower, s_f, jnp.where(upper, s_b, 0.0))
    o_intra = _dot(scores.astype(BF16), v)

    st = st_ref[...]
    o_inter = []
    for c in range(nchunk):
        sl = slice(c * CHUNK, (c + 1) * CHUNK)
        o_inter.append(_dot_nt(q_f[sl], st.astype(BF16)))
        delta_t = _dot_tn(v[sl], k_d[sl])
        st = st * jnp.exp(b_last[c]) + delta_t
    st_ref[...] = st

    o = o_intra + jnp.concatenate(o_inter, axis=0)
    ms = jnp.mean(o * o, axis=-1, keepdims=True)
    y = o * lax.rsqrt(ms + EPS) * ng_ref[...]
    g = g_ref[...].astype(F32)
    o_ref[...] = (y * (g * _sigmoid(g))).astype(BF16)


def _gla(proj, wa_pad, b_alpha, gla_norm_g, B, S):
    N = proj.shape[0]
    tc = min(256, S)
    nt = S // tc
    row = lambda b, h, i: b * nt + i
    return pl.pallas_call(
        functools.partial(_gla_kernel, tc=tc),
        grid=(B, GLA_HEADS, nt),
        in_specs=[
            pl.BlockSpec((tc, GLA_DK), lambda b, h, i: (row(b, h, i), COL_GQ // GLA_DK + h)),
            pl.BlockSpec((tc, GLA_DK), lambda b, h, i: (row(b, h, i), COL_GK // GLA_DK + h)),
            pl.BlockSpec((tc, GLA_DV), lambda b, h, i: (row(b, h, i), COL_GV // GLA_DV + h)),
            pl.BlockSpec((tc, GLA_DV), lambda b, h, i: (row(b, h, i), COL_GG // GLA_DV + h)),
            pl.BlockSpec((tc, LANES), lambda b, h, i: (row(b, h, i), COL_ACODE // LANES)),
            pl.BlockSpec((LANES, GLA_DK), lambda b, h, i: (0, h)),
            pl.BlockSpec((1, GLA_DK), lambda b, h, i: (0, h)),
            pl.BlockSpec((1, GLA_DV), lambda b, h, i: (0, 0)),
        ],
        out_specs=pl.BlockSpec((tc, GLA_DV), lambda b, h, i: (row(b, h, i), h)),
        out_shape=jax.ShapeDtypeStruct((N, GLA_HEADS * GLA_DV), BF16),
        scratch_shapes=[pltpu.VMEM((GLA_DV, GLA_DK), F32)],
        compiler_params=pltpu.CompilerParams(
            dimension_semantics=("parallel", "parallel", "arbitrary"), vmem_limit_bytes=VMEM_LIMIT),
        name="gla",
    )(proj, proj, proj, proj, proj, wa_pad, b_alpha, gla_norm_g)


def _group_rms(xf, gsum, gain):
    hi, lo = _split_bf16(xf * xf)
    ss = _dot(hi, gsum) + _dot(lo, gsum)
    return xf * lax.rsqrt(ss * (1.0 / DIFF_DH) + EPS) * gain


def _diff_attn_kernel(q_ref, k_ref, v_ref, qg_ref, kg_ref, slope_ref, lam_ref, sg_ref, o_ref,
                      ka_scr, vt_scr, wt_scr, acc_scr, s_scr, p_scr, dm_scr, *, t, seq):
    i = pl.program_id(2)
    nqc = t // LANES
    width = nqc * 2 * LANES
    chains = tuple(range(ATTN_CHAINS))
    lane = lax.broadcasted_iota(jnp.int32, (1, LANES), 1)
    first_map = lane < DIFF_DH

    gr = lax.broadcasted_iota(jnp.int32, (LANES, LANES), 0) // DIFF_DH
    gc = lax.broadcasted_iota(jnp.int32, (LANES, LANES), 1) // DIFF_DH
    gsum = jnp.where(gr == gc, 1.0, 0.0).astype(BF16)
    slope2 = slope_ref[...] * LOG2E
    slope2_w = jnp.tile(slope2, (1, width // LANES))

    @pl.when(i == 0)
    def _():
        jl = lax.broadcasted_iota(jnp.int32, (t, LANES), 0).astype(F32)
        lane_t = lax.broadcasted_iota(jnp.int32, (t, LANES), 1)
        rem = slope2 * jl
        aux = jnp.zeros((t, LANES), F32)
        for term in range(BIAS_TERMS):
            part = rem.astype(BF16).astype(F32)
            aux = jnp.where(lane_t == term, part, aux)
            rem = rem - part
        aux = aux.astype(BF16)

        jd = lax.broadcasted_iota(jnp.int32, (t, width), 0)
        col = lax.broadcasted_iota(jnp.int32, (t, width), 1)
        il = (col & (LANES - 1)) + lax.shift_right_logical(col, 8) * LANES
        ahead = jnp.minimum(il - jd, 0).astype(F32)
        dm_scr[...] = jnp.where(_chunk_id(jd) <= _chunk_id(il), (2.0 * slope2_w) * ahead, MASK_VALUE)

        def body(r, carry):
            rows = pl.ds(pl.multiple_of(r * t, t), t)
            kn = _group_rms(k_ref[rows, :].astype(F32), gsum, kg_ref[...])
            ka_scr[rows, 0:LANES] = kn.astype(BF16)
            ka_scr[rows, LANES:2 * LANES] = aux
            vt_scr[r] = v_ref[rows, :].astype(F32).T.astype(BF16)
            return carry
        lax.fori_loop(0, seq // t, body, 0)

    qn = _group_rms(q_ref[...].astype(F32), gsum, qg_ref[...]) * (DIFF_DH ** -0.5 * LOG2E)
    ones = jnp.broadcast_to(jnp.where(lane < BIAS_TERMS, 1.0, 0.0), (LANES, LANES))
    for ch in chains:
        for qc in range(nqc):
            r0 = ch * t + qc * LANES
            qq = qn[r0:r0 + LANES, :]
            w = jnp.concatenate(
                [jnp.concatenate([jnp.where(first_map, qq, 0.0), ones], axis=1),
                 jnp.concatenate([jnp.where(first_map, 0.0, qq), ones], axis=1)], axis=0)
            wt_scr[ch, :, qc * 2 * LANES:(qc + 1) * 2 * LANES] = w.T.astype(BF16)
        p_scr[ch, 1] = jnp.zeros(p_scr.shape[2:], BF16)
    acc_scr[...] = jnp.zeros_like(acc_scr)

    def scores(ch, j, slot):
        rows = pl.ds(pl.multiple_of(j * t, t), t)
        st = _dot(ka_scr[rows, :], wt_scr[ch])
        s_scr[ch, slot] = st
        return jnp.max(st, axis=0, keepdims=True)

    def accumulate(ch, j, slot, alpha):
        acc_scr[ch] = alpha * acc_scr[ch] + _dot(vt_scr[jnp.maximum(j, 0)], p_scr[ch, slot])

    def softmax(ch, j, st, m_cur, m_prev, l_prev):
        shift = slope2_w * ((j - (ATTN_CHAINS * i + ch)) * t).astype(F32)
        m_new = jnp.maximum(m_prev, m_cur + shift)
        p = jnp.exp2(st - (m_new - shift))
        alpha = jnp.exp2(m_prev - m_new)
        l_new = alpha * l_prev + jnp.sum(p, axis=0, keepdims=True)
        return p.astype(BF16), alpha, m_new, l_new

    def step(ch, j, slot, carry):
        m, l, m_cur, alpha_prev = carry
        m_next = scores(ch, j + 1, 1 - slot)
        accumulate(ch, j - 1, 1 - slot, alpha_prev)
        p, alpha, m, l = softmax(ch, j, s_scr[ch, slot], m_cur, m, l)
        p_scr[ch, slot] = p
        return m, l, m_next, alpha

    def finish(ch, j, slot, carry):
        m, l, _, alpha_prev = carry
        accumulate(ch, j - 1, 1 - slot, alpha_prev)
        st = s_scr[ch, slot] + dm_scr[...]
        p, alpha, m, l = softmax(ch, j, st, jnp.max(st, axis=0, keepdims=True), m, l)
        on = (alpha * acc_scr[ch] + _dot(vt_scr[j], p)) * (1.0 / l)
        for qc in range(nqc):
            c0 = qc * 2 * LANES
            ot = on[:, c0:c0 + LANES] - lam_ref[...] * on[:, c0 + LANES:c0 + 2 * LANES]
            o = ot.T
            msq = jnp.mean(o * o, axis=-1, keepdims=True)
            y = o * lax.rsqrt(msq + EPS) * sg_ref[...] * (1.0 - LAMBDA_INIT)
            r0 = ch * t + qc * LANES
            o_ref[r0:r0 + LANES, :] = y.astype(BF16)

    def pair(pp, carries):
        carries = list(carries)
        for j, slot in ((2 * pp, 0), (2 * pp + 1, 1)):
            for ch in chains:
                carries[ch] = step(ch, j, slot, carries[ch])
        return tuple(carries)

    init = tuple((jnp.full((1, width), MASK_VALUE, F32), jnp.zeros((1, width), F32),
                  scores(ch, 0, 0), jnp.ones((1, width), F32)) for ch in chains)
    carries = list(lax.fori_loop(0, (ATTN_CHAINS // 2) * i, pair, init))

    for k in chains:
        j = ATTN_CHAINS * i + k
        for ch in chains[k + 1:]:
            carries[ch] = step(ch, j, k % 2, carries[ch])
        finish(k, j, k % 2, carries[k])


def _diff_attn_fixed_kernel(q_ref, k_ref, v_ref, qg_ref, kg_ref, slope_ref, lam_ref, sg_ref, bound_ref,
                            o_ref, ka_scr, vt_scr, wt_scr, acc_scr, s_scr, p_scr, dm_scr, *, t, seq):
    i = pl.program_id(2)
    nqc = t // LANES
    width = nqc * 2 * LANES
    chains = tuple(range(ATTN_CHAINS))
    lane = lax.broadcasted_iota(jnp.int32, (1, LANES), 1)
    first_map = lane < DIFF_DH

    gr = lax.broadcasted_iota(jnp.int32, (LANES, LANES), 0) // DIFF_DH
    gc = lax.broadcasted_iota(jnp.int32, (LANES, LANES), 1) // DIFF_DH
    gsum = jnp.where(gr == gc, 1.0, 0.0).astype(BF16)
    slope2 = slope_ref[...] * LOG2E
    slope2_w = jnp.tile(slope2, (1, width // LANES))
    row_t = lax.broadcasted_iota(jnp.int32, (t, LANES), 0)

    def split_terms(value, ones_first):
        rem = value
        lo, hi = (BIAS_TERMS, 0) if ones_first else (0, BIAS_TERMS)
        lanes = lax.broadcasted_iota(jnp.int32, value.shape, 1)
        out = jnp.where((lanes >= hi) & (lanes < hi + BIAS_TERMS), 1.0, 0.0)
        for term in range(BIAS_TERMS):
            part = rem.astype(BF16).astype(F32)
            out = jnp.where(lanes == lo + term, part, out)
            rem = rem - part
        return out

    @pl.when(i == 0)
    def _():
        jd = lax.broadcasted_iota(jnp.int32, (t, width), 0)
        col = lax.broadcasted_iota(jnp.int32, (t, width), 1)
        il = (col & (LANES - 1)) + lax.shift_right_logical(col, 8) * LANES
        ahead = jnp.minimum(il - jd, 0).astype(F32)
        dm_scr[...] = jnp.where(_chunk_id(jd) <= _chunk_id(il), (2.0 * slope2_w) * ahead, MASK_VALUE)

        def body(r, carry):
            rows = pl.ds(pl.multiple_of(r * t, t), t)
            kn = _group_rms(k_ref[rows, :].astype(F32), gsum, kg_ref[...])
            ka_scr[rows, 0:LANES] = kn.astype(BF16)
            key_pos = (row_t + r * t).astype(F32)
            ka_scr[rows, LANES:2 * LANES] = split_terms(slope2 * key_pos, False).astype(BF16)
            vt_scr[r] = v_ref[rows, :].astype(F32).T.astype(BF16)
            return carry
        lax.fori_loop(0, seq // t, body, 0)

    qn = _group_rms(q_ref[...].astype(F32), gsum, qg_ref[...]) * (DIFF_DH ** -0.5 * LOG2E)
    row_q = lax.broadcasted_iota(jnp.int32, (LANES, LANES), 0)
    for ch in chains:
        for qc in range(nqc):
            r0 = ch * t + qc * LANES
            qq = qn[r0:r0 + LANES, :]
            q_pos = (row_q + (i * (ATTN_CHAINS * t) + r0)).astype(F32)
            side = split_terms(-(slope2 * q_pos + bound_ref[...]), True)
            w = jnp.concatenate(
                [jnp.concatenate([jnp.where(first_map, qq, 0.0), side], axis=1),
                 jnp.concatenate([jnp.where(first_map, 0.0, qq), side], axis=1)], axis=0)
            wt_scr[ch, :, qc * 2 * LANES:(qc + 1) * 2 * LANES] = w.T.astype(BF16)
        p_scr[ch, 1] = jnp.zeros(p_scr.shape[2:], BF16)
    acc_scr[...] = jnp.zeros_like(acc_scr)

    def scores(ch, j, slot):
        rows = pl.ds(pl.multiple_of(j * t, t), t)
        s_scr[ch, slot] = _dot(ka_scr[rows, :], wt_scr[ch])

    def accumulate(ch, j, slot):
        acc_scr[ch] += _dot(vt_scr[jnp.maximum(j, 0)], p_scr[ch, slot])

    def expo(ch, slot, st, l_part):
        p = jnp.exp2(st)
        p_scr[ch, slot] = p.astype(BF16)
        return l_part + jnp.sum(p.reshape(t // SUBLANES, SUBLANES, width), axis=0)

    def step(ch, j, slot, l_part):
        scores(ch, j + 1, 1 - slot)
        accumulate(ch, j - 1, 1 - slot)
        return expo(ch, slot, s_scr[ch, slot], l_part)

    def finish(ch, j, slot, l_part):
        accumulate(ch, j - 1, 1 - slot)
        l_part = expo(ch, slot, s_scr[ch, slot] + dm_scr[...], l_part)
        accumulate(ch, j, slot)
        on = acc_scr[ch] * (1.0 / jnp.sum(l_part, axis=0, keepdims=True))
        for qc in range(nqc):
            c0 = qc * 2 * LANES
            ot = on[:, c0:c0 + LANES] - lam_ref[...] * on[:, c0 + LANES:c0 + 2 * LANES]
            o = ot.T
            msq = jnp.mean(o * o, axis=-1, keepdims=True)
            y = o * lax.rsqrt(msq + EPS) * sg_ref[...] * (1.0 - LAMBDA_INIT)
            r0 = ch * t + qc * LANES
            o_ref[r0:r0 + LANES, :] = y.astype(BF16)

    def pair(pp, parts):
        parts = list(parts)
        for j, slot in ((2 * pp, 0), (2 * pp + 1, 1)):
            for ch in chains:
                parts[ch] = step(ch, j, slot, parts[ch])
        return tuple(parts)

    for ch in chains:
        scores(ch, 0, 0)
    parts = list(lax.fori_loop(0, (ATTN_CHAINS // 2) * i, pair,
                               tuple(jnp.zeros((SUBLANES, width), F32) for _ in chains)))

    for k in chains:
        j = ATTN_CHAINS * i + k
        for ch in chains[k + 1:]:
            parts[ch] = step(ch, j, k % 2, parts[ch])
        finish(k, j, k % 2, parts[k])


def _diff_attn(proj, qg2, kg2, slopes, lam, subln_g, B, S, bound=None):
    N = proj.shape[0]
    t = min(256, S // ATTN_CHAINS)
    tq = ATTN_CHAINS * t
    nq = S // tq
    vec = pl.BlockSpec((1, LANES), lambda b, h, i: (0, 0))
    body = _diff_attn_kernel if bound is None else _diff_attn_fixed_kernel
    extra = () if bound is None else (bound,)
    return pl.pallas_call(
        functools.partial(body, t=t, seq=S),
        grid=(B, DIFF_HEADS, nq),
        in_specs=[
            pl.BlockSpec((tq, LANES), lambda b, h, i: (b * nq + i, COL_DQ // LANES + h)),
            pl.BlockSpec((S, LANES), lambda b, h, i: (b, COL_DK // LANES + h)),
            pl.BlockSpec((S, LANES), lambda b, h, i: (b, COL_DV // LANES + h)),
            vec, vec,
            pl.BlockSpec((None, 1, LANES), lambda b, h, i: (h, 0, 0)),
            vec, vec,
        ] + [vec] * len(extra),
        out_specs=pl.BlockSpec((tq, DIFF_DV), lambda b, h, i: (b * nq + i, h)),
        out_shape=jax.ShapeDtypeStruct((N, DIFF_HEADS * DIFF_DV), BF16),
        scratch_shapes=[
            pltpu.VMEM((S, 2 * LANES), BF16),
            pltpu.VMEM((S // t, DIFF_DV, t), BF16),
            pltpu.VMEM((ATTN_CHAINS, 2 * LANES, 2 * t), BF16),
            pltpu.VMEM((ATTN_CHAINS, DIFF_DV, 2 * t), F32),
            pltpu.VMEM((ATTN_CHAINS, 2, t, 2 * t), F32),
            pltpu.VMEM((ATTN_CHAINS, 2, t, 2 * t), BF16),
            pltpu.VMEM((t, 2 * t), F32),
        ],
        compiler_params=pltpu.CompilerParams(
            dimension_semantics=("parallel", "parallel", "arbitrary"), vmem_limit_bytes=VMEM_LIMIT),
        name="diff_attn" if bound is None else "diff_attn_fixed",
    )(proj, proj, proj, qg2, kg2, slopes, lam, subln_g, *extra)


def _merge_kernel(oa_ref, ob_ref, ga_ref, gb_ref, x_ref, mod_ref, wa_ref, wb_ref, wo_ref, g2_ref,
                  x1_ref, h2_ref):
    ya = _dot(oa_ref[...], wa_ref[...])
    yb = _dot(ob_ref[...], wb_ref[...])
    merged = _sigmoid(ga_ref[...].astype(F32)) * ya + _sigmoid(gb_ref[...].astype(F32)) * yb
    x1 = x_ref[...] + mod_ref[2:3, :] * _dot(merged.astype(BF16), wo_ref[...])
    x1_ref[...] = x1
    ms = jnp.mean(x1 * x1, axis=-1, keepdims=True)
    y = x1 * lax.rsqrt(ms + EPS) * g2_ref[...]
    h2_ref[...] = (y * (1.0 + mod_ref[4:5, :]) + mod_ref[3:4, :]).astype(BF16)


def _merge(o_a, o_b, proj, x2, mod3, w_gla_o, w_diff_o, w_out, g2, S):
    N, D = x2.shape
    tm = min(512, S)
    tiles_per_seq = S // tm
    tok = lambda col: pl.BlockSpec((tm, D), lambda i: (i, col))
    wspec = pl.BlockSpec((D, D), lambda i: (0, 0))
    return pl.pallas_call(
        _merge_kernel,
        grid=(N // tm,),
        in_specs=[
            tok(0), tok(0), tok(COL_GA // D), tok(COL_GB // D), tok(0),
            pl.BlockSpec((None, 6, D), lambda i: (i // tiles_per_seq, 0, 0)),
            wspec, wspec, wspec,
            pl.BlockSpec((1, D), lambda i: (0, 0)),
        ],
        out_specs=[tok(0), tok(0)],
        out_shape=[jax.ShapeDtypeStruct((N, D), F32), jax.ShapeDtypeStruct((N, D), BF16)],
        compiler_params=pltpu.CompilerParams(
            dimension_semantics=("parallel",), vmem_limit_bytes=VMEM_LIMIT),
        name="merge",
    )(o_a, o_b, proj, proj, x2, mod3, w_gla_o, w_diff_o, w_out, g2)


def _conv_ffn_kernel(h2_ref, x1_ref, mod_ref, wup_ref, cw_ref, cb_ref, wdn_ref, o_ref,
                     carry_ref, acc_ref, *, tm, tf, tiles_per_seq):
    @pl.when(pl.program_id(0) % tiles_per_seq == 0)
    def _():
        carry_ref[...] = jnp.zeros_like(carry_ref)

    h2 = h2_ref[...]
    row = lax.broadcasted_iota(jnp.int32, (tm, tf), 0)

    def up(f):
        return (_dot(h2, wup_ref[:, f * tf:(f + 1) * tf]),
                _dot(h2, wup_ref[:, D_FF + f * tf:D_FF + (f + 1) * tf]))

    def conv(u, col0):
        prev = carry_ref[:, col0:col0 + tf]
        p1 = prev[SUBLANES - 1:SUBLANES, :]
        p2 = prev[SUBLANES - 2:SUBLANES - 1, :]
        r1 = jnp.where(row == 0, p1, pltpu.roll(u, 1, axis=0))
        r2 = jnp.where(row == 0, p2, jnp.where(row == 1, p1, pltpu.roll(u, 2, axis=0)))
        carry_ref[:, col0:col0 + tf] = u[tm - SUBLANES:, :]
        cw = cw_ref[:, col0:col0 + tf]
        return cw[0:1, :] * r2 + cw[1:2, :] * r1 + cw[2:3, :] * u + cb_ref[:, col0:col0 + tf]

    nf = D_FF // tf
    u_next = up(0)
    for f in range(nf):
        ua, ub = u_next
        if f + 1 < nf:
            u_next = up(f + 1)
        a = conv(ua, f * tf)
        b = conv(ub, D_FF + f * tf)
        act = (a * _sigmoid(a) * b).astype(BF16)
        contrib = _dot(act, wdn_ref[f * tf:(f + 1) * tf, :])
        if f == 0:
            acc_ref[...] = contrib
        else:
            acc_ref[...] += contrib

    o_ref[...] = x1_ref[...] + mod_ref[5:6, :] * acc_ref[...]


def _conv_ffn(h2, x1, mod3, w_up, conv_w, conv_b, w_down, S):
    N, D = x1.shape
    tm = min(512, S)
    tf = 256
    tiles_per_seq = S // tm
    tok = pl.BlockSpec((tm, D), lambda i: (i, 0))
    full = lambda a: pl.BlockSpec(a.shape, lambda i: (0, 0))
    return pl.pallas_call(
        functools.partial(_conv_ffn_kernel, tm=tm, tf=tf, tiles_per_seq=tiles_per_seq),
        grid=(N // tm,),
        in_specs=[
            tok, tok,
            pl.BlockSpec((None, 6, D), lambda i: (i // tiles_per_seq, 0, 0)),
            full(w_up), full(conv_w), full(conv_b), full(w_down),
        ],
        out_specs=tok,
        out_shape=jax.ShapeDtypeStruct((N, D), F32),
        scratch_shapes=[pltpu.VMEM((SUBLANES, 2 * D_FF), F32), pltpu.VMEM((tm, D), F32)],
        compiler_params=pltpu.CompilerParams(
            dimension_semantics=("arbitrary",), vmem_limit_bytes=VMEM_LIMIT),
        name="conv_ffn",
    )(h2, x1, mod3, w_up, conv_w, conv_b, w_down)


def kernel(x, c, w_ada, b_ada, norm1_g, w_in, w_alpha_up, b_alpha, gla_norm_g, q_norm_g, k_norm_g,
           lam_q1, lam_k1, lam_q2, lam_k2, diff_norm_g, w_gla_o, w_diff_o, w_out, norm2_g, w_up,
           conv_w, conv_b, w_down):
    B, S, D = x.shape
    N = B * S
    assert D == D_MODEL and S % CHUNK == 0 and w_ada.shape[0] == 1

    w0 = w_in[0]
    a0 = COL_GG + GLA_HEADS * GLA_DV
    w_in_r = jnp.concatenate(
        [w0[:, :a0], w0[:, a0 + GLA_RANK:],
         jnp.pad(w0[:, a0:a0 + GLA_RANK], ((0, 0), (0, LANES - GLA_RANK)))], axis=1).astype(BF16)
    wa_pad = jnp.pad(w_alpha_up[0], ((0, LANES - GLA_RANK), (0, 0))).astype(BF16)
    slopes = 2.0 ** (-8.0 * (jnp.arange(DIFF_HEADS, dtype=F32) + 1.0) / DIFF_HEADS)
    slopes = jnp.broadcast_to(slopes[:, None, None], (DIFF_HEADS, 1, LANES))
    qg2 = jnp.tile(q_norm_g, (1, 2))
    kg2 = jnp.tile(k_norm_g, (1, 2))

    mod, lam = _ada_mod(c, w_ada[0], b_ada, lam_q1, lam_k1, lam_q2, lam_k2)
    mod3 = mod.reshape(B, 6, D)
    x2 = x.reshape(N, D)

    proj = _in_proj(x2, mod3, norm1_g, w_in_r, S)
    o_a = _gla(proj, wa_pad, b_alpha, gla_norm_g, B, S)
    bound = (LOGIT_BOUND_SLACK * DIFF_DH ** 0.5 * LOG2E) * jnp.max(jnp.abs(q_norm_g)) * jnp.max(jnp.abs(k_norm_g))
    o_b = lax.cond(
        bound <= MAX_FIXED_BOUND,
        lambda: _diff_attn(proj, qg2, kg2, slopes, lam, diff_norm_g, B, S,
                           bound=jnp.broadcast_to(bound, (1, LANES)).astype(F32)),
        lambda: _diff_attn(proj, qg2, kg2, slopes, lam, diff_norm_g, B, S))
    x1, h2 = _merge(o_a, o_b, proj, x2, mod3, w_gla_o[0].astype(BF16), w_diff_o[0].astype(BF16),
                    w_out[0].astype(BF16), norm2_g, S)
    out = _conv_ffn(h2, x1, mod3, w_up[0].astype(BF16), conv_w[0], conv_b, w_down[0].astype(BF16), S)
    return out.reshape(B, S, D)
```

```python
import functools
import math

import jax
import jax.numpy as jnp
from jax import lax
from jax.experimental import pallas as pl
from jax.experimental.pallas import tpu as pltpu

F32 = jnp.float32
BF16 = jnp.bfloat16

D_MODEL = 1024
CHUNK = 64
EPS = 1e-6
GLA_HEADS = 4
GLA_DK = 128
GLA_DV = 256
GLA_RANK = 16
GLA_TAU = 16.0
GLA_HEADS_PER_STEP = 2
DIFF_HEADS = 8
DIFF_DH = 64
DIFF_DV = 128
D_FF = 2816
CONV_W = 3
DOWN_GROUP = 4
LAMBDA_INIT = 0.8 - 0.6 * math.exp(-0.3 * 0)

LANES = 128
SUBLANES = 8
MASK_VALUE = -1e30
LOG2E = math.log2(math.e)
BIAS_TERMS = 3
ATTN_CHAINS = 4
LOGIT_BOUND_SLACK = 1.02
MAX_FIXED_BOUND = 50.0

COL_GQ, COL_GK, COL_GV, COL_GG = 0, 512, 1024, 2048
COL_DQ, COL_DK, COL_DV = 3072, 4096, 5120
COL_GA, COL_GB, COL_ACODE = 6144, 7168, 8192
PROJ_COLS = 8192 + LANES
PROJ_TN = PROJ_COLS // 5

VMEM_LIMIT = 56 * 1024 * 1024


def _dot(a, b):
    return jnp.dot(a, b, preferred_element_type=F32)


def _dot_nt(a, b):
    return lax.dot_general(a, b, (((1,), (1,)), ((), ())), preferred_element_type=F32)


def _dot_tn(a, b):
    return lax.dot_general(a, b, (((0,), (0,)), ((), ())), preferred_element_type=F32)


def _split_bf16(v):
    hi = v.astype(BF16)
    lo = (v - hi.astype(F32)).astype(BF16)
    return hi, lo


def _sigmoid(v):
    return 1.0 / (1.0 + jnp.exp(-v))


def _chunk_id(pos):
    return lax.shift_right_logical(pos, CHUNK.bit_length() - 1)


def _ada_kernel(c_ref, w_ref, b_ref, lq1_ref, lk1_ref, lq2_ref, lk2_ref, mod_ref, lam_ref):
    c = c_ref[...]
    a = c * _sigmoid(c)
    a_hi, a_lo = _split_bf16(a)
    w_hi, w_lo = _split_bf16(w_ref[...])
    mod_ref[...] = _dot(a_hi, w_hi) + _dot(a_lo, w_hi) + _dot(a_hi, w_lo) + b_ref[...]
    s1 = jnp.sum(lq1_ref[...] * lk1_ref[...], axis=-1, keepdims=True)
    s2 = jnp.sum(lq2_ref[...] * lk2_ref[...], axis=-1, keepdims=True)
    lam = jnp.exp(s1) - jnp.exp(s2) + LAMBDA_INIT
    lam_ref[...] = jnp.broadcast_to(lam, lam_ref.shape)


def _ada_mod(c, w_ada, b_ada, lq1, lk1, lq2, lk2):
    B, D = c.shape
    n_out = w_ada.shape[1]
    tn = D
    small = pl.BlockSpec((1, DIFF_DH), lambda j: (0, 0))
    return pl.pallas_call(
        _ada_kernel,
        grid=(n_out // tn,),
        in_specs=[
            pl.BlockSpec((B, D), lambda j: (0, 0)),
            pl.BlockSpec((D, tn), lambda j: (0, j)),
            pl.BlockSpec((1, tn), lambda j: (0, j)),
            small, small, small, small,
        ],
        out_specs=[
            pl.BlockSpec((B, tn), lambda j: (0, j)),
            pl.BlockSpec((1, LANES), lambda j: (0, 0)),
        ],
        out_shape=[
            jax.ShapeDtypeStruct((B, n_out), F32),
            jax.ShapeDtypeStruct((1, LANES), F32),
        ],
        compiler_params=pltpu.CompilerParams(dimension_semantics=("arbitrary",)),
        name="ada_mod",
    )(c, w_ada, b_ada, lq1, lk1, lq2, lk2)


def _in_proj_kernel(x_ref, mod_ref, g_ref, w_ref, o_ref, h_scr):
    @pl.when(pl.program_id(1) == 0)
    def _():
        x = x_ref[...]
        ms = jnp.mean(x * x, axis=-1, keepdims=True)
        y = x * lax.rsqrt(ms + EPS) * g_ref[...]
        h = y * (1.0 + mod_ref[1:2, :]) + mod_ref[0:1, :]
        h_scr[...] = h.astype(BF16)

    o_ref[...] = _dot(h_scr[...], w_ref[...]).astype(BF16)


def _in_proj(x2, mod3, g1, w_in_r, S):
    N, D = x2.shape
    tm = min(1024, S)
    tiles_per_seq = S // tm
    return pl.pallas_call(
        _in_proj_kernel,
        grid=(N // tm, PROJ_COLS // PROJ_TN),
        in_specs=[
            pl.BlockSpec((tm, D), lambda i, j: (i, 0)),
            pl.BlockSpec((None, 6, D), lambda i, j: (i // tiles_per_seq, 0, 0)),
            pl.BlockSpec((1, D), lambda i, j: (0, 0)),
            pl.BlockSpec((D, PROJ_TN), lambda i, j: (0, j)),
        ],
        out_specs=pl.BlockSpec((tm, PROJ_TN), lambda i, j: (i, j)),
        out_shape=jax.ShapeDtypeStruct((N, PROJ_COLS), BF16),
        scratch_shapes=[pltpu.VMEM((tm, D), BF16)],
        compiler_params=pltpu.CompilerParams(
            dimension_semantics=("parallel", "arbitrary"), vmem_limit_bytes=VMEM_LIMIT),
        name="in_proj",
    )(x2, mod3, g1, w_in_r)


def _gla_kernel(q_ref, k_ref, v_ref, g_ref, a_ref, wa_ref, ba_ref, ng_ref, o_ref, st_ref, *, tc):
    nchunk = tc // CHUNK

    @pl.when(pl.program_id(2) == 0)
    def _():
        st_ref[...] = jnp.zeros_like(st_ref)

    row = lax.broadcasted_iota(jnp.int32, (tc, tc), 0)
    col = lax.broadcasted_iota(jnp.int32, (tc, tc), 1)
    chunk_gap = _chunk_id(row) - _chunk_id(col)
    lower = (chunk_gap == 0) & (col <= row)
    upper = (chunk_gap == 0) & (col > row)
    tri = jnp.where(lower, 1.0, 0.0).astype(BF16)

    def per_chunk(rows):
        return jnp.concatenate([jnp.broadcast_to(r, (CHUNK, GLA_DK)) for r in rows], axis=0)

    def head(hh):
        dk = slice(hh * GLA_DK, (hh + 1) * GLA_DK)
        dv = slice(hh * GLA_DV, (hh + 1) * GLA_DV)
        z = _dot(a_ref[...], wa_ref[:, dk]) + ba_ref[:, dk]
        yield
        log_a = (jnp.minimum(z, 0.0) - jnp.log1p(jnp.exp(-jnp.abs(z)))) * (1.0 / GLA_TAU)
        la_hi, la_lo = _split_bf16(log_a)
        bcum = _dot(tri, la_hi) + _dot(tri, la_lo)
        yield
        b_last = [bcum[c * CHUNK + CHUNK - 1:(c + 1) * CHUNK, :] for c in range(nchunk)]
        prefix = [jnp.zeros_like(b_last[0])]
        for c in range(nchunk):
            prefix.append(prefix[-1] + b_last[c])
        one = jnp.ones_like(b_last[0])

        eb = jnp.exp(bcum)
        enb = jnp.exp(-bcum)
        qs = q_ref[:, dk].astype(F32) * (GLA_DK ** -0.5)
        k = k_ref[:, dk].astype(F32)
        v = v_ref[:, dv]
        q_fwd = qs * eb
        q_f = q_fwd.astype(BF16)
        k_dec = k * jnp.exp(per_chunk(b_last) - bcum)
        k_d = k_dec.astype(BF16)

        s_fwd = _dot_nt(q_f, (k * enb).astype(BF16))
        s_bwd = _dot_nt((qs * enb).astype(BF16), (k * eb).astype(BF16))
        yield
        scores = jnp.where(lower, s_fwd, jnp.where(upper, s_bwd, 0.0))
        for off in range(1, nchunk):
            if off == 1:
                q_x = q_f
            else:
                between = [one] * off + [jnp.exp(prefix[c] - prefix[c - off + 1]) for c in range(off, nchunk)]
                q_x = (q_fwd * per_chunk(between)).astype(BF16)
            scores = jnp.where(chunk_gap == off, _dot_nt(q_x, k_d), scores)
            yield

        st = st_ref[hh]
        q_s = (q_fwd * per_chunk([jnp.exp(p) for p in prefix[:nchunk]])).astype(BF16)
        o = _dot(scores.astype(BF16), v) + _dot_nt(q_s, st.astype(BF16))
        to_end = [jnp.exp(prefix[nchunk] - prefix[c + 1]) for c in range(nchunk)]
        st_ref[hh] = st * jnp.exp(prefix[nchunk]) + _dot_tn(v, (k_dec * per_chunk(to_end)).astype(BF16))
        yield

        ms = jnp.mean(o * o, axis=-1, keepdims=True)
        y = o * lax.rsqrt(ms + EPS) * ng_ref[...]
        g = g_ref[:, dv].astype(F32)
        o_ref[:, dv] = (y * (g * _sigmoid(g))).astype(BF16)

    heads = [head(hh) for hh in range(GLA_HEADS_PER_STEP)]
    while heads:
        heads = [h for h in heads if next(h, True) is None]


def _gla(proj, wa_pad, b_alpha, gla_norm_g, B, S):
    N = proj.shape[0]
    tc = min(256, S)
    nt = S // tc
    hps = GLA_HEADS_PER_STEP
    row = lambda b, h, i: b * nt + i
    return pl.pallas_call(
        functools.partial(_gla_kernel, tc=tc),
        grid=(B, GLA_HEADS // hps, nt),
        in_specs=[
            pl.BlockSpec((tc, hps * GLA_DK), lambda b, h, i: (row(b, h, i), COL_GQ // (hps * GLA_DK) + h)),
            pl.BlockSpec((tc, hps * GLA_DK), lambda b, h, i: (row(b, h, i), COL_GK // (hps * GLA_DK) + h)),
            pl.BlockSpec((tc, hps * GLA_DV), lambda b, h, i: (row(b, h, i), COL_GV // (hps * GLA_DV) + h)),
            pl.BlockSpec((tc, hps * GLA_DV), lambda b, h, i: (row(b, h, i), COL_GG // (hps * GLA_DV) + h)),
            pl.BlockSpec((tc, LANES), lambda b, h, i: (row(b, h, i), COL_ACODE // LANES)),
            pl.BlockSpec((LANES, hps * GLA_DK), lambda b, h, i: (0, h)),
            pl.BlockSpec((1, hps * GLA_DK), lambda b, h, i: (0, h)),
            pl.BlockSpec((1, GLA_DV), lambda b, h, i: (0, 0)),
        ],
        out_specs=pl.BlockSpec((tc, hps * GLA_DV), lambda b, h, i: (row(b, h, i), h)),
        out_shape=jax.ShapeDtypeStruct((N, GLA_HEADS * GLA_DV), BF16),
        scratch_shapes=[pltpu.VMEM((hps, GLA_DV, GLA_DK), F32)],
        compiler_params=pltpu.CompilerParams(
            dimension_semantics=("parallel", "parallel", "arbitrary"), vmem_limit_bytes=VMEM_LIMIT),
        name="gla",
    )(proj, proj, proj, proj, proj, wa_pad, b_alpha, gla_norm_g)


def _group_rms(xf, gsum, gain):
    hi, lo = _split_bf16(xf * xf)
    ss = _dot(hi, gsum) + _dot(lo, gsum)
    return xf * lax.rsqrt(ss * (1.0 / DIFF_DH) + EPS) * gain


def _diff_attn_kernel(q_ref, k_ref, v_ref, qg_ref, kg_ref, slope_ref, lam_ref, sg_ref, o_ref,
                      ka_scr, vt_scr, wt_scr, acc_scr, s_scr, p_scr, dm_scr, *, t, seq):
    i = pl.program_id(2)
    nqc = t // LANES
    width = nqc * 2 * LANES
    chains = tuple(range(ATTN_CHAINS))
    lane = lax.broadcasted_iota(jnp.int32, (1, LANES), 1)
    first_map = lane < DIFF_DH

    gr = lax.broadcasted_iota(jnp.int32, (LANES, LANES), 0) // DIFF_DH
    gc = lax.broadcasted_iota(jnp.int32, (LANES, LANES), 1) // DIFF_DH
    gsum = jnp.where(gr == gc, 1.0, 0.0).astype(BF16)
    slope2 = slope_ref[...] * LOG2E
    slope2_w = jnp.tile(slope2, (1, width // LANES))

    @pl.when(i == 0)
    def _():
        jl = lax.broadcasted_iota(jnp.int32, (t, LANES), 0).astype(F32)
        lane_t = lax.broadcasted_iota(jnp.int32, (t, LANES), 1)
        rem = slope2 * jl
        aux = jnp.zeros((t, LANES), F32)
        for term in range(BIAS_TERMS):
            part = rem.astype(BF16).astype(F32)
            aux = jnp.where(lane_t == term, part, aux)
            rem = rem - part
        aux = aux.astype(BF16)

        jd = lax.broadcasted_iota(jnp.int32, (t, width), 0)
        col = lax.broadcasted_iota(jnp.int32, (t, width), 1)
        il = (col & (LANES - 1)) + lax.shift_right_logical(col, 8) * LANES
        ahead = jnp.minimum(il - jd, 0).astype(F32)
        dm_scr[...] = jnp.where(_chunk_id(jd) <= _chunk_id(il), (2.0 * slope2_w) * ahead, MASK_VALUE)

        def body(r, carry):
            rows = pl.ds(pl.multiple_of(r * t, t), t)
            kn = _group_rms(k_ref[rows, :].astype(F32), gsum, kg_ref[...])
            ka_scr[rows, 0:LANES] = kn.astype(BF16)
            ka_scr[rows, LANES:2 * LANES] = aux
            vt_scr[r] = v_ref[rows, :].astype(F32).T.astype(BF16)
            return carry
        lax.fori_loop(0, seq // t, body, 0)

    qn = _group_rms(q_ref[...].astype(F32), gsum, qg_ref[...]) * (DIFF_DH ** -0.5 * LOG2E)
    ones = jnp.broadcast_to(jnp.where(lane < BIAS_TERMS, 1.0, 0.0), (LANES, LANES))
    for ch in chains:
        for qc in range(nqc):
            r0 = ch * t + qc * LANES
            qq = qn[r0:r0 + LANES, :]
            w = jnp.concatenate(
                [jnp.concatenate([jnp.where(first_map, qq, 0.0), ones], axis=1),
                 jnp.concatenate([jnp.where(first_map, 0.0, qq), ones], axis=1)], axis=0)
            wt_scr[ch, :, qc * 2 * LANES:(qc + 1) * 2 * LANES] = w.T.astype(BF16)
        p_scr[ch, 1] = jnp.zeros(p_scr.shape[2:], BF16)
    acc_scr[...] = jnp.zeros_like(acc_scr)

    def scores(ch, j, slot):
        rows = pl.ds(pl.multiple_of(j * t, t), t)
        st = _dot(ka_scr[rows, :], wt_scr[ch])
        s_scr[ch, slot] = st
        return jnp.max(st, axis=0, keepdims=True)

    def accumulate(ch, j, slot, alpha):
        acc_scr[ch] = alpha * acc_scr[ch] + _dot(vt_scr[jnp.maximum(j, 0)], p_scr[ch, slot])

    def softmax(ch, j, st, m_cur, m_prev, l_prev):
        shift = slope2_w * ((j - (ATTN_CHAINS * i + ch)) * t).astype(F32)
        m_new = jnp.maximum(m_prev, m_cur + shift)
        p = jnp.exp2(st - (m_new - shift))
        alpha = jnp.exp2(m_prev - m_new)
        l_new = alpha * l_prev + jnp.sum(p, axis=0, keepdims=True)
        return p.astype(BF16), alpha, m_new, l_new

    def step(ch, j, slot, carry):
        m, l, m_cur, alpha_prev = carry
        m_next = scores(ch, j + 1, 1 - slot)
        accumulate(ch, j - 1, 1 - slot, alpha_prev)
        p, alpha, m, l = softmax(ch, j, s_scr[ch, slot], m_cur, m, l)
        p_scr[ch, slot] = p
        return m, l, m_next, alpha

    def finish(ch, j, slot, carry):
        m, l, _, alpha_prev = carry
        accumulate(ch, j - 1, 1 - slot, alpha_prev)
        st = s_scr[ch, slot] + dm_scr[...]
        p, alpha, m, l = softmax(ch, j, st, jnp.max(st, axis=0, keepdims=True), m, l)
        on = (alpha * acc_scr[ch] + _dot(vt_scr[j], p)) * (1.0 / l)
        for qc in range(nqc):
            c0 = qc * 2 * LANES
            ot = on[:, c0:c0 + LANES] - lam_ref[...] * on[:, c0 + LANES:c0 + 2 * LANES]
            o = ot.T
            msq = jnp.mean(o * o, axis=-1, keepdims=True)
            y = o * lax.rsqrt(msq + EPS) * sg_ref[...] * (1.0 - LAMBDA_INIT)
            r0 = ch * t + qc * LANES
            o_ref[r0:r0 + LANES, :] = y.astype(BF16)

    def pair(pp, carries):
        carries = list(carries)
        for j, slot in ((2 * pp, 0), (2 * pp + 1, 1)):
            for ch in chains:
                carries[ch] = step(ch, j, slot, carries[ch])
        return tuple(carries)

    init = tuple((jnp.full((1, width), MASK_VALUE, F32), jnp.zeros((1, width), F32),
                  scores(ch, 0, 0), jnp.ones((1, width), F32)) for ch in chains)
    carries = list(lax.fori_loop(0, (ATTN_CHAINS // 2) * i, pair, init))

    for k in chains:
        j = ATTN_CHAINS * i + k
        for ch in chains[k + 1:]:
            carries[ch] = step(ch, j, k % 2, carries[ch])
        finish(k, j, k % 2, carries[k])


def _diff_attn_fixed_kernel(q_ref, k_ref, v_ref, qg_ref, kg_ref, slope_ref, lam_ref, sg_ref, bound_ref,
                            o_ref, ka_scr, vt_scr, wt_scr, acc_scr, s_scr, p_scr, dm_scr, *, t, seq):
    i = pl.program_id(2)
    nqc = t // LANES
    width = nqc * 2 * LANES
    chains = tuple(range(ATTN_CHAINS))
    lane = lax.broadcasted_iota(jnp.int32, (1, LANES), 1)
    first_map = lane < DIFF_DH

    gr = lax.broadcasted_iota(jnp.int32, (LANES, LANES), 0) // DIFF_DH
    gc = lax.broadcasted_iota(jnp.int32, (LANES, LANES), 1) // DIFF_DH
    gsum = jnp.where(gr == gc, 1.0, 0.0).astype(BF16)
    slope2 = slope_ref[...] * LOG2E
    slope2_w = jnp.tile(slope2, (1, width // LANES))
    row_t = lax.broadcasted_iota(jnp.int32, (t, LANES), 0)

    def split_terms(value, ones_first):
        rem = value
        lo, hi = (BIAS_TERMS, 0) if ones_first else (0, BIAS_TERMS)
        lanes = lax.broadcasted_iota(jnp.int32, value.shape, 1)
        out = jnp.where((lanes >= hi) & (lanes < hi + BIAS_TERMS), 1.0, 0.0)
        for term in range(BIAS_TERMS):
            part = rem.astype(BF16).astype(F32)
            out = jnp.where(lanes == lo + term, part, out)
            rem = rem - part
        return out

    @pl.when(i == 0)
    def _():
        jd = lax.broadcasted_iota(jnp.int32, (t, width), 0)
        col = lax.broadcasted_iota(jnp.int32, (t, width), 1)
        il = (col & (LANES - 1)) + lax.shift_right_logical(col, 8) * LANES
        ahead = jnp.minimum(il - jd, 0).astype(F32)
        dm_scr[...] = jnp.where(_chunk_id(jd) <= _chunk_id(il), (2.0 * slope2_w) * ahead, MASK_VALUE)

        def body(r, carry):
            rows = pl.ds(pl.multiple_of(r * t, t), t)
            kn = _group_rms(k_ref[rows, :].astype(F32), gsum, kg_ref[...])
            ka_scr[rows, 0:LANES] = kn.astype(BF16)
            key_pos = (row_t + r * t).astype(F32)
            ka_scr[rows, LANES:2 * LANES] = split_terms(slope2 * key_pos, False).astype(BF16)
            vt_scr[r] = v_ref[rows, :].astype(F32).T.astype(BF16)
            return carry
        lax.fori_loop(0, seq // t, body, 0)

    qn = _group_rms(q_ref[...].astype(F32), gsum, qg_ref[...]) * (DIFF_DH ** -0.5 * LOG2E)
    row_q = lax.broadcasted_iota(jnp.int32, (LANES, LANES), 0)
    for ch in chains:
        for qc in range(nqc):
            r0 = ch * t + qc * LANES
            qq = qn[r0:r0 + LANES, :]
            q_pos = (row_q + (i * (ATTN_CHAINS * t) + r0)).astype(F32)
            side = split_terms(-(slope2 * q_pos + bound_ref[...]), True)
            w = jnp.concatenate(
                [jnp.concatenate([jnp.where(first_map, qq, 0.0), side], axis=1),
                 jnp.concatenate([jnp.where(first_map, 0.0, qq), side], axis=1)], axis=0)
            wt_scr[ch, :, qc * 2 * LANES:(qc + 1) * 2 * LANES] = w.T.astype(BF16)
        p_scr[ch, 1] = jnp.zeros(p_scr.shape[2:], BF16)
    acc_scr[...] = jnp.zeros_like(acc_scr)

    def scores(ch, j, slot):
        rows = pl.ds(pl.multiple_of(j * t, t), t)
        s_scr[ch, slot] = _dot(ka_scr[rows, :], wt_scr[ch])

    def accumulate(ch, j, slot):
        acc_scr[ch] += _dot(vt_scr[jnp.maximum(j, 0)], p_scr[ch, slot])

    def expo(ch, slot, st, l_part):
        p = jnp.exp2(st)
        p_scr[ch, slot] = p.astype(BF16)
        return l_part + jnp.sum(p.reshape(t // SUBLANES, SUBLANES, width), axis=0)

    def step(ch, j, slot, l_part):
        scores(ch, j + 1, 1 - slot)
        accumulate(ch, j - 1, 1 - slot)
        return expo(ch, slot, s_scr[ch, slot], l_part)

    def finish(ch, j, slot, l_part):
        accumulate(ch, j - 1, 1 - slot)
        l_part = expo(ch, slot, s_scr[ch, slot] + dm_scr[...], l_part)
        accumulate(ch, j, slot)
        on = acc_scr[ch] * (1.0 / jnp.sum(l_part, axis=0, keepdims=True))
        for qc in range(nqc):
            c0 = qc * 2 * LANES
            ot = on[:, c0:c0 + LANES] - lam_ref[...] * on[:, c0 + LANES:c0 + 2 * LANES]
            o = ot.T
            msq = jnp.mean(o * o, axis=-1, keepdims=True)
            y = o * lax.rsqrt(msq + EPS) * sg_ref[...] * (1.0 - LAMBDA_INIT)
            r0 = ch * t + qc * LANES
            o_ref[r0:r0 + LANES, :] = y.astype(BF16)

    def pair(pp, parts):
        parts = list(parts)
        for j, slot in ((2 * pp, 0), (2 * pp + 1, 1)):
            for ch in chains:
                parts[ch] = step(ch, j, slot, parts[ch])
        return tuple(parts)

    for ch in chains:
        scores(ch, 0, 0)
    parts = list(lax.fori_loop(0, (ATTN_CHAINS // 2) * i, pair,
                               tuple(jnp.zeros((SUBLANES, width), F32) for _ in chains)))

    for k in chains:
        j = ATTN_CHAINS * i + k
        for ch in chains[k + 1:]:
            parts[ch] = step(ch, j, k % 2, parts[ch])
        finish(k, j, k % 2, parts[k])


def _diff_attn(proj, qg2, kg2, slopes, lam, subln_g, B, S, bound=None):
    N = proj.shape[0]
    t = min(256, S // ATTN_CHAINS)
    tq = ATTN_CHAINS * t
    nq = S // tq
    vec = pl.BlockSpec((1, LANES), lambda b, h, i: (0, 0))
    body = _diff_attn_kernel if bound is None else _diff_attn_fixed_kernel
    extra = () if bound is None else (bound,)
    return pl.pallas_call(
        functools.partial(body, t=t, seq=S),
        grid=(B, DIFF_HEADS, nq),
        in_specs=[
            pl.BlockSpec((tq, LANES), lambda b, h, i: (b * nq + i, COL_DQ // LANES + h)),
            pl.BlockSpec((S, LANES), lambda b, h, i: (b, COL_DK // LANES + h)),
            pl.BlockSpec((S, LANES), lambda b, h, i: (b, COL_DV // LANES + h)),
            vec, vec,
            pl.BlockSpec((None, 1, LANES), lambda b, h, i: (h, 0, 0)),
            vec, vec,
        ] + [vec] * len(extra),
        out_specs=pl.BlockSpec((tq, DIFF_DV), lambda b, h, i: (b * nq + i, h)),
        out_shape=jax.ShapeDtypeStruct((N, DIFF_HEADS * DIFF_DV), BF16),
        scratch_shapes=[
            pltpu.VMEM((S, 2 * LANES), BF16),
            pltpu.VMEM((S // t, DIFF_DV, t), BF16),
            pltpu.VMEM((ATTN_CHAINS, 2 * LANES, 2 * t), BF16),
            pltpu.VMEM((ATTN_CHAINS, DIFF_DV, 2 * t), F32),
            pltpu.VMEM((ATTN_CHAINS, 2, t, 2 * t), F32),
            pltpu.VMEM((ATTN_CHAINS, 2, t, 2 * t), BF16),
            pltpu.VMEM((t, 2 * t), F32),
        ],
        compiler_params=pltpu.CompilerParams(
            dimension_semantics=("parallel", "parallel", "arbitrary"), vmem_limit_bytes=VMEM_LIMIT),
        name="diff_attn" if bound is None else "diff_attn_fixed",
    )(proj, proj, proj, qg2, kg2, slopes, lam, subln_g, *extra)


def _merge_kernel(oa_ref, ob_ref, ga_ref, gb_ref, x_ref, mod_ref, wa_ref, wb_ref, wo_ref, g2_ref,
                  x1_ref, h2_ref):
    ya = _dot(oa_ref[...], wa_ref[...])
    yb = _dot(ob_ref[...], wb_ref[...])
    merged = _sigmoid(ga_ref[...].astype(F32)) * ya + _sigmoid(gb_ref[...].astype(F32)) * yb
    x1 = x_ref[...] + mod_ref[2:3, :] * _dot(merged.astype(BF16), wo_ref[...])
    x1_ref[...] = x1
    ms = jnp.mean(x1 * x1, axis=-1, keepdims=True)
    y = x1 * lax.rsqrt(ms + EPS) * g2_ref[...]
    h2_ref[...] = (y * (1.0 + mod_ref[4:5, :]) + mod_ref[3:4, :]).astype(BF16)


def _merge(o_a, o_b, proj, x2, mod3, w_gla_o, w_diff_o, w_out, g2, S):
    N, D = x2.shape
    tm = min(512, S)
    tiles_per_seq = S // tm
    tok = lambda col: pl.BlockSpec((tm, D), lambda i: (i, col))
    wspec = pl.BlockSpec((D, D), lambda i: (0, 0))
    return pl.pallas_call(
        _merge_kernel,
        grid=(N // tm,),
        in_specs=[
            tok(0), tok(0), tok(COL_GA // D), tok(COL_GB // D), tok(0),
            pl.BlockSpec((None, 6, D), lambda i: (i // tiles_per_seq, 0, 0)),
            wspec, wspec, wspec,
            pl.BlockSpec((1, D), lambda i: (0, 0)),
        ],
        out_specs=[tok(0), tok(0)],
        out_shape=[jax.ShapeDtypeStruct((N, D), F32), jax.ShapeDtypeStruct((N, D), BF16)],
        compiler_params=pltpu.CompilerParams(
            dimension_semantics=("parallel",), vmem_limit_bytes=VMEM_LIMIT),
        name="merge",
    )(o_a, o_b, proj, proj, x2, mod3, w_gla_o, w_diff_o, w_out, g2)


def _conv_ffn_kernel(h2_ref, x1_ref, mod_ref, wup_ref, cw_ref, cb_ref, wdn_ref, o_ref,
                     carry_ref, acc_ref, ubuf_ref, act_ref, *, tm, tf, tiles_per_seq):
    @pl.when(pl.program_id(0) % tiles_per_seq == 0)
    def _():
        carry_ref[...] = jnp.zeros_like(carry_ref)

    h2 = h2_ref[...]

    def up(f):
        return (_dot(h2, wup_ref[:, f * tf:(f + 1) * tf]),
                _dot(h2, wup_ref[:, D_FF + f * tf:D_FF + (f + 1) * tf]))

    def conv(u, col0, slot, half):
        cols = slice(half * tf, (half + 1) * tf)
        ubuf_ref[slot, 0:SUBLANES, cols] = carry_ref[:, col0:col0 + tf]
        ubuf_ref[slot, SUBLANES:, cols] = u
        carry_ref[:, col0:col0 + tf] = u[tm - SUBLANES:, :]
        r1 = ubuf_ref[slot, SUBLANES - 1:SUBLANES - 1 + tm, cols]
        r2 = ubuf_ref[slot, SUBLANES - 2:SUBLANES - 2 + tm, cols]
        cw = cw_ref[:, col0:col0 + tf]
        return cw[0:1, :] * r2 + cw[1:2, :] * r1 + cw[2:3, :] * u + cb_ref[:, col0:col0 + tf]

    nf = D_FF // tf
    u_next = up(0)
    for f in range(nf):
        ua, ub = u_next
        if f + 1 < nf:
            u_next = up(f + 1)
        a = conv(ua, f * tf, f % 2, 0)
        b = conv(ub, D_FF + f * tf, f % 2, 1)
        act_ref[:, (f % DOWN_GROUP) * tf:(f % DOWN_GROUP + 1) * tf] = (a * _sigmoid(a) * b).astype(BF16)
        if (f + 1) % DOWN_GROUP == 0 or f + 1 == nf:
            f0 = f - f % DOWN_GROUP
            contrib = _dot(act_ref[:, :(f + 1 - f0) * tf], wdn_ref[f0 * tf:(f + 1) * tf, :])
            if f0 == 0:
                acc_ref[...] = contrib
            else:
                acc_ref[...] += contrib

    o_ref[...] = x1_ref[...] + mod_ref[5:6, :] * acc_ref[...]


def _conv_ffn(h2, x1, mod3, w_up, conv_w, conv_b, w_down, S):
    N, D = x1.shape
    tm = min(256, S)
    tf = 256
    tiles_per_seq = S // tm
    tok = pl.BlockSpec((tm, D), lambda i: (i, 0))
    full = lambda a: pl.BlockSpec(a.shape, lambda i: (0, 0))
    return pl.pallas_call(
        functools.partial(_conv_ffn_kernel, tm=tm, tf=tf, tiles_per_seq=tiles_per_seq),
        grid=(N // tm,),
        in_specs=[
            tok, tok,
            pl.BlockSpec((None, 6, D), lambda i: (i // tiles_per_seq, 0, 0)),
            full(w_up), full(conv_w), full(conv_b), full(w_down),
        ],
        out_specs=tok,
        out_shape=jax.ShapeDtypeStruct((N, D), F32),
        scratch_shapes=[
            pltpu.VMEM((SUBLANES, 2 * D_FF), F32),
            pltpu.VMEM((tm, D), F32),
            pltpu.VMEM((2, SUBLANES + tm, 2 * tf), F32),
            pltpu.VMEM((tm, DOWN_GROUP * tf), BF16),
        ],
        compiler_params=pltpu.CompilerParams(
            dimension_semantics=("arbitrary",), vmem_limit_bytes=VMEM_LIMIT),
        name="conv_ffn",
    )(h2, x1, mod3, w_up, conv_w, conv_b, w_down)


def kernel(x, c, w_ada, b_ada, norm1_g, w_in, w_alpha_up, b_alpha, gla_norm_g, q_norm_g, k_norm_g,
           lam_q1, lam_k1, lam_q2, lam_k2, diff_norm_g, w_gla_o, w_diff_o, w_out, norm2_g, w_up,
           conv_w, conv_b, w_down):
    B, S, D = x.shape
    N = B * S
    assert D == D_MODEL and S % CHUNK == 0 and w_ada.shape[0] == 1

    w0 = w_in[0]
    a0 = COL_GG + GLA_HEADS * GLA_DV
    w_in_r = jnp.concatenate(
        [w0[:, :a0], w0[:, a0 + GLA_RANK:],
         jnp.pad(w0[:, a0:a0 + GLA_RANK], ((0, 0), (0, LANES - GLA_RANK)))], axis=1).astype(BF16)
    wa_pad = jnp.pad(w_alpha_up[0], ((0, LANES - GLA_RANK), (0, 0))).astype(BF16)
    slopes = 2.0 ** (-8.0 * (jnp.arange(DIFF_HEADS, dtype=F32) + 1.0) / DIFF_HEADS)
    slopes = jnp.broadcast_to(slopes[:, None, None], (DIFF_HEADS, 1, LANES))
    qg2 = jnp.tile(q_norm_g, (1, 2))
    kg2 = jnp.tile(k_norm_g, (1, 2))

    mod, lam = _ada_mod(c, w_ada[0], b_ada, lam_q1, lam_k1, lam_q2, lam_k2)
    mod3 = mod.reshape(B, 6, D)
    x2 = x.reshape(N, D)

    proj = _in_proj(x2, mod3, norm1_g, w_in_r, S)
    o_a = _gla(proj, wa_pad, b_alpha, gla_norm_g, B, S)
    bound = (LOGIT_BOUND_SLACK * DIFF_DH ** 0.5 * LOG2E) * jnp.max(jnp.abs(q_norm_g)) * jnp.max(jnp.abs(k_norm_g))
    o_b = lax.cond(
        bound <= MAX_FIXED_BOUND,
        lambda: _diff_attn(proj, qg2, kg2, slopes, lam, diff_norm_g, B, S,
                           bound=jnp.broadcast_to(bound, (1, LANES)).astype(F32)),
        lambda: _diff_attn(proj, qg2, kg2, slopes, lam, diff_norm_g, B, S))
    x1, h2 = _merge(o_a, o_b, proj, x2, mod3, w_gla_o[0].astype(BF16), w_diff_o[0].astype(BF16),
                    w_out[0].astype(BF16), norm2_g, S)
    out = _conv_ffn(h2, x1, mod3, w_up[0].astype(BF16), conv_w[0], conv_b, w_down[0].astype(BF16), S)
    return out.reshape(B, S, D)
```

```python
import functools
import math

import jax
import jax.numpy as jnp
from jax import lax
from jax.experimental import pallas as pl
from jax.experimental.pallas import tpu as pltpu

F32 = jnp.float32
BF16 = jnp.bfloat16

D_MODEL = 1024
CHUNK = 64
EPS = 1e-6
GLA_HEADS = 4
GLA_DK = 128
GLA_DV = 256
GLA_RANK = 16
GLA_TAU = 16.0
GLA_HEADS_PER_STEP = 2
DIFF_HEADS = 8
DIFF_DH = 64
DIFF_DV = 128
D_FF = 2816
CONV_W = 3
DOWN_GROUP = 4
LAMBDA_INIT = 0.8 - 0.6 * math.exp(-0.3 * 0)

LANES = 128
SUBLANES = 8
MASK_VALUE = -1e30
LOG2E = math.log2(math.e)
BIAS_TERMS = 3
ATTN_CHAINS = 8
LOGIT_BOUND_SLACK = 1.02
MAX_FIXED_BOUND = 50.0

COL_GQ, COL_GK, COL_GV, COL_GG = 0, 512, 1024, 2048
COL_DQ, COL_DK, COL_DV = 3072, 4096, 5120
COL_GA, COL_GB, COL_ACODE = 6144, 7168, 8192
PROJ_COLS = 8192 + LANES
PROJ_TN = PROJ_COLS // 5

VMEM_LIMIT = 56 * 1024 * 1024


def _dot(a, b):
    return jnp.dot(a, b, preferred_element_type=F32)


def _dot_nt(a, b):
    return lax.dot_general(a, b, (((1,), (1,)), ((), ())), preferred_element_type=F32)


def _dot_tn(a, b):
    return lax.dot_general(a, b, (((0,), (0,)), ((), ())), preferred_element_type=F32)


def _split_bf16(v):
    hi = v.astype(BF16)
    lo = (v - hi.astype(F32)).astype(BF16)
    return hi, lo


def _sigmoid(v):
    return 1.0 / (1.0 + jnp.exp(-v))


def _chunk_id(pos):
    return lax.shift_right_logical(pos, CHUNK.bit_length() - 1)


def _ada_kernel(c_ref, w_ref, b_ref, lq1_ref, lk1_ref, lq2_ref, lk2_ref, mod_ref, lam_ref):
    c = c_ref[...]
    a = c * _sigmoid(c)
    a_hi, a_lo = _split_bf16(a)
    w_hi, w_lo = _split_bf16(w_ref[...])
    mod_ref[...] = _dot(a_hi, w_hi) + _dot(a_lo, w_hi) + _dot(a_hi, w_lo) + b_ref[...]
    s1 = jnp.sum(lq1_ref[...] * lk1_ref[...], axis=-1, keepdims=True)
    s2 = jnp.sum(lq2_ref[...] * lk2_ref[...], axis=-1, keepdims=True)
    lam = jnp.exp(s1) - jnp.exp(s2) + LAMBDA_INIT
    lam_ref[...] = jnp.broadcast_to(lam, lam_ref.shape)


def _ada_mod(c, w_ada, b_ada, lq1, lk1, lq2, lk2):
    B, D = c.shape
    n_out = w_ada.shape[1]
    tn = D
    small = pl.BlockSpec((1, DIFF_DH), lambda j: (0, 0))
    return pl.pallas_call(
        _ada_kernel,
        grid=(n_out // tn,),
        in_specs=[
            pl.BlockSpec((B, D), lambda j: (0, 0)),
            pl.BlockSpec((D, tn), lambda j: (0, j)),
            pl.BlockSpec((1, tn), lambda j: (0, j)),
            small, small, small, small,
        ],
        out_specs=[
            pl.BlockSpec((B, tn), lambda j: (0, j)),
            pl.BlockSpec((1, LANES), lambda j: (0, 0)),
        ],
        out_shape=[
            jax.ShapeDtypeStruct((B, n_out), F32),
            jax.ShapeDtypeStruct((1, LANES), F32),
        ],
        compiler_params=pltpu.CompilerParams(dimension_semantics=("arbitrary",)),
        name="ada_mod",
    )(c, w_ada, b_ada, lq1, lk1, lq2, lk2)


def _in_proj_kernel(x_ref, mod_ref, g_ref, w_ref, o_ref, h_scr):
    @pl.when(pl.program_id(1) == 0)
    def _():
        x = x_ref[...]
        ms = jnp.mean(x * x, axis=-1, keepdims=True)
        y = x * lax.rsqrt(ms + EPS) * g_ref[...]
        h = y * (1.0 + mod_ref[1:2, :]) + mod_ref[0:1, :]
        h_scr[...] = h.astype(BF16)

    o_ref[...] = _dot(h_scr[...], w_ref[...]).astype(BF16)


def _in_proj(x2, mod3, g1, w_in_r, S):
    N, D = x2.shape
    tm = min(1024, S)
    tiles_per_seq = S // tm
    return pl.pallas_call(
        _in_proj_kernel,
        grid=(N // tm, PROJ_COLS // PROJ_TN),
        in_specs=[
            pl.BlockSpec((tm, D), lambda i, j: (i, 0)),
            pl.BlockSpec((None, 6, D), lambda i, j: (i // tiles_per_seq, 0, 0)),
            pl.BlockSpec((1, D), lambda i, j: (0, 0)),
            pl.BlockSpec((D, PROJ_TN), lambda i, j: (0, j)),
        ],
        out_specs=pl.BlockSpec((tm, PROJ_TN), lambda i, j: (i, j)),
        out_shape=jax.ShapeDtypeStruct((N, PROJ_COLS), BF16),
        scratch_shapes=[pltpu.VMEM((tm, D), BF16)],
        compiler_params=pltpu.CompilerParams(
            dimension_semantics=("parallel", "arbitrary"), vmem_limit_bytes=VMEM_LIMIT),
        name="in_proj",
    )(x2, mod3, g1, w_in_r)


def _gla_kernel(q_ref, k_ref, v_ref, g_ref, a_ref, wa_ref, ba_ref, ng_ref, o_ref, st_ref, *, tc):
    nchunk = tc // CHUNK

    @pl.when(pl.program_id(2) == 0)
    def _():
        st_ref[...] = jnp.zeros_like(st_ref)

    row = lax.broadcasted_iota(jnp.int32, (tc, tc), 0)
    col = lax.broadcasted_iota(jnp.int32, (tc, tc), 1)
    chunk_gap = _chunk_id(row) - _chunk_id(col)
    lower = (chunk_gap == 0) & (col <= row)
    upper = (chunk_gap == 0) & (col > row)
    tri = jnp.where(lower, 1.0, 0.0).astype(BF16)

    def per_chunk(rows):
        return jnp.concatenate([jnp.broadcast_to(r, (CHUNK, GLA_DK)) for r in rows], axis=0)

    def head(hh):
        dk = slice(hh * GLA_DK, (hh + 1) * GLA_DK)
        dv = slice(hh * GLA_DV, (hh + 1) * GLA_DV)
        z = _dot(a_ref[...], wa_ref[:, dk]) + ba_ref[:, dk]
        yield
        log_a = (jnp.minimum(z, 0.0) - jnp.log1p(jnp.exp(-jnp.abs(z)))) * (1.0 / GLA_TAU)
        la_hi, la_lo = _split_bf16(log_a)
        bcum = _dot(tri, la_hi) + _dot(tri, la_lo)
        yield
        b_last = [bcum[c * CHUNK + CHUNK - 1:(c + 1) * CHUNK, :] for c in range(nchunk)]
        prefix = [jnp.zeros_like(b_last[0])]
        for c in range(nchunk):
            prefix.append(prefix[-1] + b_last[c])
        one = jnp.ones_like(b_last[0])

        eb = jnp.exp(bcum)
        enb = jnp.exp(-bcum)
        qs = q_ref[:, dk].astype(F32) * (GLA_DK ** -0.5)
        k = k_ref[:, dk].astype(F32)
        v = v_ref[:, dv]
        q_fwd = qs * eb
        q_f = q_fwd.astype(BF16)
        k_dec = k * jnp.exp(per_chunk(b_last) - bcum)
        k_d = k_dec.astype(BF16)

        s_fwd = _dot_nt(q_f, (k * enb).astype(BF16))
        s_bwd = _dot_nt((qs * enb).astype(BF16), (k * eb).astype(BF16))
        yield
        scores = jnp.where(lower, s_fwd, jnp.where(upper, s_bwd, 0.0))
        for off in range(1, nchunk):
            if off == 1:
                q_x = q_f
            else:
                between = [one] * off + [jnp.exp(prefix[c] - prefix[c - off + 1]) for c in range(off, nchunk)]
                q_x = (q_fwd * per_chunk(between)).astype(BF16)
            scores = jnp.where(chunk_gap == off, _dot_nt(q_x, k_d), scores)
            yield

        st = st_ref[hh]
        q_s = (q_fwd * per_chunk([jnp.exp(p) for p in prefix[:nchunk]])).astype(BF16)
        o = _dot(scores.astype(BF16), v) + _dot_nt(q_s, st.astype(BF16))
        to_end = [jnp.exp(prefix[nchunk] - prefix[c + 1]) for c in range(nchunk)]
        st_ref[hh] = st * jnp.exp(prefix[nchunk]) + _dot_tn(v, (k_dec * per_chunk(to_end)).astype(BF16))
        yield

        ms = jnp.mean(o * o, axis=-1, keepdims=True)
        y = o * lax.rsqrt(ms + EPS) * ng_ref[...]
        g = g_ref[:, dv].astype(F32)
        o_ref[:, dv] = (y * (g * _sigmoid(g))).astype(BF16)

    heads = [head(hh) for hh in range(GLA_HEADS_PER_STEP)]
    while heads:
        heads = [h for h in heads if next(h, True) is None]


def _gla(proj, wa_pad, b_alpha, gla_norm_g, B, S):
    N = proj.shape[0]
    tc = min(256, S)
    nt = S // tc
    hps = GLA_HEADS_PER_STEP
    row = lambda b, h, i: b * nt + i
    return pl.pallas_call(
        functools.partial(_gla_kernel, tc=tc),
        grid=(B, GLA_HEADS // hps, nt),
        in_specs=[
            pl.BlockSpec((tc, hps * GLA_DK), lambda b, h, i: (row(b, h, i), COL_GQ // (hps * GLA_DK) + h)),
            pl.BlockSpec((tc, hps * GLA_DK), lambda b, h, i: (row(b, h, i), COL_GK // (hps * GLA_DK) + h)),
            pl.BlockSpec((tc, hps * GLA_DV), lambda b, h, i: (row(b, h, i), COL_GV // (hps * GLA_DV) + h)),
            pl.BlockSpec((tc, hps * GLA_DV), lambda b, h, i: (row(b, h, i), COL_GG // (hps * GLA_DV) + h)),
            pl.BlockSpec((tc, LANES), lambda b, h, i: (row(b, h, i), COL_ACODE // LANES)),
            pl.BlockSpec((LANES, hps * GLA_DK), lambda b, h, i: (0, h)),
            pl.BlockSpec((1, hps * GLA_DK), lambda b, h, i: (0, h)),
            pl.BlockSpec((1, GLA_DV), lambda b, h, i: (0, 0)),
        ],
        out_specs=pl.BlockSpec((tc, hps * GLA_DV), lambda b, h, i: (row(b, h, i), h)),
        out_shape=jax.ShapeDtypeStruct((N, GLA_HEADS * GLA_DV), BF16),
        scratch_shapes=[pltpu.VMEM((hps, GLA_DV, GLA_DK), F32)],
        compiler_params=pltpu.CompilerParams(
            dimension_semantics=("parallel", "parallel", "arbitrary"), vmem_limit_bytes=VMEM_LIMIT),
        name="gla",
    )(proj, proj, proj, proj, proj, wa_pad, b_alpha, gla_norm_g)


def _group_rms(xf, gsum, gain):
    hi, lo = _split_bf16(xf * xf)
    ss = _dot(hi, gsum) + _dot(lo, gsum)
    return xf * lax.rsqrt(ss * (1.0 / DIFF_DH) + EPS) * gain


def _diff_attn_kernel(q_ref, k_ref, v_ref, qg_ref, kg_ref, slope_ref, lam_ref, sg_ref, o_ref,
                      ka_scr, vt_scr, wt_scr, acc_scr, s_scr, p_scr, dm_scr, *, t, seq):
    i = pl.program_id(2)
    nqc = t // LANES
    width = nqc * 2 * LANES
    chains = tuple(range(ATTN_CHAINS))
    lane = lax.broadcasted_iota(jnp.int32, (1, LANES), 1)
    first_map = lane < DIFF_DH

    gr = lax.broadcasted_iota(jnp.int32, (LANES, LANES), 0) // DIFF_DH
    gc = lax.broadcasted_iota(jnp.int32, (LANES, LANES), 1) // DIFF_DH
    gsum = jnp.where(gr == gc, 1.0, 0.0).astype(BF16)
    slope2 = slope_ref[...] * LOG2E
    slope2_w = jnp.tile(slope2, (1, width // LANES))

    @pl.when(i == 0)
    def _():
        jl = lax.broadcasted_iota(jnp.int32, (t, LANES), 0).astype(F32)
        lane_t = lax.broadcasted_iota(jnp.int32, (t, LANES), 1)
        rem = slope2 * jl
        aux = jnp.zeros((t, LANES), F32)
        for term in range(BIAS_TERMS):
            part = rem.astype(BF16).astype(F32)
            aux = jnp.where(lane_t == term, part, aux)
            rem = rem - part
        aux = aux.astype(BF16)

        jd = lax.broadcasted_iota(jnp.int32, (t, width), 0)
        col = lax.broadcasted_iota(jnp.int32, (t, width), 1)
        il = (col & (LANES - 1)) + lax.shift_right_logical(col, 8) * LANES
        ahead = jnp.minimum(il - jd, 0).astype(F32)
        dm_scr[...] = jnp.where(_chunk_id(jd) <= _chunk_id(il), (2.0 * slope2_w) * ahead, MASK_VALUE)

        def body(r, carry):
            rows = pl.ds(pl.multiple_of(r * t, t), t)
            kn = _group_rms(k_ref[rows, :].astype(F32), gsum, kg_ref[...])
            ka_scr[rows, 0:LANES] = kn.astype(BF16)
            ka_scr[rows, LANES:2 * LANES] = aux
            vt_scr[r] = v_ref[rows, :].astype(F32).T.astype(BF16)
            return carry
        lax.fori_loop(0, seq // t, body, 0, unroll=2)

    qn = _group_rms(q_ref[...].astype(F32), gsum, qg_ref[...]) * (DIFF_DH ** -0.5 * LOG2E)
    ones = jnp.broadcast_to(jnp.where(lane < BIAS_TERMS, 1.0, 0.0), (LANES, LANES))
    for ch in chains:
        for qc in range(nqc):
            r0 = ch * t + qc * LANES
            qq = qn[r0:r0 + LANES, :]
            w = jnp.concatenate(
                [jnp.concatenate([jnp.where(first_map, qq, 0.0), ones], axis=1),
                 jnp.concatenate([jnp.where(first_map, 0.0, qq), ones], axis=1)], axis=0)
            wt_scr[ch, :, qc * 2 * LANES:(qc + 1) * 2 * LANES] = w.T.astype(BF16)
        p_scr[ch, 1] = jnp.zeros(p_scr.shape[2:], BF16)
    acc_scr[...] = jnp.zeros_like(acc_scr)

    def scores(ch, j, slot):
        rows = pl.ds(pl.multiple_of(j * t, t), t)
        st = _dot(ka_scr[rows, :], wt_scr[ch])
        s_scr[ch, slot] = st
        return jnp.max(st, axis=0, keepdims=True)

    def accumulate(ch, j, slot, alpha):
        acc_scr[ch] = alpha * acc_scr[ch] + _dot(vt_scr[jnp.maximum(j, 0)], p_scr[ch, slot])

    def softmax(ch, j, st, m_cur, m_prev, l_prev):
        shift = slope2_w * ((j - (ATTN_CHAINS * i + ch)) * t).astype(F32)
        m_new = jnp.maximum(m_prev, m_cur + shift)
        p = jnp.exp2(st - (m_new - shift))
        alpha = jnp.exp2(m_prev - m_new)
        l_new = alpha * l_prev + jnp.sum(p, axis=0, keepdims=True)
        return p.astype(BF16), alpha, m_new, l_new

    def step(ch, j, slot, carry):
        m, l, m_cur, alpha_prev = carry
        m_next = scores(ch, j + 1, 1 - slot)
        accumulate(ch, j - 1, 1 - slot, alpha_prev)
        p, alpha, m, l = softmax(ch, j, s_scr[ch, slot], m_cur, m, l)
        p_scr[ch, slot] = p
        return m, l, m_next, alpha

    def finish(ch, j, slot, carry):
        m, l, _, alpha_prev = carry
        accumulate(ch, j - 1, 1 - slot, alpha_prev)
        st = s_scr[ch, slot] + dm_scr[...]
        p, alpha, m, l = softmax(ch, j, st, jnp.max(st, axis=0, keepdims=True), m, l)
        on = (alpha * acc_scr[ch] + _dot(vt_scr[j], p)) * (1.0 / l)
        for qc in range(nqc):
            c0 = qc * 2 * LANES
            ot = on[:, c0:c0 + LANES] - lam_ref[...] * on[:, c0 + LANES:c0 + 2 * LANES]
            o = ot.T
            msq = jnp.mean(o * o, axis=-1, keepdims=True)
            y = o * lax.rsqrt(msq + EPS) * sg_ref[...] * (1.0 - LAMBDA_INIT)
            r0 = ch * t + qc * LANES
            o_ref[r0:r0 + LANES, :] = y.astype(BF16)

    def pair(pp, carries):
        carries = list(carries)
        for j, slot in ((2 * pp, 0), (2 * pp + 1, 1)):
            for ch in chains:
                carries[ch] = step(ch, j, slot, carries[ch])
        return tuple(carries)

    init = tuple((jnp.full((1, width), MASK_VALUE, F32), jnp.zeros((1, width), F32),
                  scores(ch, 0, 0), jnp.ones((1, width), F32)) for ch in chains)
    carries = list(lax.fori_loop(0, (ATTN_CHAINS // 2) * i, pair, init))

    for k in chains:
        j = ATTN_CHAINS * i + k
        for ch in chains[k + 1:]:
            carries[ch] = step(ch, j, k % 2, carries[ch])
        finish(k, j, k % 2, carries[k])


def _diff_attn_fixed_kernel(q_ref, k_ref, v_ref, qg_ref, kg_ref, slope_ref, lam_ref, sg_ref, bound_ref,
                            o_ref, ka_scr, vt_scr, wt_scr, acc_scr, s_scr, p_scr, dm_scr, *, t, seq):
    i = pl.program_id(2)
    nqc = t // LANES
    width = nqc * 2 * LANES
    chains = tuple(range(ATTN_CHAINS))

    gr = lax.broadcasted_iota(jnp.int32, (LANES, LANES), 0) // DIFF_DH
    gc = lax.broadcasted_iota(jnp.int32, (LANES, LANES), 1) // DIFF_DH
    gsum = jnp.where(gr == gc, 1.0, 0.0).astype(BF16)
    slope2 = slope_ref[...] * LOG2E
    slope2_w = jnp.tile(slope2, (1, width // LANES))

    def bf16_terms(value):
        terms, rem = [], value
        for _ in range(BIAS_TERMS):
            terms.append(rem.astype(BF16).astype(F32))
            rem = rem - terms[-1]
        return terms

    @pl.when(i == 0)
    def _():
        jd = lax.broadcasted_iota(jnp.int32, (t, width), 0)
        col = lax.broadcasted_iota(jnp.int32, (t, width), 1)
        il = (col & (LANES - 1)) + lax.shift_right_logical(col, 8) * LANES
        ahead = jnp.minimum(il - jd, 0).astype(F32)
        dm_scr[...] = jnp.where(_chunk_id(jd) <= _chunk_id(il), (2.0 * slope2_w) * ahead, MASK_VALUE)

        lane_t = lax.broadcasted_iota(jnp.int32, (t, LANES), 1)
        row_t = lax.broadcasted_iota(jnp.int32, (t, LANES), 0).astype(F32)
        in_tile = jnp.where((lane_t >= 2 * BIAS_TERMS) & (lane_t < 3 * BIAS_TERMS), 1.0, 0.0)
        for n, term in enumerate(bf16_terms(slope2 * row_t)):
            in_tile = jnp.where(lane_t == n, term, in_tile)
        lane_1 = lax.broadcasted_iota(jnp.int32, (1, LANES), 1)

        def body(r, carry):
            rows = pl.ds(pl.multiple_of(r * t, t), t)
            kn = _group_rms(k_ref[rows, :].astype(F32), gsum, kg_ref[...])
            ka_scr[rows, 0:LANES] = kn.astype(BF16)
            start = jnp.zeros((1, LANES), F32)
            for n, term in enumerate(bf16_terms(slope2 * (r * t).astype(F32))):
                start = jnp.where(lane_1 == BIAS_TERMS + n, term, start)
            tile_lanes = (lane_t >= BIAS_TERMS) & (lane_t < 2 * BIAS_TERMS)
            ka_scr[rows, LANES:2 * LANES] = jnp.where(tile_lanes, start, in_tile).astype(BF16)
            vt_scr[r] = v_ref[rows, :].astype(F32).T.astype(BF16)
            return carry
        lax.fori_loop(0, seq // t, body, 0, unroll=2)

    qn = _group_rms(q_ref[...].astype(F32), gsum, qg_ref[...]) * (DIFF_DH ** -0.5 * LOG2E)
    row_q = lax.broadcasted_iota(jnp.int32, (LANES, LANES), 0)
    lane_q = lax.broadcasted_iota(jnp.int32, (1, LANES), 1)
    for ch in chains:
        for qc in range(nqc):
            r0 = ch * t + qc * LANES
            qt = qn[r0:r0 + LANES, :].T
            q_pos = (lane_q + (i * (ATTN_CHAINS * t) + r0)).astype(F32)
            side = jnp.where(row_q < 2 * BIAS_TERMS, 1.0, 0.0)
            for n, term in enumerate(bf16_terms(-(slope2 * q_pos + bound_ref[...]))):
                side = jnp.where(row_q == 2 * BIAS_TERMS + n, term, side)
            side = side.astype(BF16)
            c0 = qc * 2 * LANES
            wt_scr[ch, 0:LANES, c0:c0 + LANES] = jnp.where(row_q < DIFF_DH, qt, 0.0).astype(BF16)
            wt_scr[ch, 0:LANES, c0 + LANES:c0 + 2 * LANES] = jnp.where(row_q < DIFF_DH, 0.0, qt).astype(BF16)
            wt_scr[ch, LANES:2 * LANES, c0:c0 + LANES] = side
            wt_scr[ch, LANES:2 * LANES, c0 + LANES:c0 + 2 * LANES] = side
        p_scr[ch, 1] = jnp.zeros(p_scr.shape[2:], BF16)
    acc_scr[...] = jnp.zeros_like(acc_scr)

    def scores(ch, j, slot):
        rows = pl.ds(pl.multiple_of(j * t, t), t)
        s_scr[ch, slot] = _dot(ka_scr[rows, :], wt_scr[ch])

    def accumulate(ch, j, slot):
        acc_scr[ch] += _dot(vt_scr[jnp.maximum(j, 0)], p_scr[ch, slot])

    def expo(ch, slot, st, l_part):
        p = jnp.exp2(st)
        p_scr[ch, slot] = p.astype(BF16)
        return l_part + jnp.sum(p.reshape(t // SUBLANES, SUBLANES, width), axis=0)

    def step(ch, j, slot, l_part):
        scores(ch, j + 1, 1 - slot)
        accumulate(ch, j - 1, 1 - slot)
        return expo(ch, slot, s_scr[ch, slot], l_part)

    def finish(ch, j, slot, l_part):
        accumulate(ch, j - 1, 1 - slot)
        l_part = expo(ch, slot, s_scr[ch, slot] + dm_scr[...], l_part)
        accumulate(ch, j, slot)
        on = acc_scr[ch] * (1.0 / jnp.sum(l_part, axis=0, keepdims=True))
        for qc in range(nqc):
            c0 = qc * 2 * LANES
            ot = on[:, c0:c0 + LANES] - lam_ref[...] * on[:, c0 + LANES:c0 + 2 * LANES]
            o = ot.T
            msq = jnp.mean(o * o, axis=-1, keepdims=True)
            y = o * lax.rsqrt(msq + EPS) * sg_ref[...] * (1.0 - LAMBDA_INIT)
            r0 = ch * t + qc * LANES
            o_ref[r0:r0 + LANES, :] = y.astype(BF16)

    def pair(pp, parts):
        parts = list(parts)
        for j, slot in ((2 * pp, 0), (2 * pp + 1, 1)):
            for ch in chains:
                parts[ch] = step(ch, j, slot, parts[ch])
        return tuple(parts)

    for ch in chains:
        scores(ch, 0, 0)
    parts = list(lax.fori_loop(0, (ATTN_CHAINS // 2) * i, pair,
                               tuple(jnp.zeros((SUBLANES, width), F32) for _ in chains)))

    for k in chains:
        j = ATTN_CHAINS * i + k
        for ch in chains[k + 1:]:
            parts[ch] = step(ch, j, k % 2, parts[ch])
        finish(k, j, k % 2, parts[k])


def _diff_attn(proj, qg2, kg2, slopes, lam, subln_g, B, S, bound=None):
    N = proj.shape[0]
    t = min(256, S // ATTN_CHAINS)
    tq = ATTN_CHAINS * t
    nq = S // tq
    vec = pl.BlockSpec((1, LANES), lambda b, h, i: (0, 0))
    body = _diff_attn_kernel if bound is None else _diff_attn_fixed_kernel
    extra = () if bound is None else (bound,)
    return pl.pallas_call(
        functools.partial(body, t=t, seq=S),
        grid=(B, DIFF_HEADS, nq),
        in_specs=[
            pl.BlockSpec((tq, LANES), lambda b, h, i: (b * nq + i, COL_DQ // LANES + h)),
            pl.BlockSpec((S, LANES), lambda b, h, i: (b, COL_DK // LANES + h)),
            pl.BlockSpec((S, LANES), lambda b, h, i: (b, COL_DV // LANES + h)),
            vec, vec,
            pl.BlockSpec((None, 1, LANES), lambda b, h, i: (h, 0, 0)),
            vec, vec,
        ] + [vec] * len(extra),
        out_specs=pl.BlockSpec((tq, DIFF_DV), lambda b, h, i: (b * nq + i, h)),
        out_shape=jax.ShapeDtypeStruct((N, DIFF_HEADS * DIFF_DV), BF16),
        scratch_shapes=[
            pltpu.VMEM((S, 2 * LANES), BF16),
            pltpu.VMEM((S // t, DIFF_DV, t), BF16),
            pltpu.VMEM((ATTN_CHAINS, 2 * LANES, 2 * t), BF16),
            pltpu.VMEM((ATTN_CHAINS, DIFF_DV, 2 * t), F32),
            pltpu.VMEM((ATTN_CHAINS, 2, t, 2 * t), F32),
            pltpu.VMEM((ATTN_CHAINS, 2, t, 2 * t), BF16),
            pltpu.VMEM((t, 2 * t), F32),
        ],
        compiler_params=pltpu.CompilerParams(
            dimension_semantics=("parallel", "parallel", "arbitrary"), vmem_limit_bytes=VMEM_LIMIT),
        name="diff_attn" if bound is None else "diff_attn_fixed",
    )(proj, proj, proj, qg2, kg2, slopes, lam, subln_g, *extra)


def _merge_kernel(oa_ref, ob_ref, ga_ref, gb_ref, x_ref, mod_ref, wa_ref, wb_ref, wo_ref, g2_ref,
                  x1_ref, h2_ref):
    ya = _dot(oa_ref[...], wa_ref[...])
    yb = _dot(ob_ref[...], wb_ref[...])
    merged = _sigmoid(ga_ref[...].astype(F32)) * ya + _sigmoid(gb_ref[...].astype(F32)) * yb
    x1 = x_ref[...] + mod_ref[2:3, :] * _dot(merged.astype(BF16), wo_ref[...])
    x1_ref[...] = x1
    ms = jnp.mean(x1 * x1, axis=-1, keepdims=True)
    y = x1 * lax.rsqrt(ms + EPS) * g2_ref[...]
    h2_ref[...] = (y * (1.0 + mod_ref[4:5, :]) + mod_ref[3:4, :]).astype(BF16)


def _merge(o_a, o_b, proj, x2, mod3, w_gla_o, w_diff_o, w_out, g2, S):
    N, D = x2.shape
    tm = min(512, S)
    tiles_per_seq = S // tm
    tok = lambda col: pl.BlockSpec((tm, D), lambda i: (i, col))
    wspec = pl.BlockSpec((D, D), lambda i: (0, 0))
    return pl.pallas_call(
        _merge_kernel,
        grid=(N // tm,),
        in_specs=[
            tok(0), tok(0), tok(COL_GA // D), tok(COL_GB // D), tok(0),
            pl.BlockSpec((None, 6, D), lambda i: (i // tiles_per_seq, 0, 0)),
            wspec, wspec, wspec,
            pl.BlockSpec((1, D), lambda i: (0, 0)),
        ],
        out_specs=[tok(0), tok(0)],
        out_shape=[jax.ShapeDtypeStruct((N, D), F32), jax.ShapeDtypeStruct((N, D), BF16)],
        compiler_params=pltpu.CompilerParams(
            dimension_semantics=("parallel",), vmem_limit_bytes=VMEM_LIMIT),
        name="merge",
    )(o_a, o_b, proj, proj, x2, mod3, w_gla_o, w_diff_o, w_out, g2)


def _conv_ffn_kernel(h2_ref, x1_ref, mod_ref, wup_ref, cw_ref, cb_ref, wdn_ref, o_ref,
                     carry_ref, acc_ref, ubuf_ref, act_ref, *, tm, tf, tiles_per_seq):
    @pl.when(pl.program_id(0) % tiles_per_seq == 0)
    def _():
        carry_ref[...] = jnp.zeros_like(carry_ref)

    h2 = h2_ref[...]

    def up(f):
        return (_dot(h2, wup_ref[:, f * tf:(f + 1) * tf]),
                _dot(h2, wup_ref[:, D_FF + f * tf:D_FF + (f + 1) * tf]))

    def conv(u, col0, slot, half):
        cols = slice(half * tf, (half + 1) * tf)
        ubuf_ref[slot, 0:SUBLANES, cols] = carry_ref[:, col0:col0 + tf]
        ubuf_ref[slot, SUBLANES:, cols] = u
        carry_ref[:, col0:col0 + tf] = u[tm - SUBLANES:, :]
        r1 = ubuf_ref[slot, SUBLANES - 1:SUBLANES - 1 + tm, cols]
        r2 = ubuf_ref[slot, SUBLANES - 2:SUBLANES - 2 + tm, cols]
        cw = cw_ref[:, col0:col0 + tf]
        return cw[0:1, :] * r2 + cw[1:2, :] * r1 + cw[2:3, :] * u + cb_ref[:, col0:col0 + tf]

    nf = D_FF // tf
    u_next = up(0)
    for f in range(nf):
        ua, ub = u_next
        if f + 1 < nf:
            u_next = up(f + 1)
        a = conv(ua, f * tf, f % 2, 0)
        b = conv(ub, D_FF + f * tf, f % 2, 1)
        act_ref[:, (f % DOWN_GROUP) * tf:(f % DOWN_GROUP + 1) * tf] = (a * _sigmoid(a) * b).astype(BF16)
        if (f + 1) % DOWN_GROUP == 0 or f + 1 == nf:
            f0 = f - f % DOWN_GROUP
            contrib = _dot(act_ref[:, :(f + 1 - f0) * tf], wdn_ref[f0 * tf:(f + 1) * tf, :])
            if f0 == 0:
                acc_ref[...] = contrib
            else:
                acc_ref[...] += contrib

    o_ref[...] = x1_ref[...] + mod_ref[5:6, :] * acc_ref[...]


def _conv_ffn(h2, x1, mod3, w_up, conv_w, conv_b, w_down, S):
    N, D = x1.shape
    tm = min(256, S)
    tf = 256
    tiles_per_seq = S // tm
    tok = pl.BlockSpec((tm, D), lambda i: (i, 0))
    full = lambda a: pl.BlockSpec(a.shape, lambda i: (0, 0))
    return pl.pallas_call(
        functools.partial(_conv_ffn_kernel, tm=tm, tf=tf, tiles_per_seq=tiles_per_seq),
        grid=(N // tm,),
        in_specs=[
            tok, tok,
            pl.BlockSpec((None, 6, D), lambda i: (i // tiles_per_seq, 0, 0)),
            full(w_up), full(conv_w), full(conv_b), full(w_down),
        ],
        out_specs=tok,
        out_shape=jax.ShapeDtypeStruct((N, D), F32),
        scratch_shapes=[
            pltpu.VMEM((SUBLANES, 2 * D_FF), F32),
            pltpu.VMEM((tm, D), F32),
            pltpu.VMEM((2, SUBLANES + tm, 2 * tf), F32),
            pltpu.VMEM((tm, DOWN_GROUP * tf), BF16),
        ],
        compiler_params=pltpu.CompilerParams(
            dimension_semantics=("arbitrary",), vmem_limit_bytes=VMEM_LIMIT),
        name="conv_ffn",
    )(h2, x1, mod3, w_up, conv_w, conv_b, w_down)


def kernel(x, c, w_ada, b_ada, norm1_g, w_in, w_alpha_up, b_alpha, gla_norm_g, q_norm_g, k_norm_g,
           lam_q1, lam_k1, lam_q2, lam_k2, diff_norm_g, w_gla_o, w_diff_o, w_out, norm2_g, w_up,
           conv_w, conv_b, w_down):
    B, S, D = x.shape
    N = B * S
    assert D == D_MODEL and S % CHUNK == 0 and w_ada.shape[0] == 1

    w0 = w_in[0]
    a0 = COL_GG + GLA_HEADS * GLA_DV
    w_in_r = jnp.concatenate(
        [w0[:, :a0], w0[:, a0 + GLA_RANK:],
         jnp.pad(w0[:, a0:a0 + GLA_RANK], ((0, 0), (0, LANES - GLA_RANK)))], axis=1).astype(BF16)
    wa_pad = jnp.pad(w_alpha_up[0], ((0, LANES - GLA_RANK), (0, 0))).astype(BF16)
    slopes = 2.0 ** (-8.0 * (jnp.arange(DIFF_HEADS, dtype=F32) + 1.0) / DIFF_HEADS)
    slopes = jnp.broadcast_to(slopes[:, None, None], (DIFF_HEADS, 1, LANES))
    qg2 = jnp.tile(q_norm_g, (1, 2))
    kg2 = jnp.tile(k_norm_g, (1, 2))

    mod, lam = _ada_mod(c, w_ada[0], b_ada, lam_q1, lam_k1, lam_q2, lam_k2)
    mod3 = mod.reshape(B, 6, D)
    x2 = x.reshape(N, D)

    proj = _in_proj(x2, mod3, norm1_g, w_in_r, S)
    o_a = _gla(proj, wa_pad, b_alpha, gla_norm_g, B, S)
    bound = (LOGIT_BOUND_SLACK * DIFF_DH ** 0.5 * LOG2E) * jnp.max(jnp.abs(q_norm_g)) * jnp.max(jnp.abs(k_norm_g))
    o_b = lax.cond(
        bound <= MAX_FIXED_BOUND,
        lambda: _diff_attn(proj, qg2, kg2, slopes, lam, diff_norm_g, B, S,
                           bound=jnp.broadcast_to(bound, (1, LANES)).astype(F32)),
        lambda: _diff_attn(proj, qg2, kg2, slopes, lam, diff_norm_g, B, S))
    x1, h2 = _merge(o_a, o_b, proj, x2, mod3, w_gla_o[0].astype(BF16), w_diff_o[0].astype(BF16),
                    w_out[0].astype(BF16), norm2_g, S)
    out = _conv_ffn(h2, x1, mod3, w_up[0].astype(BF16), conv_w[0], conv_b, w_down[0].astype(BF16), S)
    return out.reshape(B, S, D)
```

```python
import functools
import math

import jax
import jax.numpy as jnp
from jax import lax
from jax.experimental import pallas as pl
from jax.experimental.pallas import tpu as pltpu

F32 = jnp.float32
BF16 = jnp.bfloat16

D_MODEL = 1024
CHUNK = 64
EPS = 1e-6
GLA_HEADS = 4
GLA_DK = 128
GLA_DV = 256
GLA_RANK = 16
GLA_TAU = 16.0
GLA_HEADS_PER_STEP = 4
DIFF_HEADS = 8
DIFF_DH = 64
DIFF_DV = 128
D_FF = 2816
CONV_W = 3
DOWN_GROUP = 4
LAMBDA_INIT = 0.8 - 0.6 * math.exp(-0.3 * 0)

LANES = 128
SUBLANES = 8
MASK_VALUE = -1e30
LOG2E = math.log2(math.e)
BIAS_TERMS = 3
ATTN_CHAINS = 8
LOGIT_BOUND_SLACK = 1.02
MAX_FIXED_BOUND = 50.0

COL_GQ, COL_GK, COL_GV, COL_GG = 0, 512, 1024, 2048
COL_DQ, COL_DK, COL_DV = 3072, 4096, 5120
COL_GA, COL_GB, COL_ACODE = 6144, 7168, 8192
PROJ_COLS = 8192 + 2 * LANES
PROJ_TN = PROJ_COLS // 3

VMEM_LIMIT = 56 * 1024 * 1024


def _dot(a, b):
    return jnp.dot(a, b, preferred_element_type=F32)


def _dot_nt(a, b):
    return lax.dot_general(a, b, (((1,), (1,)), ((), ())), preferred_element_type=F32)


def _dot_tn(a, b):
    return lax.dot_general(a, b, (((0,), (0,)), ((), ())), preferred_element_type=F32)


def _split_bf16(v):
    hi = v.astype(BF16)
    lo = (v - hi.astype(F32)).astype(BF16)
    return hi, lo


def _sigmoid(v):
    return 1.0 / (1.0 + jnp.exp(-v))


def _chunk_id(pos):
    return lax.shift_right_logical(pos, CHUNK.bit_length() - 1)


def _ada_kernel(c_ref, w_ref, b_ref, lq1_ref, lk1_ref, lq2_ref, lk2_ref, mod_ref, lam_ref):
    c = c_ref[...]
    a = c * _sigmoid(c)
    a_hi, a_lo = _split_bf16(a)
    w_hi, w_lo = _split_bf16(w_ref[...])
    mod_ref[...] = _dot(a_hi, w_hi) + _dot(a_lo, w_hi) + _dot(a_hi, w_lo) + b_ref[...]
    s1 = jnp.sum(lq1_ref[...] * lk1_ref[...], axis=-1, keepdims=True)
    s2 = jnp.sum(lq2_ref[...] * lk2_ref[...], axis=-1, keepdims=True)
    lam = jnp.exp(s1) - jnp.exp(s2) + LAMBDA_INIT
    lam_ref[...] = jnp.broadcast_to(lam, lam_ref.shape)


def _ada_mod(c, w_ada, b_ada, lq1, lk1, lq2, lk2):
    B, D = c.shape
    n_out = w_ada.shape[1]
    tn = D
    small = pl.BlockSpec((1, DIFF_DH), lambda j: (0, 0))
    return pl.pallas_call(
        _ada_kernel,
        grid=(n_out // tn,),
        in_specs=[
            pl.BlockSpec((B, D), lambda j: (0, 0)),
            pl.BlockSpec((D, tn), lambda j: (0, j)),
            pl.BlockSpec((1, tn), lambda j: (0, j)),
            small, small, small, small,
        ],
        out_specs=[
            pl.BlockSpec((B, tn), lambda j: (0, j)),
            pl.BlockSpec((1, LANES), lambda j: (0, 0)),
        ],
        out_shape=[
            jax.ShapeDtypeStruct((B, n_out), F32),
            jax.ShapeDtypeStruct((1, LANES), F32),
        ],
        compiler_params=pltpu.CompilerParams(dimension_semantics=("arbitrary",)),
        name="ada_mod",
    )(c, w_ada, b_ada, lq1, lk1, lq2, lk2)


def _in_proj_kernel(x_ref, mod_ref, g_ref, w_ref, o_ref, h_scr):
    @pl.when(pl.program_id(1) == 0)
    def _():
        x = x_ref[...]
        ms = jnp.mean(x * x, axis=-1, keepdims=True)
        y = x * lax.rsqrt(ms + EPS) * g_ref[...]
        h = y * (1.0 + mod_ref[1:2, :]) + mod_ref[0:1, :]
        h_scr[...] = h.astype(BF16)

    o_ref[...] = _dot(h_scr[...], w_ref[...]).astype(BF16)


def _in_proj(x2, mod3, g1, w_in_r, S):
    N, D = x2.shape
    tm = min(1024, S)
    tiles_per_seq = S // tm
    return pl.pallas_call(
        _in_proj_kernel,
        grid=(N // tm, PROJ_COLS // PROJ_TN),
        in_specs=[
            pl.BlockSpec((tm, D), lambda i, j: (i, 0)),
            pl.BlockSpec((None, 6, D), lambda i, j: (i // tiles_per_seq, 0, 0)),
            pl.BlockSpec((1, D), lambda i, j: (0, 0)),
            pl.BlockSpec((D, PROJ_TN), lambda i, j: (0, j)),
        ],
        out_specs=pl.BlockSpec((tm, PROJ_TN), lambda i, j: (i, j)),
        out_shape=jax.ShapeDtypeStruct((N, PROJ_COLS), BF16),
        scratch_shapes=[pltpu.VMEM((tm, D), BF16)],
        compiler_params=pltpu.CompilerParams(
            dimension_semantics=("parallel", "arbitrary"), vmem_limit_bytes=VMEM_LIMIT),
        name="in_proj",
    )(x2, mod3, g1, w_in_r)


def _gla_kernel(q_ref, k_ref, v_ref, g_ref, a_ref, wa_ref, ba_ref, ng_ref, o_ref, st_ref, *, tc):
    nchunk = tc // CHUNK

    @pl.when(pl.program_id(2) == 0)
    def _():
        st_ref[...] = jnp.zeros_like(st_ref)

    row = lax.broadcasted_iota(jnp.int32, (tc, tc), 0)
    col = lax.broadcasted_iota(jnp.int32, (tc, tc), 1)
    chunk_gap = _chunk_id(row) - _chunk_id(col)
    lower = (chunk_gap == 0) & (col <= row)
    upper = (chunk_gap == 0) & (col > row)
    tri = jnp.where(lower, 1.0, 0.0).astype(BF16)

    def per_chunk(rows):
        return jnp.concatenate([jnp.broadcast_to(r, (CHUNK, GLA_DK)) for r in rows], axis=0)

    def head(hh):
        dk = slice(hh * GLA_DK, (hh + 1) * GLA_DK)
        dv = slice(hh * GLA_DV, (hh + 1) * GLA_DV)
        z = _dot(a_ref[...], wa_ref[:, dk]) + ba_ref[:, dk]
        yield
        log_a = (jnp.minimum(z, 0.0) - jnp.log1p(jnp.exp(-jnp.abs(z)))) * (1.0 / GLA_TAU)
        la_hi, la_lo = _split_bf16(log_a)
        bcum = _dot(tri, la_hi) + _dot(tri, la_lo)
        yield
        b_last = [bcum[c * CHUNK + CHUNK - 1:(c + 1) * CHUNK, :] for c in range(nchunk)]
        prefix = [jnp.zeros_like(b_last[0])]
        for c in range(nchunk):
            prefix.append(prefix[-1] + b_last[c])
        one = jnp.ones_like(b_last[0])

        eb = jnp.exp(bcum)
        enb = jnp.exp(-bcum)
        qs = q_ref[:, dk].astype(F32) * (GLA_DK ** -0.5)
        k = k_ref[:, dk].astype(F32)
        v = v_ref[:, dv]
        q_fwd = qs * eb
        q_f = q_fwd.astype(BF16)
        k_dec = k * jnp.exp(per_chunk(b_last) - bcum)
        k_d = k_dec.astype(BF16)

        s_fwd = _dot_nt(q_f, (k * enb).astype(BF16))
        s_bwd = _dot_nt((qs * enb).astype(BF16), (k * eb).astype(BF16))
        yield
        scores = jnp.where(lower, s_fwd, jnp.where(upper, s_bwd, 0.0))
        for off in range(1, nchunk):
            if off == 1:
                q_x = q_f
            else:
                between = [one] * off + [jnp.exp(prefix[c] - prefix[c - off + 1]) for c in range(off, nchunk)]
                q_x = (q_fwd * per_chunk(between)).astype(BF16)
            scores = jnp.where(chunk_gap == off, _dot_nt(q_x, k_d), scores)
            yield

        st = st_ref[hh]
        q_s = (q_fwd * per_chunk([jnp.exp(p) for p in prefix[:nchunk]])).astype(BF16)
        o = _dot(scores.astype(BF16), v) + _dot_nt(q_s, st.astype(BF16))
        to_end = [jnp.exp(prefix[nchunk] - prefix[c + 1]) for c in range(nchunk)]
        st_ref[hh] = st * jnp.exp(prefix[nchunk]) + _dot_tn(v, (k_dec * per_chunk(to_end)).astype(BF16))
        yield

        ms = jnp.mean(o * o, axis=-1, keepdims=True)
        y = o * lax.rsqrt(ms + EPS) * ng_ref[...]
        g = g_ref[:, dv].astype(F32)
        o_ref[:, dv] = (y * (g * _sigmoid(g))).astype(BF16)

    heads = [head(hh) for hh in range(GLA_HEADS_PER_STEP)]
    while heads:
        heads = [h for h in heads if next(h, True) is None]


def _gla(proj, wa_pad, b_alpha, gla_norm_g, B, S):
    N = proj.shape[0]
    tc = min(256, S)
    nt = S // tc
    hps = GLA_HEADS_PER_STEP
    row = lambda b, h, i: b * nt + i
    return pl.pallas_call(
        functools.partial(_gla_kernel, tc=tc),
        grid=(B, GLA_HEADS // hps, nt),
        in_specs=[
            pl.BlockSpec((tc, hps * GLA_DK), lambda b, h, i: (row(b, h, i), COL_GQ // (hps * GLA_DK) + h)),
            pl.BlockSpec((tc, hps * GLA_DK), lambda b, h, i: (row(b, h, i), COL_GK // (hps * GLA_DK) + h)),
            pl.BlockSpec((tc, hps * GLA_DV), lambda b, h, i: (row(b, h, i), COL_GV // (hps * GLA_DV) + h)),
            pl.BlockSpec((tc, hps * GLA_DV), lambda b, h, i: (row(b, h, i), COL_GG // (hps * GLA_DV) + h)),
            pl.BlockSpec((tc, LANES), lambda b, h, i: (row(b, h, i), COL_ACODE // LANES)),
            pl.BlockSpec((LANES, hps * GLA_DK), lambda b, h, i: (0, h)),
            pl.BlockSpec((1, hps * GLA_DK), lambda b, h, i: (0, h)),
            pl.BlockSpec((1, GLA_DV), lambda b, h, i: (0, 0)),
        ],
        out_specs=pl.BlockSpec((tc, hps * GLA_DV), lambda b, h, i: (row(b, h, i), h)),
        out_shape=jax.ShapeDtypeStruct((N, GLA_HEADS * GLA_DV), BF16),
        scratch_shapes=[pltpu.VMEM((hps, GLA_DV, GLA_DK), F32)],
        compiler_params=pltpu.CompilerParams(
            dimension_semantics=("parallel", "parallel", "arbitrary"), vmem_limit_bytes=VMEM_LIMIT),
        name="gla",
    )(proj, proj, proj, proj, proj, wa_pad, b_alpha, gla_norm_g)


def _group_rms(xf, gsum, gain):
    hi, lo = _split_bf16(xf * xf)
    ss = _dot(hi, gsum) + _dot(lo, gsum)
    return xf * lax.rsqrt(ss * (1.0 / DIFF_DH) + EPS) * gain


def _diff_attn_kernel(q_ref, k_ref, v_ref, qg_ref, kg_ref, slope_ref, lam_ref, sg_ref, o_ref,
                      ka_scr, vt_scr, wt_scr, acc_scr, s_scr, p_scr, dm_scr, *, t, seq):
    i = pl.program_id(2)
    nqc = t // LANES
    width = nqc * 2 * LANES
    chains = tuple(range(ATTN_CHAINS))
    lane = lax.broadcasted_iota(jnp.int32, (1, LANES), 1)
    first_map = lane < DIFF_DH

    gr = lax.broadcasted_iota(jnp.int32, (LANES, LANES), 0) // DIFF_DH
    gc = lax.broadcasted_iota(jnp.int32, (LANES, LANES), 1) // DIFF_DH
    gsum = jnp.where(gr == gc, 1.0, 0.0).astype(BF16)
    slope2 = slope_ref[...] * LOG2E
    slope2_w = jnp.tile(slope2, (1, width // LANES))

    @pl.when(i == 0)
    def _():
        jl = lax.broadcasted_iota(jnp.int32, (t, LANES), 0).astype(F32)
        lane_t = lax.broadcasted_iota(jnp.int32, (t, LANES), 1)
        rem = slope2 * jl
        aux = jnp.zeros((t, LANES), F32)
        for term in range(BIAS_TERMS):
            part = rem.astype(BF16).astype(F32)
            aux = jnp.where(lane_t == term, part, aux)
            rem = rem - part
        aux = aux.astype(BF16)

        jd = lax.broadcasted_iota(jnp.int32, (t, width), 0)
        col = lax.broadcasted_iota(jnp.int32, (t, width), 1)
        il = (col & (LANES - 1)) + lax.shift_right_logical(col, 8) * LANES
        ahead = jnp.minimum(il - jd, 0).astype(F32)
        dm_scr[...] = jnp.where(_chunk_id(jd) <= _chunk_id(il), (2.0 * slope2_w) * ahead, MASK_VALUE)

        def body(r, carry):
            rows = pl.ds(pl.multiple_of(r * t, t), t)
            kn = _group_rms(k_ref[rows, :].astype(F32), gsum, kg_ref[...])
            ka_scr[rows, 0:LANES] = kn.astype(BF16)
            ka_scr[rows, LANES:2 * LANES] = aux
            vt_scr[r] = v_ref[rows, :].astype(F32).T.astype(BF16)
            return carry
        lax.fori_loop(0, seq // t, body, 0, unroll=2)

    qn = _group_rms(q_ref[...].astype(F32), gsum, qg_ref[...]) * (DIFF_DH ** -0.5 * LOG2E)
    ones = jnp.broadcast_to(jnp.where(lane < BIAS_TERMS, 1.0, 0.0), (LANES, LANES))
    for ch in chains:
        for qc in range(nqc):
            r0 = ch * t + qc * LANES
            qq = qn[r0:r0 + LANES, :]
            w = jnp.concatenate(
                [jnp.concatenate([jnp.where(first_map, qq, 0.0), ones], axis=1),
                 jnp.concatenate([jnp.where(first_map, 0.0, qq), ones], axis=1)], axis=0)
            wt_scr[ch, :, qc * 2 * LANES:(qc + 1) * 2 * LANES] = w.T.astype(BF16)
        p_scr[ch, 1] = jnp.zeros(p_scr.shape[2:], BF16)
    acc_scr[...] = jnp.zeros_like(acc_scr)

    def scores(ch, j, slot):
        rows = pl.ds(pl.multiple_of(j * t, t), t)
        st = _dot(ka_scr[rows, :], wt_scr[ch])
        s_scr[ch, slot] = st
        return jnp.max(st, axis=0, keepdims=True)

    def accumulate(ch, j, slot, alpha):
        acc_scr[ch] = alpha * acc_scr[ch] + _dot(vt_scr[jnp.maximum(j, 0)], p_scr[ch, slot])

    def softmax(ch, j, st, m_cur, m_prev, l_prev):
        shift = slope2_w * ((j - (ATTN_CHAINS * i + ch)) * t).astype(F32)
        m_new = jnp.maximum(m_prev, m_cur + shift)
        p = jnp.exp2(st - (m_new - shift))
        alpha = jnp.exp2(m_prev - m_new)
        l_new = alpha * l_prev + jnp.sum(p, axis=0, keepdims=True)
        return p.astype(BF16), alpha, m_new, l_new

    def step(ch, j, slot, carry):
        m, l, m_cur, alpha_prev = carry
        m_next = scores(ch, j + 1, 1 - slot)
        accumulate(ch, j - 1, 1 - slot, alpha_prev)
        p, alpha, m, l = softmax(ch, j, s_scr[ch, slot], m_cur, m, l)
        p_scr[ch, slot] = p
        return m, l, m_next, alpha

    def finish(ch, j, slot, carry):
        m, l, _, alpha_prev = carry
        accumulate(ch, j - 1, 1 - slot, alpha_prev)
        st = s_scr[ch, slot] + dm_scr[...]
        p, alpha, m, l = softmax(ch, j, st, jnp.max(st, axis=0, keepdims=True), m, l)
        on = (alpha * acc_scr[ch] + _dot(vt_scr[j], p)) * (1.0 / l)
        for qc in range(nqc):
            c0 = qc * 2 * LANES
            ot = on[:, c0:c0 + LANES] - lam_ref[...] * on[:, c0 + LANES:c0 + 2 * LANES]
            o = ot.T
            msq = jnp.mean(o * o, axis=-1, keepdims=True)
            y = o * lax.rsqrt(msq + EPS) * sg_ref[...] * (1.0 - LAMBDA_INIT)
            r0 = ch * t + qc * LANES
            o_ref[r0:r0 + LANES, :] = y.astype(BF16)

    def pair(pp, carries):
        carries = list(carries)
        for j, slot in ((2 * pp, 0), (2 * pp + 1, 1)):
            for ch in chains:
                carries[ch] = step(ch, j, slot, carries[ch])
        return tuple(carries)

    init = tuple((jnp.full((1, width), MASK_VALUE, F32), jnp.zeros((1, width), F32),
                  scores(ch, 0, 0), jnp.ones((1, width), F32)) for ch in chains)
    carries = list(lax.fori_loop(0, (ATTN_CHAINS // 2) * i, pair, init))

    for k in chains:
        j = ATTN_CHAINS * i + k
        for ch in chains[k + 1:]:
            carries[ch] = step(ch, j, k % 2, carries[ch])
        finish(k, j, k % 2, carries[k])


def _diff_attn_fixed_kernel(q_ref, k_ref, v_ref, qg_ref, kg_ref, slope_ref, lam_ref, sg_ref, bound_ref,
                            o_ref, ka_scr, vt_scr, wt_scr, acc_scr, s_scr, p_scr, dm_scr, *, t, seq):
    i = pl.program_id(2)
    nqc = t // LANES
    width = nqc * 2 * LANES
    chains = tuple(range(ATTN_CHAINS))

    gr = lax.broadcasted_iota(jnp.int32, (LANES, LANES), 0) // DIFF_DH
    gc = lax.broadcasted_iota(jnp.int32, (LANES, LANES), 1) // DIFF_DH
    gsum = jnp.where(gr == gc, 1.0, 0.0).astype(BF16)
    slope2 = slope_ref[...] * LOG2E
    slope2_w = jnp.tile(slope2, (1, width // LANES))

    def bf16_terms(value):
        terms, rem = [], value
        for _ in range(BIAS_TERMS):
            terms.append(rem.astype(BF16).astype(F32))
            rem = rem - terms[-1]
        return terms

    @pl.when(i == 0)
    def _():
        jd = lax.broadcasted_iota(jnp.int32, (t, width), 0)
        col = lax.broadcasted_iota(jnp.int32, (t, width), 1)
        il = (col & (LANES - 1)) + lax.shift_right_logical(col, 8) * LANES
        ahead = jnp.minimum(il - jd, 0).astype(F32)
        dm_scr[...] = jnp.where(_chunk_id(jd) <= _chunk_id(il), (2.0 * slope2_w) * ahead, MASK_VALUE)

        lane_t = lax.broadcasted_iota(jnp.int32, (t, LANES), 1)
        row_t = lax.broadcasted_iota(jnp.int32, (t, LANES), 0).astype(F32)
        in_tile = jnp.where((lane_t >= 2 * BIAS_TERMS) & (lane_t < 3 * BIAS_TERMS), 1.0, 0.0)
        for n, term in enumerate(bf16_terms(slope2 * row_t)):
            in_tile = jnp.where(lane_t == n, term, in_tile)
        lane_1 = lax.broadcasted_iota(jnp.int32, (1, LANES), 1)

        def body(r, carry):
            rows = pl.ds(pl.multiple_of(r * t, t), t)
            kn = _group_rms(k_ref[rows, :].astype(F32), gsum, kg_ref[...])
            ka_scr[rows, 0:LANES] = kn.astype(BF16)
            start = jnp.zeros((1, LANES), F32)
            for n, term in enumerate(bf16_terms(slope2 * jnp.asarray(r * t).astype(F32))):
                start = jnp.where(lane_1 == BIAS_TERMS + n, term, start)
            tile_lanes = (lane_t >= BIAS_TERMS) & (lane_t < 2 * BIAS_TERMS)
            ka_scr[rows, LANES:2 * LANES] = jnp.where(tile_lanes, start, in_tile).astype(BF16)
            vt_scr[r] = v_ref[rows, :].astype(F32).T.astype(BF16)
            return carry
        lax.fori_loop(0, seq // t, body, 0, unroll=2)

    qn = _group_rms(q_ref[...].astype(F32), gsum, qg_ref[...]) * (DIFF_DH ** -0.5 * LOG2E)
    row_q = lax.broadcasted_iota(jnp.int32, (LANES, LANES), 0)
    lane_q = lax.broadcasted_iota(jnp.int32, (1, LANES), 1)
    for ch in chains:
        for qc in range(nqc):
            r0 = ch * t + qc * LANES
            qt = qn[r0:r0 + LANES, :].T
            q_pos = (lane_q + (i * (ATTN_CHAINS * t) + r0)).astype(F32)
            side = jnp.where(row_q < 2 * BIAS_TERMS, 1.0, 0.0)
            for n, term in enumerate(bf16_terms(-(slope2 * q_pos + bound_ref[...]))):
                side = jnp.where(row_q == 2 * BIAS_TERMS + n, term, side)
            side = side.astype(BF16)
            c0 = qc * 2 * LANES
            wt_scr[ch, 0:LANES, c0:c0 + LANES] = jnp.where(row_q < DIFF_DH, qt, 0.0).astype(BF16)
            wt_scr[ch, 0:LANES, c0 + LANES:c0 + 2 * LANES] = jnp.where(row_q < DIFF_DH, 0.0, qt).astype(BF16)
            wt_scr[ch, LANES:2 * LANES, c0:c0 + LANES] = side
            wt_scr[ch, LANES:2 * LANES, c0 + LANES:c0 + 2 * LANES] = side
        p_scr[ch, 1] = jnp.zeros(p_scr.shape[2:], BF16)
    acc_scr[...] = jnp.zeros_like(acc_scr)

    def scores(ch, j, slot):
        rows = pl.ds(pl.multiple_of(j * t, t), t)
        s_scr[ch, slot] = _dot(ka_scr[rows, :], wt_scr[ch])

    def accumulate(ch, j, slot):
        acc_scr[ch] += _dot(vt_scr[jnp.maximum(j, 0)], p_scr[ch, slot])

    def expo(ch, slot, st, l_part):
        p = jnp.exp2(st)
        p_scr[ch, slot] = p.astype(BF16)
        return l_part + jnp.sum(p.reshape(t // SUBLANES, SUBLANES, width), axis=0)

    def step(ch, j, slot, l_part):
        scores(ch, j + 1, 1 - slot)
        accumulate(ch, j - 1, 1 - slot)
        return expo(ch, slot, s_scr[ch, slot], l_part)

    def finish(ch, j, slot, l_part):
        accumulate(ch, j - 1, 1 - slot)
        l_part = expo(ch, slot, s_scr[ch, slot] + dm_scr[...], l_part)
        accumulate(ch, j, slot)
        on = acc_scr[ch] * (1.0 / jnp.sum(l_part, axis=0, keepdims=True))
        for qc in range(nqc):
            c0 = qc * 2 * LANES
            ot = on[:, c0:c0 + LANES] - lam_ref[...] * on[:, c0 + LANES:c0 + 2 * LANES]
            o = ot.T
            msq = jnp.mean(o * o, axis=-1, keepdims=True)
            y = o * lax.rsqrt(msq + EPS) * sg_ref[...] * (1.0 - LAMBDA_INIT)
            r0 = ch * t + qc * LANES
            o_ref[r0:r0 + LANES, :] = y.astype(BF16)

    def pair(pp, parts):
        parts = list(parts)
        for j, slot in ((2 * pp, 0), (2 * pp + 1, 1)):
            for ch in chains:
                parts[ch] = step(ch, j, slot, parts[ch])
        return tuple(parts)

    for ch in chains:
        scores(ch, 0, 0)
    parts = list(lax.fori_loop(0, (ATTN_CHAINS // 2) * i, pair,
                               tuple(jnp.zeros((SUBLANES, width), F32) for _ in chains)))

    for k in chains:
        j = ATTN_CHAINS * i + k
        for ch in chains[k + 1:]:
            parts[ch] = step(ch, j, k % 2, parts[ch])
        finish(k, j, k % 2, parts[k])


def _diff_attn(proj, qg2, kg2, slopes, lam, subln_g, B, S, bound=None):
    N = proj.shape[0]
    t = min(256, S // ATTN_CHAINS)
    tq = ATTN_CHAINS * t
    nq = S // tq
    vec = pl.BlockSpec((1, LANES), lambda b, h, i: (0, 0))
    body = _diff_attn_kernel if bound is None else _diff_attn_fixed_kernel
    extra = () if bound is None else (bound,)
    return pl.pallas_call(
        functools.partial(body, t=t, seq=S),
        grid=(B, DIFF_HEADS, nq),
        in_specs=[
            pl.BlockSpec((tq, LANES), lambda b, h, i: (b * nq + i, COL_DQ // LANES + h)),
            pl.BlockSpec((S, LANES), lambda b, h, i: (b, COL_DK // LANES + h)),
            pl.BlockSpec((S, LANES), lambda b, h, i: (b, COL_DV // LANES + h)),
            vec, vec,
            pl.BlockSpec((None, 1, LANES), lambda b, h, i: (h, 0, 0)),
            vec, vec,
        ] + [vec] * len(extra),
        out_specs=pl.BlockSpec((tq, DIFF_DV), lambda b, h, i: (b * nq + i, h)),
        out_shape=jax.ShapeDtypeStruct((N, DIFF_HEADS * DIFF_DV), BF16),
        scratch_shapes=[
            pltpu.VMEM((S, 2 * LANES), BF16),
            pltpu.VMEM((S // t, DIFF_DV, t), BF16),
            pltpu.VMEM((ATTN_CHAINS, 2 * LANES, 2 * t), BF16),
            pltpu.VMEM((ATTN_CHAINS, DIFF_DV, 2 * t), F32),
            pltpu.VMEM((ATTN_CHAINS, 2, t, 2 * t), F32),
            pltpu.VMEM((ATTN_CHAINS, 2, t, 2 * t), BF16),
            pltpu.VMEM((t, 2 * t), F32),
        ],
        compiler_params=pltpu.CompilerParams(
            dimension_semantics=("parallel", "parallel", "arbitrary"), vmem_limit_bytes=VMEM_LIMIT),
        name="diff_attn" if bound is None else "diff_attn_fixed",
    )(proj, proj, proj, qg2, kg2, slopes, lam, subln_g, *extra)


def _merge_kernel(oa_ref, ob_ref, ga_ref, gb_ref, x_ref, mod_ref, wa_ref, wb_ref, wo_ref, g2_ref,
                  x1_ref, h2_ref):
    ya = _dot(oa_ref[...], wa_ref[...])
    yb = _dot(ob_ref[...], wb_ref[...])
    merged = _sigmoid(ga_ref[...].astype(F32)) * ya + _sigmoid(gb_ref[...].astype(F32)) * yb
    x1 = x_ref[...] + mod_ref[2:3, :] * _dot(merged.astype(BF16), wo_ref[...])
    x1_ref[...] = x1
    ms = jnp.mean(x1 * x1, axis=-1, keepdims=True)
    y = x1 * lax.rsqrt(ms + EPS) * g2_ref[...]
    h2_ref[...] = (y * (1.0 + mod_ref[4:5, :]) + mod_ref[3:4, :]).astype(BF16)


def _merge(o_a, o_b, proj, x2, mod3, w_gla_o, w_diff_o, w_out, g2, S):
    N, D = x2.shape
    tm = min(512, S)
    tiles_per_seq = S // tm
    tok = lambda col: pl.BlockSpec((tm, D), lambda i: (i, col))
    wspec = pl.BlockSpec((D, D), lambda i: (0, 0))
    return pl.pallas_call(
        _merge_kernel,
        grid=(N // tm,),
        in_specs=[
            tok(0), tok(0), tok(COL_GA // D), tok(COL_GB // D), tok(0),
            pl.BlockSpec((None, 6, D), lambda i: (i // tiles_per_seq, 0, 0)),
            wspec, wspec, wspec,
            pl.BlockSpec((1, D), lambda i: (0, 0)),
        ],
        out_specs=[tok(0), tok(0)],
        out_shape=[jax.ShapeDtypeStruct((N, D), F32), jax.ShapeDtypeStruct((N, D), BF16)],
        compiler_params=pltpu.CompilerParams(
            dimension_semantics=("parallel",), vmem_limit_bytes=VMEM_LIMIT),
        name="merge",
    )(o_a, o_b, proj, proj, x2, mod3, w_gla_o, w_diff_o, w_out, g2)


def _conv_ffn_kernel(h2_ref, x1_ref, mod_ref, wup_ref, cw_ref, cb_ref, wdn_ref, o_ref,
                     carry_ref, acc_ref, ubuf_ref, act_ref, *, tm, tf, tiles_per_seq):
    @pl.when(pl.program_id(0) % tiles_per_seq == 0)
    def _():
        carry_ref[...] = jnp.zeros_like(carry_ref)

    h2 = h2_ref[...]

    def up(f):
        return (_dot(h2, wup_ref[:, f * tf:(f + 1) * tf]),
                _dot(h2, wup_ref[:, D_FF + f * tf:D_FF + (f + 1) * tf]))

    def conv(u, col0, slot, half):
        cols = slice(half * tf, (half + 1) * tf)
        ubuf_ref[slot, 0:SUBLANES, cols] = carry_ref[:, col0:col0 + tf]
        ubuf_ref[slot, SUBLANES:, cols] = u
        carry_ref[:, col0:col0 + tf] = u[tm - SUBLANES:, :]
        r1 = ubuf_ref[slot, SUBLANES - 1:SUBLANES - 1 + tm, cols]
        r2 = ubuf_ref[slot, SUBLANES - 2:SUBLANES - 2 + tm, cols]
        cw = cw_ref[:, col0:col0 + tf]
        return cw[0:1, :] * r2 + cw[1:2, :] * r1 + cw[2:3, :] * u + cb_ref[:, col0:col0 + tf]

    nf = D_FF // tf
    u_next = up(0)
    for f in range(nf):
        ua, ub = u_next
        if f + 1 < nf:
            u_next = up(f + 1)
        a = conv(ua, f * tf, f % 2, 0)
        b = conv(ub, D_FF + f * tf, f % 2, 1)
        act_ref[:, (f % DOWN_GROUP) * tf:(f % DOWN_GROUP + 1) * tf] = (a * _sigmoid(a) * b).astype(BF16)
        if (f + 1) % DOWN_GROUP == 0 or f + 1 == nf:
            f0 = f - f % DOWN_GROUP
            contrib = _dot(act_ref[:, :(f + 1 - f0) * tf], wdn_ref[f0 * tf:(f + 1) * tf, :])
            if f0 == 0:
                acc_ref[...] = contrib
            else:
                acc_ref[...] += contrib

    o_ref[...] = x1_ref[...] + mod_ref[5:6, :] * acc_ref[...]


def _conv_ffn(h2, x1, mod3, w_up, conv_w, conv_b, w_down, S):
    N, D = x1.shape
    tm = min(256, S)
    tf = 256
    tiles_per_seq = S // tm
    tok = pl.BlockSpec((tm, D), lambda i: (i, 0))
    full = lambda a: pl.BlockSpec(a.shape, lambda i: (0, 0))
    return pl.pallas_call(
        functools.partial(_conv_ffn_kernel, tm=tm, tf=tf, tiles_per_seq=tiles_per_seq),
        grid=(N // tm,),
        in_specs=[
            tok, tok,
            pl.BlockSpec((None, 6, D), lambda i: (i // tiles_per_seq, 0, 0)),
            full(w_up), full(conv_w), full(conv_b), full(w_down),
        ],
        out_specs=tok,
        out_shape=jax.ShapeDtypeStruct((N, D), F32),
        scratch_shapes=[
            pltpu.VMEM((SUBLANES, 2 * D_FF), F32),
            pltpu.VMEM((tm, D), F32),
            pltpu.VMEM((2, SUBLANES + tm, 2 * tf), F32),
            pltpu.VMEM((tm, DOWN_GROUP * tf), BF16),
        ],
        compiler_params=pltpu.CompilerParams(
            dimension_semantics=("arbitrary",), vmem_limit_bytes=VMEM_LIMIT),
        name="conv_ffn",
    )(h2, x1, mod3, w_up, conv_w, conv_b, w_down)


def kernel(x, c, w_ada, b_ada, norm1_g, w_in, w_alpha_up, b_alpha, gla_norm_g, q_norm_g, k_norm_g,
           lam_q1, lam_k1, lam_q2, lam_k2, diff_norm_g, w_gla_o, w_diff_o, w_out, norm2_g, w_up,
           conv_w, conv_b, w_down):
    B, S, D = x.shape
    N = B * S
    assert D == D_MODEL and S % CHUNK == 0 and w_ada.shape[0] == 1

    w0 = w_in[0]
    a0 = COL_GG + GLA_HEADS * GLA_DV
    w_in_r = jnp.concatenate(
        [w0[:, :a0], w0[:, a0 + GLA_RANK:],
         jnp.pad(w0[:, a0:a0 + GLA_RANK], ((0, 0), (0, PROJ_COLS - COL_ACODE - GLA_RANK)))], axis=1).astype(BF16)
    wa_pad = jnp.pad(w_alpha_up[0], ((0, LANES - GLA_RANK), (0, 0))).astype(BF16)
    slopes = 2.0 ** (-8.0 * (jnp.arange(DIFF_HEADS, dtype=F32) + 1.0) / DIFF_HEADS)
    slopes = jnp.broadcast_to(slopes[:, None, None], (DIFF_HEADS, 1, LANES))
    qg2 = jnp.tile(q_norm_g, (1, 2))
    kg2 = jnp.tile(k_norm_g, (1, 2))

    mod, lam = _ada_mod(c, w_ada[0], b_ada, lam_q1, lam_k1, lam_q2, lam_k2)
    mod3 = mod.reshape(B, 6, D)
    x2 = x.reshape(N, D)

    proj = _in_proj(x2, mod3, norm1_g, w_in_r, S)
    o_a = _gla(proj, wa_pad, b_alpha, gla_norm_g, B, S)
    bound = (LOGIT_BOUND_SLACK * DIFF_DH ** 0.5 * LOG2E) * jnp.max(jnp.abs(q_norm_g)) * jnp.max(jnp.abs(k_norm_g))
    o_b = lax.cond(
        bound <= MAX_FIXED_BOUND,
        lambda: _diff_attn(proj, qg2, kg2, slopes, lam, diff_norm_g, B, S,
                           bound=jnp.broadcast_to(bound, (1, LANES)).astype(F32)),
        lambda: _diff_attn(proj, qg2, kg2, slopes, lam, diff_norm_g, B, S))
    x1, h2 = _merge(o_a, o_b, proj, x2, mod3, w_gla_o[0].astype(BF16), w_diff_o[0].astype(BF16),
                    w_out[0].astype(BF16), norm2_g, S)
    out = _conv_ffn(h2, x1, mod3, w_up[0].astype(BF16), conv_w[0], conv_b, w_down[0].astype(BF16), S)
    return out.reshape(B, S, D)
```

```python
import functools
import math

import jax
import jax.numpy as jnp
from jax import lax
from jax.experimental import pallas as pl
from jax.experimental.pallas import tpu as pltpu

F32 = jnp.float32
BF16 = jnp.bfloat16

D_MODEL = 1024
CHUNK = 64
EPS = 1e-6
GLA_HEADS = 4
GLA_DK = 128
GLA_DV = 256
GLA_RANK = 16
GLA_TAU = 16.0
GLA_HEADS_PER_STEP = 4
DIFF_HEADS = 8
DIFF_DH = 64
DIFF_DV = 128
D_FF = 2816
CONV_W = 3
DOWN_GROUP = 4
LAMBDA_INIT = 0.8 - 0.6 * math.exp(-0.3 * 0)

LANES = 128
SUBLANES = 8
MASK_VALUE = -1e30
LOG2E = math.log2(math.e)
BIAS_TERMS = 3
ATTN_CHAINS = 8
LOGIT_BOUND_SLACK = 1.02
MAX_FIXED_BOUND = 50.0

COL_GQ, COL_GK, COL_GV, COL_GG = 0, 512, 1024, 2048
COL_DQ, COL_DK, COL_DV = 3072, 4096, 5120
COL_GA, COL_GB, COL_ACODE = 6144, 7168, 8192
PROJ_COLS = 8192 + 2 * LANES
PROJ_TN = PROJ_COLS // 3

VMEM_LIMIT = 56 * 1024 * 1024


def _dot(a, b):
    return jnp.dot(a, b, preferred_element_type=F32)


def _dot_nt(a, b):
    return lax.dot_general(a, b, (((1,), (1,)), ((), ())), preferred_element_type=F32)


def _dot_tn(a, b):
    return lax.dot_general(a, b, (((0,), (0,)), ((), ())), preferred_element_type=F32)


def _split_bf16(v):
    hi = v.astype(BF16)
    lo = (v - hi.astype(F32)).astype(BF16)
    return hi, lo


def _sigmoid(v):
    return 1.0 / (1.0 + jnp.exp(-v))


def _chunk_id(pos):
    return lax.shift_right_logical(pos, CHUNK.bit_length() - 1)


def _ada_kernel(c_ref, w_ref, b_ref, lq1_ref, lk1_ref, lq2_ref, lk2_ref, mod_ref, lam_ref):
    c = c_ref[...]
    a = c * _sigmoid(c)
    a_hi, a_lo = _split_bf16(a)
    w_hi, w_lo = _split_bf16(w_ref[...])
    mod_ref[...] = _dot(a_hi, w_hi) + _dot(a_lo, w_hi) + _dot(a_hi, w_lo) + b_ref[...]
    s1 = jnp.sum(lq1_ref[...] * lk1_ref[...], axis=-1, keepdims=True)
    s2 = jnp.sum(lq2_ref[...] * lk2_ref[...], axis=-1, keepdims=True)
    lam = jnp.exp(s1) - jnp.exp(s2) + LAMBDA_INIT
    lam_ref[...] = jnp.broadcast_to(lam, lam_ref.shape)


def _ada_mod(c, w_ada, b_ada, lq1, lk1, lq2, lk2):
    B, D = c.shape
    n_out = w_ada.shape[1]
    tn = D
    small = pl.BlockSpec((1, DIFF_DH), lambda j: (0, 0))
    return pl.pallas_call(
        _ada_kernel,
        grid=(n_out // tn,),
        in_specs=[
            pl.BlockSpec((B, D), lambda j: (0, 0)),
            pl.BlockSpec((D, tn), lambda j: (0, j)),
            pl.BlockSpec((1, tn), lambda j: (0, j)),
            small, small, small, small,
        ],
        out_specs=[
            pl.BlockSpec((B, tn), lambda j: (0, j)),
            pl.BlockSpec((1, LANES), lambda j: (0, 0)),
        ],
        out_shape=[
            jax.ShapeDtypeStruct((B, n_out), F32),
            jax.ShapeDtypeStruct((1, LANES), F32),
        ],
        compiler_params=pltpu.CompilerParams(dimension_semantics=("arbitrary",)),
        name="ada_mod",
    )(c, w_ada, b_ada, lq1, lk1, lq2, lk2)


def _in_proj_kernel(x_ref, mod_ref, g_ref, w_ref, o_ref, h_scr):
    @pl.when(pl.program_id(1) == 0)
    def _():
        x = x_ref[...]
        ms = jnp.mean(x * x, axis=-1, keepdims=True)
        y = x * lax.rsqrt(ms + EPS) * g_ref[...]
        h = y * (1.0 + mod_ref[1:2, :]) + mod_ref[0:1, :]
        h_scr[...] = h.astype(BF16)

    o_ref[...] = _dot(h_scr[...], w_ref[...]).astype(BF16)


def _in_proj(x2, mod3, g1, w_in_r, S):
    N, D = x2.shape
    tm = min(1024, S)
    tiles_per_seq = S // tm
    return pl.pallas_call(
        _in_proj_kernel,
        grid=(N // tm, PROJ_COLS // PROJ_TN),
        in_specs=[
            pl.BlockSpec((tm, D), lambda i, j: (i, 0)),
            pl.BlockSpec((None, 6, D), lambda i, j: (i // tiles_per_seq, 0, 0)),
            pl.BlockSpec((1, D), lambda i, j: (0, 0)),
            pl.BlockSpec((D, PROJ_TN), lambda i, j: (0, j)),
        ],
        out_specs=pl.BlockSpec((tm, PROJ_TN), lambda i, j: (i, j)),
        out_shape=jax.ShapeDtypeStruct((N, PROJ_COLS), BF16),
        scratch_shapes=[pltpu.VMEM((tm, D), BF16)],
        compiler_params=pltpu.CompilerParams(
            dimension_semantics=("parallel", "arbitrary"), vmem_limit_bytes=VMEM_LIMIT),
        name="in_proj",
    )(x2, mod3, g1, w_in_r)


def _gla_kernel(q_ref, k_ref, v_ref, g_ref, a_ref, wa_ref, ba_ref, ng_ref, o_ref, st_ref, *, tc):
    nchunk = tc // CHUNK

    @pl.when(pl.program_id(2) == 0)
    def _():
        st_ref[...] = jnp.zeros_like(st_ref)

    row = lax.broadcasted_iota(jnp.int32, (tc, tc), 0)
    col = lax.broadcasted_iota(jnp.int32, (tc, tc), 1)
    chunk_gap = _chunk_id(row) - _chunk_id(col)
    lower = (chunk_gap == 0) & (col <= row)
    upper = (chunk_gap == 0) & (col > row)
    tri = jnp.where(lower, 1.0, 0.0).astype(BF16)

    def per_chunk(rows):
        return jnp.concatenate([jnp.broadcast_to(r, (CHUNK, GLA_DK)) for r in rows], axis=0)

    def head(hh):
        dk = slice(hh * GLA_DK, (hh + 1) * GLA_DK)
        dv = slice(hh * GLA_DV, (hh + 1) * GLA_DV)
        z = _dot(a_ref[...], wa_ref[:, dk]) + ba_ref[:, dk]
        yield
        log_a = (jnp.minimum(z, 0.0) - jnp.log1p(jnp.exp(-jnp.abs(z)))) * (1.0 / GLA_TAU)
        la_hi, la_lo = _split_bf16(log_a)
        bcum = _dot(tri, la_hi) + _dot(tri, la_lo)
        yield
        b_last = [bcum[c * CHUNK + CHUNK - 1:(c + 1) * CHUNK, :] for c in range(nchunk)]
        prefix = [jnp.zeros_like(b_last[0])]
        for c in range(nchunk):
            prefix.append(prefix[-1] + b_last[c])
        one = jnp.ones_like(b_last[0])

        eb = jnp.exp(bcum)
        enb = jnp.exp(-bcum)
        qs = q_ref[:, dk].astype(F32) * (GLA_DK ** -0.5)
        k = k_ref[:, dk].astype(F32)
        v = v_ref[:, dv]
        q_fwd = qs * eb
        q_f = q_fwd.astype(BF16)
        k_dec = k * jnp.exp(per_chunk(b_last) - bcum)
        k_d = k_dec.astype(BF16)

        s_fwd = _dot_nt(q_f, (k * enb).astype(BF16))
        s_bwd = _dot_nt((qs * enb).astype(BF16), (k * eb).astype(BF16))
        yield
        scores = jnp.where(lower, s_fwd, jnp.where(upper, s_bwd, 0.0))
        for off in range(1, nchunk):
            if off == 1:
                q_x = q_f
            else:
                between = [one] * off + [jnp.exp(prefix[c] - prefix[c - off + 1]) for c in range(off, nchunk)]
                q_x = (q_fwd * per_chunk(between)).astype(BF16)
            scores = jnp.where(chunk_gap == off, _dot_nt(q_x, k_d), scores)
            yield

        st = st_ref[hh]
        q_s = (q_fwd * per_chunk([jnp.exp(p) for p in prefix[:nchunk]])).astype(BF16)
        o = _dot(scores.astype(BF16), v) + _dot_nt(q_s, st.astype(BF16))
        to_end = [jnp.exp(prefix[nchunk] - prefix[c + 1]) for c in range(nchunk)]
        st_ref[hh] = st * jnp.exp(prefix[nchunk]) + _dot_tn(v, (k_dec * per_chunk(to_end)).astype(BF16))
        yield

        ms = jnp.mean(o * o, axis=-1, keepdims=True)
        y = o * lax.rsqrt(ms + EPS) * ng_ref[...]
        g = g_ref[:, dv].astype(F32)
        o_ref[:, dv] = (y * (g * _sigmoid(g))).astype(BF16)

    heads = [head(hh) for hh in range(GLA_HEADS_PER_STEP)]
    while heads:
        heads = [h for h in heads if next(h, True) is None]


def _gla(proj, wa_pad, b_alpha, gla_norm_g, B, S):
    N = proj.shape[0]
    tc = min(256, S)
    nt = S // tc
    hps = GLA_HEADS_PER_STEP
    row = lambda b, h, i: b * nt + i
    return pl.pallas_call(
        functools.partial(_gla_kernel, tc=tc),
        grid=(B, GLA_HEADS // hps, nt),
        in_specs=[
            pl.BlockSpec((tc, hps * GLA_DK), lambda b, h, i: (row(b, h, i), COL_GQ // (hps * GLA_DK) + h)),
            pl.BlockSpec((tc, hps * GLA_DK), lambda b, h, i: (row(b, h, i), COL_GK // (hps * GLA_DK) + h)),
            pl.BlockSpec((tc, hps * GLA_DV), lambda b, h, i: (row(b, h, i), COL_GV // (hps * GLA_DV) + h)),
            pl.BlockSpec((tc, hps * GLA_DV), lambda b, h, i: (row(b, h, i), COL_GG // (hps * GLA_DV) + h)),
            pl.BlockSpec((tc, LANES), lambda b, h, i: (row(b, h, i), COL_ACODE // LANES)),
            pl.BlockSpec((LANES, hps * GLA_DK), lambda b, h, i: (0, h)),
            pl.BlockSpec((1, hps * GLA_DK), lambda b, h, i: (0, h)),
            pl.BlockSpec((1, GLA_DV), lambda b, h, i: (0, 0)),
        ],
        out_specs=pl.BlockSpec((tc, hps * GLA_DV), lambda b, h, i: (row(b, h, i), h)),
        out_shape=jax.ShapeDtypeStruct((N, GLA_HEADS * GLA_DV), BF16),
        scratch_shapes=[pltpu.VMEM((hps, GLA_DV, GLA_DK), F32)],
        compiler_params=pltpu.CompilerParams(
            dimension_semantics=("parallel", "parallel", "arbitrary"), vmem_limit_bytes=VMEM_LIMIT),
        name="gla",
    )(proj, proj, proj, proj, proj, wa_pad, b_alpha, gla_norm_g)


def _group_rms(xf, gsum, gain):
    hi, lo = _split_bf16(xf * xf)
    ss = _dot(hi, gsum) + _dot(lo, gsum)
    return xf * lax.rsqrt(ss * (1.0 / DIFF_DH) + EPS) * gain


def _diff_attn_kernel(q_ref, k_ref, v_ref, qg_ref, kg_ref, slope_ref, lam_ref, sg_ref, o_ref,
                      ka_scr, vt_scr, wt_scr, acc_scr, s_scr, p_scr, dm_scr, *, t, seq):
    i = pl.program_id(2)
    nqc = t // LANES
    width = nqc * 2 * LANES
    chains = tuple(range(ATTN_CHAINS))
    lane = lax.broadcasted_iota(jnp.int32, (1, LANES), 1)
    first_map = lane < DIFF_DH

    gr = lax.broadcasted_iota(jnp.int32, (LANES, LANES), 0) // DIFF_DH
    gc = lax.broadcasted_iota(jnp.int32, (LANES, LANES), 1) // DIFF_DH
    gsum = jnp.where(gr == gc, 1.0, 0.0).astype(BF16)
    slope2 = slope_ref[...] * LOG2E
    slope2_w = jnp.tile(slope2, (1, width // LANES))

    @pl.when(i == 0)
    def _():
        jl = lax.broadcasted_iota(jnp.int32, (t, LANES), 0).astype(F32)
        lane_t = lax.broadcasted_iota(jnp.int32, (t, LANES), 1)
        rem = slope2 * jl
        aux = jnp.zeros((t, LANES), F32)
        for term in range(BIAS_TERMS):
            part = rem.astype(BF16).astype(F32)
            aux = jnp.where(lane_t == term, part, aux)
            rem = rem - part
        aux = aux.astype(BF16)

        jd = lax.broadcasted_iota(jnp.int32, (t, width), 0)
        col = lax.broadcasted_iota(jnp.int32, (t, width), 1)
        il = (col & (LANES - 1)) + lax.shift_right_logical(col, 8) * LANES
        ahead = jnp.minimum(il - jd, 0).astype(F32)
        dm_scr[...] = jnp.where(_chunk_id(jd) <= _chunk_id(il), (2.0 * slope2_w) * ahead, MASK_VALUE)

        def body(r, carry):
            rows = pl.ds(pl.multiple_of(r * t, t), t)
            kn = _group_rms(k_ref[rows, :].astype(F32), gsum, kg_ref[...])
            ka_scr[rows, 0:LANES] = kn.astype(BF16)
            ka_scr[rows, LANES:2 * LANES] = aux
            vt_scr[r] = v_ref[rows, :].astype(F32).T.astype(BF16)
            return carry
        lax.fori_loop(0, seq // t, body, 0, unroll=2)

    qn = _group_rms(q_ref[...].astype(F32), gsum, qg_ref[...]) * (DIFF_DH ** -0.5 * LOG2E)
    ones = jnp.broadcast_to(jnp.where(lane < BIAS_TERMS, 1.0, 0.0), (LANES, LANES))
    for ch in chains:
        for qc in range(nqc):
            r0 = ch * t + qc * LANES
            qq = qn[r0:r0 + LANES, :]
            w = jnp.concatenate(
                [jnp.concatenate([jnp.where(first_map, qq, 0.0), ones], axis=1),
                 jnp.concatenate([jnp.where(first_map, 0.0, qq), ones], axis=1)], axis=0)
            wt_scr[ch, :, qc * 2 * LANES:(qc + 1) * 2 * LANES] = w.T.astype(BF16)
        p_scr[ch, 1] = jnp.zeros(p_scr.shape[2:], BF16)
    acc_scr[...] = jnp.zeros_like(acc_scr)

    def scores(ch, j, slot):
        rows = pl.ds(pl.multiple_of(j * t, t), t)
        st = _dot(ka_scr[rows, :], wt_scr[ch])
        s_scr[ch, slot] = st
        return jnp.max(st, axis=0, keepdims=True)

    def accumulate(ch, j, slot, alpha):
        acc_scr[ch] = alpha * acc_scr[ch] + _dot(vt_scr[jnp.maximum(j, 0)], p_scr[ch, slot])

    def softmax(ch, j, st, m_cur, m_prev, l_prev):
        shift = slope2_w * ((j - (ATTN_CHAINS * i + ch)) * t).astype(F32)
        m_new = jnp.maximum(m_prev, m_cur + shift)
        p = jnp.exp2(st - (m_new - shift))
        alpha = jnp.exp2(m_prev - m_new)
        l_new = alpha * l_prev + jnp.sum(p, axis=0, keepdims=True)
        return p.astype(BF16), alpha, m_new, l_new

    def step(ch, j, slot, carry):
        m, l, m_cur, alpha_prev = carry
        m_next = scores(ch, j + 1, 1 - slot)
        accumulate(ch, j - 1, 1 - slot, alpha_prev)
        p, alpha, m, l = softmax(ch, j, s_scr[ch, slot], m_cur, m, l)
        p_scr[ch, slot] = p
        return m, l, m_next, alpha

    def finish(ch, j, slot, carry):
        m, l, _, alpha_prev = carry
        accumulate(ch, j - 1, 1 - slot, alpha_prev)
        st = s_scr[ch, slot] + dm_scr[...]
        p, alpha, m, l = softmax(ch, j, st, jnp.max(st, axis=0, keepdims=True), m, l)
        on = (alpha * acc_scr[ch] + _dot(vt_scr[j], p)) * (1.0 / l)
        for qc in range(nqc):
            c0 = qc * 2 * LANES
            ot = on[:, c0:c0 + LANES] - lam_ref[...] * on[:, c0 + LANES:c0 + 2 * LANES]
            o = ot.T
            msq = jnp.mean(o * o, axis=-1, keepdims=True)
            y = o * lax.rsqrt(msq + EPS) * sg_ref[...] * (1.0 - LAMBDA_INIT)
            r0 = ch * t + qc * LANES
            o_ref[r0:r0 + LANES, :] = y.astype(BF16)

    def pair(pp, carries):
        carries = list(carries)
        for j, slot in ((2 * pp, 0), (2 * pp + 1, 1)):
            for ch in chains:
                carries[ch] = step(ch, j, slot, carries[ch])
        return tuple(carries)

    init = tuple((jnp.full((1, width), MASK_VALUE, F32), jnp.zeros((1, width), F32),
                  scores(ch, 0, 0), jnp.ones((1, width), F32)) for ch in chains)
    carries = list(lax.fori_loop(0, (ATTN_CHAINS // 2) * i, pair, init))

    for k in chains:
        j = ATTN_CHAINS * i + k
        for ch in chains[k + 1:]:
            carries[ch] = step(ch, j, k % 2, carries[ch])
        finish(k, j, k % 2, carries[k])


def _diff_attn_fixed_kernel(q_ref, k_ref, v_ref, qg_ref, kg_ref, slope_ref, lam_ref, sg_ref, bound_ref,
                            o_ref, ka_scr, vt_scr, wt_scr, acc_scr, p_scr, dm_scr, *, t, seq):
    i = pl.program_id(2)
    nqc = t // LANES
    width = nqc * 2 * LANES
    chains = tuple(range(ATTN_CHAINS))

    gr = lax.broadcasted_iota(jnp.int32, (LANES, LANES), 0) // DIFF_DH
    gc = lax.broadcasted_iota(jnp.int32, (LANES, LANES), 1) // DIFF_DH
    gsum = jnp.where(gr == gc, 1.0, 0.0).astype(BF16)
    slope2 = slope_ref[...] * LOG2E
    slope2_w = jnp.tile(slope2, (1, width // LANES))

    def bf16_terms(value):
        terms, rem = [], value
        for _ in range(BIAS_TERMS):
            terms.append(rem.astype(BF16).astype(F32))
            rem = rem - terms[-1]
        return terms

    @pl.when(i == 0)
    def _():
        jd = lax.broadcasted_iota(jnp.int32, (t, width), 0)
        col = lax.broadcasted_iota(jnp.int32, (t, width), 1)
        il = (col & (LANES - 1)) + lax.shift_right_logical(col, 8) * LANES
        ahead = jnp.minimum(il - jd, 0).astype(F32)
        dm_scr[...] = jnp.where(_chunk_id(jd) <= _chunk_id(il), (2.0 * slope2_w) * ahead, MASK_VALUE)

        lane_t = lax.broadcasted_iota(jnp.int32, (t, LANES), 1)
        row_t = lax.broadcasted_iota(jnp.int32, (t, LANES), 0).astype(F32)
        in_tile = jnp.where((lane_t >= 2 * BIAS_TERMS) & (lane_t < 3 * BIAS_TERMS), 1.0, 0.0)
        for n, term in enumerate(bf16_terms(slope2 * row_t)):
            in_tile = jnp.where(lane_t == n, term, in_tile)
        lane_1 = lax.broadcasted_iota(jnp.int32, (1, LANES), 1)

        def body(r, carry):
            rows = pl.ds(pl.multiple_of(r * t, t), t)
            kn = _group_rms(k_ref[rows, :].astype(F32), gsum, kg_ref[...])
            ka_scr[rows, 0:LANES] = kn.astype(BF16)
            start = jnp.zeros((1, LANES), F32)
            for n, term in enumerate(bf16_terms(slope2 * jnp.asarray(r * t).astype(F32))):
                start = jnp.where(lane_1 == BIAS_TERMS + n, term, start)
            tile_lanes = (lane_t >= BIAS_TERMS) & (lane_t < 2 * BIAS_TERMS)
            ka_scr[rows, LANES:2 * LANES] = jnp.where(tile_lanes, start, in_tile).astype(BF16)
            vt_scr[r] = v_ref[rows, :].astype(F32).T.astype(BF16)
            return carry
        lax.fori_loop(0, seq // t, body, 0, unroll=2)

    qn = _group_rms(q_ref[...].astype(F32), gsum, qg_ref[...]) * (DIFF_DH ** -0.5 * LOG2E)
    row_q = lax.broadcasted_iota(jnp.int32, (LANES, LANES), 0)
    lane_q = lax.broadcasted_iota(jnp.int32, (1, LANES), 1)
    for ch in chains:
        for qc in range(nqc):
            r0 = ch * t + qc * LANES
            qt = qn[r0:r0 + LANES, :].T
            q_pos = (lane_q + (i * (ATTN_CHAINS * t) + r0)).astype(F32)
            side = jnp.where(row_q < 2 * BIAS_TERMS, 1.0, 0.0)
            for n, term in enumerate(bf16_terms(-(slope2 * q_pos + bound_ref[...]))):
                side = jnp.where(row_q == 2 * BIAS_TERMS + n, term, side)
            side = side.astype(BF16)
            c0 = qc * 2 * LANES
            wt_scr[ch, 0:LANES, c0:c0 + LANES] = jnp.where(row_q < DIFF_DH, qt, 0.0).astype(BF16)
            wt_scr[ch, 0:LANES, c0 + LANES:c0 + 2 * LANES] = jnp.where(row_q < DIFF_DH, 0.0, qt).astype(BF16)
            wt_scr[ch, LANES:2 * LANES, c0:c0 + LANES] = side
            wt_scr[ch, LANES:2 * LANES, c0 + LANES:c0 + 2 * LANES] = side
        p_scr[ch, 1] = jnp.zeros(p_scr.shape[2:], BF16)
    acc_scr[...] = jnp.zeros_like(acc_scr)

    def accumulate(ch, j, slot):
        acc_scr[ch] += _dot(vt_scr[jnp.maximum(j, 0)], p_scr[ch, slot])

    def step(ch, j, slot, l_part, diagonal=False):
        rows = pl.ds(pl.multiple_of(j * t, t), t)
        st = _dot(ka_scr[rows, :], wt_scr[ch])
        if diagonal:
            st = st + dm_scr[...]
        p = jnp.exp2(st)
        p_scr[ch, slot] = p.astype(BF16)
        accumulate(ch, j - 1, 1 - slot)
        return l_part + jnp.sum(p.reshape(t // SUBLANES, SUBLANES, width), axis=0)

    def finish(ch, j, slot, l_part):
        accumulate(ch, j, slot)
        on = acc_scr[ch] * (1.0 / jnp.sum(l_part, axis=0, keepdims=True))
        for qc in range(nqc):
            c0 = qc * 2 * LANES
            ot = on[:, c0:c0 + LANES] - lam_ref[...] * on[:, c0 + LANES:c0 + 2 * LANES]
            o = ot.T
            msq = jnp.mean(o * o, axis=-1, keepdims=True)
            y = o * lax.rsqrt(msq + EPS) * sg_ref[...] * (1.0 - LAMBDA_INIT)
            r0 = ch * t + qc * LANES
            o_ref[r0:r0 + LANES, :] = y.astype(BF16)

    def pair(pp, parts):
        parts = list(parts)
        for j, slot in ((2 * pp, 0), (2 * pp + 1, 1)):
            for ch in chains:
                parts[ch] = step(ch, j, slot, parts[ch])
        return tuple(parts)

    parts = list(lax.fori_loop(0, (ATTN_CHAINS // 2) * i, pair,
                               tuple(jnp.zeros((SUBLANES, width), F32) for _ in chains)))

    for k in chains:
        j = ATTN_CHAINS * i + k
        for ch in chains[k:]:
            parts[ch] = step(ch, j, k % 2, parts[ch], diagonal=(ch == k))
        finish(k, j, k % 2, parts[k])


def _diff_attn(proj, qg2, kg2, slopes, lam, subln_g, B, S, bound=None):
    N = proj.shape[0]
    t = min(256, S // ATTN_CHAINS)
    tq = ATTN_CHAINS * t
    nq = S // tq
    vec = pl.BlockSpec((1, LANES), lambda b, h, i: (0, 0))
    body = _diff_attn_kernel if bound is None else _diff_attn_fixed_kernel
    extra = () if bound is None else (bound,)
    return pl.pallas_call(
        functools.partial(body, t=t, seq=S),
        grid=(B, DIFF_HEADS, nq),
        in_specs=[
            pl.BlockSpec((tq, LANES), lambda b, h, i: (b * nq + i, COL_DQ // LANES + h)),
            pl.BlockSpec((S, LANES), lambda b, h, i: (b, COL_DK // LANES + h)),
            pl.BlockSpec((S, LANES), lambda b, h, i: (b, COL_DV // LANES + h)),
            vec, vec,
            pl.BlockSpec((None, 1, LANES), lambda b, h, i: (h, 0, 0)),
            vec, vec,
        ] + [vec] * len(extra),
        out_specs=pl.BlockSpec((tq, DIFF_DV), lambda b, h, i: (b * nq + i, h)),
        out_shape=jax.ShapeDtypeStruct((N, DIFF_HEADS * DIFF_DV), BF16),
        scratch_shapes=[
            pltpu.VMEM((S, 2 * LANES), BF16),
            pltpu.VMEM((S // t, DIFF_DV, t), BF16),
            pltpu.VMEM((ATTN_CHAINS, 2 * LANES, 2 * t), BF16),
            pltpu.VMEM((ATTN_CHAINS, DIFF_DV, 2 * t), F32),
        ] + ([pltpu.VMEM((ATTN_CHAINS, 2, t, 2 * t), F32)] if bound is None else []) + [
            pltpu.VMEM((ATTN_CHAINS, 2, t, 2 * t), BF16),
            pltpu.VMEM((t, 2 * t), F32),
        ],
        compiler_params=pltpu.CompilerParams(
            dimension_semantics=("parallel", "parallel", "arbitrary"), vmem_limit_bytes=VMEM_LIMIT),
        name="diff_attn" if bound is None else "diff_attn_fixed",
    )(proj, proj, proj, qg2, kg2, slopes, lam, subln_g, *extra)


def _merge_kernel(oa_ref, ob_ref, ga_ref, gb_ref, x_ref, mod_ref, wa_ref, wb_ref, wo_ref, g2_ref,
                  x1_ref, h2_ref):
    ya = _dot(oa_ref[...], wa_ref[...])
    yb = _dot(ob_ref[...], wb_ref[...])
    merged = _sigmoid(ga_ref[...].astype(F32)) * ya + _sigmoid(gb_ref[...].astype(F32)) * yb
    x1 = x_ref[...] + mod_ref[2:3, :] * _dot(merged.astype(BF16), wo_ref[...])
    x1_ref[...] = x1
    ms = jnp.mean(x1 * x1, axis=-1, keepdims=True)
    y = x1 * lax.rsqrt(ms + EPS) * g2_ref[...]
    h2_ref[...] = (y * (1.0 + mod_ref[4:5, :]) + mod_ref[3:4, :]).astype(BF16)


def _merge(o_a, o_b, proj, x2, mod3, w_gla_o, w_diff_o, w_out, g2, S):
    N, D = x2.shape
    tm = min(512, S)
    tiles_per_seq = S // tm
    tok = lambda col: pl.BlockSpec((tm, D), lambda i: (i, col))
    wspec = pl.BlockSpec((D, D), lambda i: (0, 0))
    return pl.pallas_call(
        _merge_kernel,
        grid=(N // tm,),
        in_specs=[
            tok(0), tok(0), tok(COL_GA // D), tok(COL_GB // D), tok(0),
            pl.BlockSpec((None, 6, D), lambda i: (i // tiles_per_seq, 0, 0)),
            wspec, wspec, wspec,
            pl.BlockSpec((1, D), lambda i: (0, 0)),
        ],
        out_specs=[tok(0), tok(0)],
        out_shape=[jax.ShapeDtypeStruct((N, D), F32), jax.ShapeDtypeStruct((N, D), BF16)],
        compiler_params=pltpu.CompilerParams(
            dimension_semantics=("parallel",), vmem_limit_bytes=VMEM_LIMIT),
        name="merge",
    )(o_a, o_b, proj, proj, x2, mod3, w_gla_o, w_diff_o, w_out, g2)


def _conv_ffn_kernel(h2_ref, x1_ref, mod_ref, wup_ref, cw_ref, cb_ref, wdn_ref, o_ref,
                     carry_ref, acc_ref, ubuf_ref, act_ref, *, tm, tf, tiles_per_seq):
    @pl.when(pl.program_id(0) % tiles_per_seq == 0)
    def _():
        carry_ref[...] = jnp.zeros_like(carry_ref)

    h2 = h2_ref[...]

    def up(f):
        return (_dot(h2, wup_ref[:, f * tf:(f + 1) * tf]),
                _dot(h2, wup_ref[:, D_FF + f * tf:D_FF + (f + 1) * tf]))

    def conv(u, col0, slot, half):
        cols = slice(half * tf, (half + 1) * tf)
        ubuf_ref[slot, 0:SUBLANES, cols] = carry_ref[:, col0:col0 + tf]
        ubuf_ref[slot, SUBLANES:, cols] = u
        carry_ref[:, col0:col0 + tf] = u[tm - SUBLANES:, :]
        r1 = ubuf_ref[slot, SUBLANES - 1:SUBLANES - 1 + tm, cols]
        r2 = ubuf_ref[slot, SUBLANES - 2:SUBLANES - 2 + tm, cols]
        cw = cw_ref[:, col0:col0 + tf]
        return cw[0:1, :] * r2 + cw[1:2, :] * r1 + cw[2:3, :] * u + cb_ref[:, col0:col0 + tf]

    nf = D_FF // tf
    u_next = up(0)
    for f in range(nf):
        ua, ub = u_next
        if f + 1 < nf:
            u_next = up(f + 1)
        a = conv(ua, f * tf, f % 2, 0)
        b = conv(ub, D_FF + f * tf, f % 2, 1)
        act_ref[:, (f % DOWN_GROUP) * tf:(f % DOWN_GROUP + 1) * tf] = (a * _sigmoid(a) * b).astype(BF16)
        if (f + 1) % DOWN_GROUP == 0 or f + 1 == nf:
            f0 = f - f % DOWN_GROUP
            contrib = _dot(act_ref[:, :(f + 1 - f0) * tf], wdn_ref[f0 * tf:(f + 1) * tf, :])
            if f0 == 0:
                acc_ref[...] = contrib
            else:
                acc_ref[...] += contrib

    o_ref[...] = x1_ref[...] + mod_ref[5:6, :] * acc_ref[...]


def _conv_ffn(h2, x1, mod3, w_up, conv_w, conv_b, w_down, S):
    N, D = x1.shape
    tm = min(256, S)
    tf = 256
    tiles_per_seq = S // tm
    tok = pl.BlockSpec((tm, D), lambda i: (i, 0))
    full = lambda a: pl.BlockSpec(a.shape, lambda i: (0, 0))
    return pl.pallas_call(
        functools.partial(_conv_ffn_kernel, tm=tm, tf=tf, tiles_per_seq=tiles_per_seq),
        grid=(N // tm,),
        in_specs=[
            tok, tok,
            pl.BlockSpec((None, 6, D), lambda i: (i // tiles_per_seq, 0, 0)),
            full(w_up), full(conv_w), full(conv_b), full(w_down),
        ],
        out_specs=tok,
        out_shape=jax.ShapeDtypeStruct((N, D), F32),
        scratch_shapes=[
            pltpu.VMEM((SUBLANES, 2 * D_FF), F32),
            pltpu.VMEM((tm, D), F32),
            pltpu.VMEM((2, SUBLANES + tm, 2 * tf), F32),
            pltpu.VMEM((tm, DOWN_GROUP * tf), BF16),
        ],
        compiler_params=pltpu.CompilerParams(
            dimension_semantics=("arbitrary",), vmem_limit_bytes=VMEM_LIMIT),
        name="conv_ffn",
    )(h2, x1, mod3, w_up, conv_w, conv_b, w_down)


def kernel(x, c, w_ada, b_ada, norm1_g, w_in, w_alpha_up, b_alpha, gla_norm_g, q_norm_g, k_norm_g,
           lam_q1, lam_k1, lam_q2, lam_k2, diff_norm_g, w_gla_o, w_diff_o, w_out, norm2_g, w_up,
           conv_w, conv_b, w_down):
    B, S, D = x.shape
    N = B * S
    assert D == D_MODEL and S % CHUNK == 0 and w_ada.shape[0] == 1

    w0 = w_in[0]
    a0 = COL_GG + GLA_HEADS * GLA_DV
    w_in_r = jnp.concatenate(
        [w0[:, :a0], w0[:, a0 + GLA_RANK:],
         jnp.pad(w0[:, a0:a0 + GLA_RANK], ((0, 0), (0, PROJ_COLS - COL_ACODE - GLA_RANK)))], axis=1).astype(BF16)
    wa_pad = jnp.pad(w_alpha_up[0], ((0, LANES - GLA_RANK), (0, 0))).astype(BF16)
    slopes = 2.0 ** (-8.0 * (jnp.arange(DIFF_HEADS, dtype=F32) + 1.0) / DIFF_HEADS)
    slopes = jnp.broadcast_to(slopes[:, None, None], (DIFF_HEADS, 1, LANES))
    qg2 = jnp.tile(q_norm_g, (1, 2))
    kg2 = jnp.tile(k_norm_g, (1, 2))

    mod, lam = _ada_mod(c, w_ada[0], b_ada, lam_q1, lam_k1, lam_q2, lam_k2)
    mod3 = mod.reshape(B, 6, D)
    x2 = x.reshape(N, D)

    proj = _in_proj(x2, mod3, norm1_g, w_in_r, S)
    o_a = _gla(proj, wa_pad, b_alpha, gla_norm_g, B, S)
    bound = (LOGIT_BOUND_SLACK * DIFF_DH ** 0.5 * LOG2E) * jnp.max(jnp.abs(q_norm_g)) * jnp.max(jnp.abs(k_norm_g))
    o_b = lax.cond(
        bound <= MAX_FIXED_BOUND,
        lambda: _diff_attn(proj, qg2, kg2, slopes, lam, diff_norm_g, B, S,
                           bound=jnp.broadcast_to(bound, (1, LANES)).astype(F32)),
        lambda: _diff_attn(proj, qg2, kg2, slopes, lam, diff_norm_g, B, S))
    x1, h2 = _merge(o_a, o_b, proj, x2, mod3, w_gla_o[0].astype(BF16), w_diff_o[0].astype(BF16),
                    w_out[0].astype(BF16), norm2_g, S)
    out = _conv_ffn(h2, x1, mod3, w_up[0].astype(BF16), conv_w[0], conv_b, w_down[0].astype(BF16), S)
    return out.reshape(B, S, D)
```

```python
import functools
import math

import jax
import jax.numpy as jnp
from jax import lax
from jax.experimental import pallas as pl
from jax.experimental.pallas import tpu as pltpu

F32 = jnp.float32
BF16 = jnp.bfloat16

D_MODEL = 1024
CHUNK = 64
EPS = 1e-6
GLA_HEADS = 4
GLA_DK = 128
GLA_DV = 256
GLA_RANK = 16
GLA_TAU = 16.0
GLA_HEADS_PER_STEP = 4
DIFF_HEADS = 8
DIFF_DH = 64
DIFF_DV = 128
D_FF = 2816
CONV_W = 3
DOWN_GROUP = 4
LAMBDA_INIT = 0.8 - 0.6 * math.exp(-0.3 * 0)

LANES = 128
SUBLANES = 8
MASK_VALUE = -1e30
LOG2E = math.log2(math.e)
BIAS_TERMS = 3
ATTN_CHAINS = 8
LOGIT_BOUND_SLACK = 1.02
MAX_FIXED_BOUND = 50.0

COL_GQ, COL_GK, COL_GV, COL_GG = 0, 512, 1024, 2048
COL_DQ, COL_DK, COL_DV = 3072, 4096, 5120
COL_GA, COL_GB, COL_ACODE = 6144, 7168, 8192
PROJ_COLS = 8192 + 2 * LANES
PROJ_TN = PROJ_COLS // 3

VMEM_LIMIT = 56 * 1024 * 1024


def _dot(a, b):
    return jnp.dot(a, b, preferred_element_type=F32)


def _dot_nt(a, b):
    return lax.dot_general(a, b, (((1,), (1,)), ((), ())), preferred_element_type=F32)


def _dot_tn(a, b):
    return lax.dot_general(a, b, (((0,), (0,)), ((), ())), preferred_element_type=F32)


def _split_bf16(v):
    hi = v.astype(BF16)
    lo = (v - hi.astype(F32)).astype(BF16)
    return hi, lo


def _sigmoid(v):
    return 1.0 / (1.0 + jnp.exp(-v))


def _silu(v):
    h = 0.5 * v
    return h + h * jnp.tanh(h)


def _chunk_id(pos):
    return lax.shift_right_logical(pos, CHUNK.bit_length() - 1)


def _ada_kernel(c_ref, w_ref, b_ref, lq1_ref, lk1_ref, lq2_ref, lk2_ref, mod_ref, lam_ref):
    c = c_ref[...]
    a = c * _sigmoid(c)
    a_hi, a_lo = _split_bf16(a)
    w_hi, w_lo = _split_bf16(w_ref[...])
    mod_ref[...] = _dot(a_hi, w_hi) + _dot(a_lo, w_hi) + _dot(a_hi, w_lo) + b_ref[...]
    s1 = jnp.sum(lq1_ref[...] * lk1_ref[...], axis=-1, keepdims=True)
    s2 = jnp.sum(lq2_ref[...] * lk2_ref[...], axis=-1, keepdims=True)
    lam = jnp.exp(s1) - jnp.exp(s2) + LAMBDA_INIT
    lam_ref[...] = jnp.broadcast_to(lam, lam_ref.shape)


def _ada_mod(c, w_ada, b_ada, lq1, lk1, lq2, lk2):
    B, D = c.shape
    n_out = w_ada.shape[1]
    tn = D
    small = pl.BlockSpec((1, DIFF_DH), lambda j: (0, 0))
    return pl.pallas_call(
        _ada_kernel,
        grid=(n_out // tn,),
        in_specs=[
            pl.BlockSpec((B, D), lambda j: (0, 0)),
            pl.BlockSpec((D, tn), lambda j: (0, j)),
            pl.BlockSpec((1, tn), lambda j: (0, j)),
            small, small, small, small,
        ],
        out_specs=[
            pl.BlockSpec((B, tn), lambda j: (0, j)),
            pl.BlockSpec((1, LANES), lambda j: (0, 0)),
        ],
        out_shape=[
            jax.ShapeDtypeStruct((B, n_out), F32),
            jax.ShapeDtypeStruct((1, LANES), F32),
        ],
        compiler_params=pltpu.CompilerParams(dimension_semantics=("arbitrary",)),
        name="ada_mod",
    )(c, w_ada, b_ada, lq1, lk1, lq2, lk2)


def _in_proj_kernel(x_ref, mod_ref, g_ref, w_ref, o_ref, h_scr):
    @pl.when(pl.program_id(1) == 0)
    def _():
        x = x_ref[...]
        ms = jnp.mean(x * x, axis=-1, keepdims=True)
        y = x * lax.rsqrt(ms + EPS) * g_ref[...]
        h = y * (1.0 + mod_ref[1:2, :]) + mod_ref[0:1, :]
        h_scr[...] = h.astype(BF16)

    o_ref[...] = _dot(h_scr[...], w_ref[...]).astype(BF16)


def _in_proj(x2, mod3, g1, w_in_r, S):
    N, D = x2.shape
    tm = min(1024, S)
    tiles_per_seq = S // tm
    return pl.pallas_call(
        _in_proj_kernel,
        grid=(N // tm, PROJ_COLS // PROJ_TN),
        in_specs=[
            pl.BlockSpec((tm, D), lambda i, j: (i, 0)),
            pl.BlockSpec((None, 6, D), lambda i, j: (i // tiles_per_seq, 0, 0)),
            pl.BlockSpec((1, D), lambda i, j: (0, 0)),
            pl.BlockSpec((D, PROJ_TN), lambda i, j: (0, j)),
        ],
        out_specs=pl.BlockSpec((tm, PROJ_TN), lambda i, j: (i, j)),
        out_shape=jax.ShapeDtypeStruct((N, PROJ_COLS), BF16),
        scratch_shapes=[pltpu.VMEM((tm, D), BF16)],
        compiler_params=pltpu.CompilerParams(
            dimension_semantics=("parallel", "arbitrary"), vmem_limit_bytes=VMEM_LIMIT),
        name="in_proj",
    )(x2, mod3, g1, w_in_r)


def _gla_kernel(q_ref, k_ref, v_ref, g_ref, a_ref, wa_ref, ba_ref, ng_ref, o_ref, st_ref, *, tc):
    nchunk = tc // CHUNK

    @pl.when(pl.program_id(2) == 0)
    def _():
        st_ref[...] = jnp.zeros_like(st_ref)

    row = lax.broadcasted_iota(jnp.int32, (tc, tc), 0)
    col = lax.broadcasted_iota(jnp.int32, (tc, tc), 1)
    chunk_gap = _chunk_id(row) - _chunk_id(col)
    lower = (chunk_gap == 0) & (col <= row)
    upper = (chunk_gap == 0) & (col > row)
    tri = jnp.where(lower, 1.0, 0.0).astype(BF16)

    def per_chunk(rows):
        return jnp.concatenate([jnp.broadcast_to(r, (CHUNK, GLA_DK)) for r in rows], axis=0)

    def head(hh):
        dk = slice(hh * GLA_DK, (hh + 1) * GLA_DK)
        dv = slice(hh * GLA_DV, (hh + 1) * GLA_DV)
        z = _dot(a_ref[...], wa_ref[:, dk]) + ba_ref[:, dk]
        yield
        log_a = (jnp.minimum(z, 0.0) - jnp.log1p(jnp.exp(-jnp.abs(z)))) * (1.0 / GLA_TAU)
        la_hi, la_lo = _split_bf16(log_a)
        bcum = _dot(tri, la_hi) + _dot(tri, la_lo)
        yield
        b_last = [bcum[c * CHUNK + CHUNK - 1:(c + 1) * CHUNK, :] for c in range(nchunk)]
        prefix = [jnp.zeros_like(b_last[0])]
        for c in range(nchunk):
            prefix.append(prefix[-1] + b_last[c])
        one = jnp.ones_like(b_last[0])

        eb = jnp.exp(bcum)
        enb = jnp.exp(-bcum)
        qs = q_ref[:, dk].astype(F32) * (GLA_DK ** -0.5)
        k = k_ref[:, dk].astype(F32)
        v = v_ref[:, dv]
        q_fwd = qs * eb
        q_f = q_fwd.astype(BF16)
        k_dec = k * jnp.exp(per_chunk(b_last) - bcum)
        k_d = k_dec.astype(BF16)

        s_fwd = _dot_nt(q_f, (k * enb).astype(BF16))
        s_bwd = _dot_nt((qs * enb).astype(BF16), (k * eb).astype(BF16))
        yield
        scores = jnp.where(lower, s_fwd, jnp.where(upper, s_bwd, 0.0))
        for off in range(1, nchunk):
            if off == 1:
                q_x = q_f
            else:
                between = [one] * off + [jnp.exp(prefix[c] - prefix[c - off + 1]) for c in range(off, nchunk)]
                q_x = (q_fwd * per_chunk(between)).astype(BF16)
            scores = jnp.where(chunk_gap == off, _dot_nt(q_x, k_d), scores)
            yield

        st = st_ref[hh]
        q_s = (q_fwd * per_chunk([jnp.exp(p) for p in prefix[:nchunk]])).astype(BF16)
        o = _dot(scores.astype(BF16), v) + _dot_nt(q_s, st.astype(BF16))
        to_end = [jnp.exp(prefix[nchunk] - prefix[c + 1]) for c in range(nchunk)]
        st_ref[hh] = st * jnp.exp(prefix[nchunk]) + _dot_tn(v, (k_dec * per_chunk(to_end)).astype(BF16))
        yield

        ms = jnp.mean(o * o, axis=-1, keepdims=True)
        y = o * lax.rsqrt(ms + EPS) * ng_ref[...]
        g = g_ref[:, dv].astype(F32)
        o_ref[:, dv] = (y * _silu(g)).astype(BF16)

    heads = [head(hh) for hh in range(GLA_HEADS_PER_STEP)]
    while heads:
        heads = [h for h in heads if next(h, True) is None]


def _gla(proj, wa_pad, b_alpha, gla_norm_g, B, S):
    N = proj.shape[0]
    tc = min(256, S)
    nt = S // tc
    hps = GLA_HEADS_PER_STEP
    row = lambda b, h, i: b * nt + i
    return pl.pallas_call(
        functools.partial(_gla_kernel, tc=tc),
        grid=(B, GLA_HEADS // hps, nt),
        in_specs=[
            pl.BlockSpec((tc, hps * GLA_DK), lambda b, h, i: (row(b, h, i), COL_GQ // (hps * GLA_DK) + h)),
            pl.BlockSpec((tc, hps * GLA_DK), lambda b, h, i: (row(b, h, i), COL_GK // (hps * GLA_DK) + h)),
            pl.BlockSpec((tc, hps * GLA_DV), lambda b, h, i: (row(b, h, i), COL_GV // (hps * GLA_DV) + h)),
            pl.BlockSpec((tc, hps * GLA_DV), lambda b, h, i: (row(b, h, i), COL_GG // (hps * GLA_DV) + h)),
            pl.BlockSpec((tc, LANES), lambda b, h, i: (row(b, h, i), COL_ACODE // LANES)),
            pl.BlockSpec((LANES, hps * GLA_DK), lambda b, h, i: (0, h)),
            pl.BlockSpec((1, hps * GLA_DK), lambda b, h, i: (0, h)),
            pl.BlockSpec((1, GLA_DV), lambda b, h, i: (0, 0)),
        ],
        out_specs=pl.BlockSpec((tc, hps * GLA_DV), lambda b, h, i: (row(b, h, i), h)),
        out_shape=jax.ShapeDtypeStruct((N, GLA_HEADS * GLA_DV), BF16),
        scratch_shapes=[pltpu.VMEM((hps, GLA_DV, GLA_DK), F32)],
        compiler_params=pltpu.CompilerParams(
            dimension_semantics=("parallel", "parallel", "arbitrary"), vmem_limit_bytes=VMEM_LIMIT),
        name="gla",
    )(proj, proj, proj, proj, proj, wa_pad, b_alpha, gla_norm_g)


def _group_rms(xf, gsum, gain):
    ss = _dot((xf * xf).astype(BF16), gsum)
    return xf * lax.rsqrt(ss * (1.0 / DIFF_DH) + EPS) * gain


def _diff_attn_kernel(q_ref, k_ref, v_ref, qg_ref, kg_ref, slope_ref, lam_ref, sg_ref, o_ref,
                      ka_scr, vt_scr, wt_scr, acc_scr, s_scr, p_scr, dm_scr, *, t, seq):
    i = pl.program_id(2)
    nqc = t // LANES
    width = nqc * 2 * LANES
    chains = tuple(range(ATTN_CHAINS))
    lane = lax.broadcasted_iota(jnp.int32, (1, LANES), 1)
    first_map = lane < DIFF_DH

    gr = lax.broadcasted_iota(jnp.int32, (LANES, LANES), 0) // DIFF_DH
    gc = lax.broadcasted_iota(jnp.int32, (LANES, LANES), 1) // DIFF_DH
    gsum = jnp.where(gr == gc, 1.0, 0.0).astype(BF16)
    slope2 = slope_ref[...] * LOG2E
    slope2_w = jnp.tile(slope2, (1, width // LANES))

    @pl.when(i == 0)
    def _():
        jl = lax.broadcasted_iota(jnp.int32, (t, LANES), 0).astype(F32)
        lane_t = lax.broadcasted_iota(jnp.int32, (t, LANES), 1)
        rem = slope2 * jl
        aux = jnp.zeros((t, LANES), F32)
        for term in range(BIAS_TERMS):
            part = rem.astype(BF16).astype(F32)
            aux = jnp.where(lane_t == term, part, aux)
            rem = rem - part
        aux = aux.astype(BF16)

        jd = lax.broadcasted_iota(jnp.int32, (t, width), 0)
        col = lax.broadcasted_iota(jnp.int32, (t, width), 1)
        il = (col & (LANES - 1)) + lax.shift_right_logical(col, (2 * LANES).bit_length() - 1) * LANES
        ahead = jnp.minimum(il - jd, 0).astype(F32)
        dm_scr[...] = jnp.where(_chunk_id(jd) <= _chunk_id(il), (2.0 * slope2_w) * ahead, MASK_VALUE)

        def body(r, carry):
            rows = pl.ds(pl.multiple_of(r * t, t), t)
            kn = _group_rms(k_ref[rows, :].astype(F32), gsum, kg_ref[...])
            ka_scr[rows, 0:LANES] = kn.astype(BF16)
            ka_scr[rows, LANES:2 * LANES] = aux
            vt_scr[r] = v_ref[rows, :].astype(F32).T.astype(BF16)
            return carry
        lax.fori_loop(0, seq // t, body, 0, unroll=2)

    qn = _group_rms(q_ref[...].astype(F32), gsum, qg_ref[...]) * (DIFF_DH ** -0.5 * LOG2E)
    ones = jnp.broadcast_to(jnp.where(lane < BIAS_TERMS, 1.0, 0.0), (LANES, LANES))
    for ch in chains:
        for qc in range(nqc):
            r0 = ch * t + qc * LANES
            qq = qn[r0:r0 + LANES, :]
            w = jnp.concatenate(
                [jnp.concatenate([jnp.where(first_map, qq, 0.0), ones], axis=1),
                 jnp.concatenate([jnp.where(first_map, 0.0, qq), ones], axis=1)], axis=0)
            wt_scr[ch, :, qc * 2 * LANES:(qc + 1) * 2 * LANES] = w.T.astype(BF16)
        p_scr[ch, 1] = jnp.zeros(p_scr.shape[2:], BF16)
    acc_scr[...] = jnp.zeros_like(acc_scr)

    def scores(ch, j, slot):
        rows = pl.ds(pl.multiple_of(j * t, t), t)
        st = _dot(ka_scr[rows, :], wt_scr[ch])
        s_scr[ch, slot] = st
        return jnp.max(st, axis=0, keepdims=True)

    def accumulate(ch, j, slot, alpha):
        acc_scr[ch] = alpha * acc_scr[ch] + _dot(vt_scr[jnp.maximum(j, 0)], p_scr[ch, slot])

    def softmax(ch, j, st, m_cur, m_prev, l_prev):
        shift = slope2_w * ((j - (ATTN_CHAINS * i + ch)) * t).astype(F32)
        m_new = jnp.maximum(m_prev, m_cur + shift)
        p = jnp.exp2(st - (m_new - shift))
        alpha = jnp.exp2(m_prev - m_new)
        l_new = alpha * l_prev + jnp.sum(p, axis=0, keepdims=True)
        return p.astype(BF16), alpha, m_new, l_new

    def step(ch, j, slot, carry):
        m, l, m_cur, alpha_prev = carry
        m_next = scores(ch, j + 1, 1 - slot)
        accumulate(ch, j - 1, 1 - slot, alpha_prev)
        p, alpha, m, l = softmax(ch, j, s_scr[ch, slot], m_cur, m, l)
        p_scr[ch, slot] = p
        return m, l, m_next, alpha

    def finish(ch, j, slot, carry):
        m, l, _, alpha_prev = carry
        accumulate(ch, j - 1, 1 - slot, alpha_prev)
        st = s_scr[ch, slot] + dm_scr[...]
        p, alpha, m, l = softmax(ch, j, st, jnp.max(st, axis=0, keepdims=True), m, l)
        on = (alpha * acc_scr[ch] + _dot(vt_scr[j], p)) * (1.0 / l)
        for qc in range(nqc):
            c0 = qc * 2 * LANES
            ot = on[:, c0:c0 + LANES] - lam_ref[...] * on[:, c0 + LANES:c0 + 2 * LANES]
            o = ot.T
            msq = jnp.mean(o * o, axis=-1, keepdims=True)
            y = o * lax.rsqrt(msq + EPS) * sg_ref[...] * (1.0 - LAMBDA_INIT)
            r0 = ch * t + qc * LANES
            o_ref[r0:r0 + LANES, :] = y.astype(BF16)

    def pair(pp, carries):
        carries = list(carries)
        for j, slot in ((2 * pp, 0), (2 * pp + 1, 1)):
            for ch in chains:
                carries[ch] = step(ch, j, slot, carries[ch])
        return tuple(carries)

    init = tuple((jnp.full((1, width), MASK_VALUE, F32), jnp.zeros((1, width), F32),
                  scores(ch, 0, 0), jnp.ones((1, width), F32)) for ch in chains)
    carries = list(lax.fori_loop(0, (ATTN_CHAINS // 2) * i, pair, init))

    for k in chains:
        j = ATTN_CHAINS * i + k
        for ch in chains[k + 1:]:
            carries[ch] = step(ch, j, k % 2, carries[ch])
        finish(k, j, k % 2, carries[k])


def _diff_attn_fixed_kernel(q_ref, k_ref, v_ref, qg_ref, kg_ref, slope_ref, lam_ref, sg_ref, bound_ref,
                            o_ref, ka_scr, vt_scr, wt_scr, acc_scr, p_scr, dm_scr, *, t, seq):
    i = pl.program_id(2)
    nqc = t // LANES
    width = nqc * 2 * LANES
    chains = tuple(range(ATTN_CHAINS))

    gr = lax.broadcasted_iota(jnp.int32, (LANES, LANES), 0) // DIFF_DH
    gc = lax.broadcasted_iota(jnp.int32, (LANES, LANES), 1) // DIFF_DH
    gsum = jnp.where(gr == gc, 1.0, 0.0).astype(BF16)
    slope2 = slope_ref[...] * LOG2E
    slope2_w = jnp.tile(slope2, (1, width // LANES))

    def bf16_terms(value):
        terms, rem = [], value
        for _ in range(BIAS_TERMS):
            terms.append(rem.astype(BF16).astype(F32))
            rem = rem - terms[-1]
        return terms

    @pl.when(i == 0)
    def _():
        jd = lax.broadcasted_iota(jnp.int32, (t, width), 0)
        col = lax.broadcasted_iota(jnp.int32, (t, width), 1)
        il = (col & (LANES - 1)) + lax.shift_right_logical(col, (2 * LANES).bit_length() - 1) * LANES
        ahead = jnp.minimum(il - jd, 0).astype(F32)
        dm_scr[...] = jnp.where(_chunk_id(jd) <= _chunk_id(il), (2.0 * slope2_w) * ahead, MASK_VALUE)

        lane_t = lax.broadcasted_iota(jnp.int32, (t, LANES), 1)
        row_t = lax.broadcasted_iota(jnp.int32, (t, LANES), 0).astype(F32)
        in_tile = jnp.where((lane_t >= 2 * BIAS_TERMS) & (lane_t < 3 * BIAS_TERMS), 1.0, 0.0)
        for n, term in enumerate(bf16_terms(slope2 * row_t)):
            in_tile = jnp.where(lane_t == n, term, in_tile)
        lane_1 = lax.broadcasted_iota(jnp.int32, (1, LANES), 1)

        def body(r, carry):
            rows = pl.ds(pl.multiple_of(r * t, t), t)
            kn = _group_rms(k_ref[rows, :].astype(F32), gsum, kg_ref[...])
            ka_scr[rows, 0:LANES] = kn.astype(BF16)
            start = jnp.zeros((1, LANES), F32)
            for n, term in enumerate(bf16_terms(slope2 * jnp.asarray(r * t).astype(F32))):
                start = jnp.where(lane_1 == BIAS_TERMS + n, term, start)
            tile_lanes = (lane_t >= BIAS_TERMS) & (lane_t < 2 * BIAS_TERMS)
            ka_scr[rows, LANES:2 * LANES] = jnp.where(tile_lanes, start, in_tile).astype(BF16)
            vt_scr[r] = v_ref[rows, :].astype(F32).T.astype(BF16)
            return carry
        lax.fori_loop(0, seq // t, body, 0, unroll=2)

    qn = _group_rms(q_ref[...].astype(F32), gsum, qg_ref[...]) * (DIFF_DH ** -0.5 * LOG2E)
    row_q = lax.broadcasted_iota(jnp.int32, (LANES, LANES), 0)
    lane_q = lax.broadcasted_iota(jnp.int32, (1, LANES), 1)
    for ch in chains:
        for qc in range(nqc):
            r0 = ch * t + qc * LANES
            qt = qn[r0:r0 + LANES, :].T
            q_pos = (lane_q + (i * (ATTN_CHAINS * t) + r0)).astype(F32)
            side = jnp.where(row_q < 2 * BIAS_TERMS, 1.0, 0.0)
            for n, term in enumerate(bf16_terms(-(slope2 * q_pos + bound_ref[...]))):
                side = jnp.where(row_q == 2 * BIAS_TERMS + n, term, side)
            side = side.astype(BF16)
            c0 = qc * 2 * LANES
            wt_scr[ch, 0:LANES, c0:c0 + LANES] = jnp.where(row_q < DIFF_DH, qt, 0.0).astype(BF16)
            wt_scr[ch, 0:LANES, c0 + LANES:c0 + 2 * LANES] = jnp.where(row_q < DIFF_DH, 0.0, qt).astype(BF16)
            wt_scr[ch, LANES:2 * LANES, c0:c0 + LANES] = side
            wt_scr[ch, LANES:2 * LANES, c0 + LANES:c0 + 2 * LANES] = side
        p_scr[ch, 1] = jnp.zeros(p_scr.shape[2:], BF16)
    acc_scr[...] = jnp.zeros_like(acc_scr)

    def accumulate(ch, j, slot):
        acc_scr[ch] += _dot(vt_scr[jnp.maximum(j, 0)], p_scr[ch, slot])

    def step(ch, j, slot, l_part, diagonal=False):
        rows = pl.ds(pl.multiple_of(j * t, t), t)
        st = _dot(ka_scr[rows, :], wt_scr[ch])
        if diagonal:
            st = st + dm_scr[...]
        p = jnp.exp2(st)
        p_scr[ch, slot] = p.astype(BF16)
        accumulate(ch, j - 1, 1 - slot)
        return l_part + jnp.sum(p.reshape(t // SUBLANES, SUBLANES, width), axis=0)

    def finish(ch, j, slot, l_part):
        accumulate(ch, j, slot)
        on = acc_scr[ch] * (1.0 / jnp.sum(l_part, axis=0, keepdims=True))
        for qc in range(nqc):
            c0 = qc * 2 * LANES
            ot = on[:, c0:c0 + LANES] - lam_ref[...] * on[:, c0 + LANES:c0 + 2 * LANES]
            o = ot.T
            msq = jnp.mean(o * o, axis=-1, keepdims=True)
            y = o * lax.rsqrt(msq + EPS) * sg_ref[...] * (1.0 - LAMBDA_INIT)
            r0 = ch * t + qc * LANES
            o_ref[r0:r0 + LANES, :] = y.astype(BF16)

    def pair(pp, parts):
        parts = list(parts)
        for j, slot in ((2 * pp, 0), (2 * pp + 1, 1)):
            for ch in chains:
                parts[ch] = step(ch, j, slot, parts[ch])
        return tuple(parts)

    parts = list(lax.fori_loop(0, (ATTN_CHAINS // 2) * i, pair,
                               tuple(jnp.zeros((SUBLANES, width), F32) for _ in chains)))

    for k in chains:
        j = ATTN_CHAINS * i + k
        for ch in chains[k:]:
            parts[ch] = step(ch, j, k % 2, parts[ch], diagonal=(ch == k))
        finish(k, j, k % 2, parts[k])


def _diff_attn(proj, qg2, kg2, slopes, lam, subln_g, B, S, bound=None):
    N = proj.shape[0]
    t = min(256, S // ATTN_CHAINS)
    tq = ATTN_CHAINS * t
    nq = S // tq
    vec = pl.BlockSpec((1, LANES), lambda b, h, i: (0, 0))
    body = _diff_attn_kernel if bound is None else _diff_attn_fixed_kernel
    extra = () if bound is None else (bound,)
    return pl.pallas_call(
        functools.partial(body, t=t, seq=S),
        grid=(B, DIFF_HEADS, nq),
        in_specs=[
            pl.BlockSpec((tq, LANES), lambda b, h, i: (b * nq + i, COL_DQ // LANES + h)),
            pl.BlockSpec((S, LANES), lambda b, h, i: (b, COL_DK // LANES + h)),
            pl.BlockSpec((S, LANES), lambda b, h, i: (b, COL_DV // LANES + h)),
            vec, vec,
            pl.BlockSpec((None, 1, LANES), lambda b, h, i: (h, 0, 0)),
            vec, vec,
        ] + [vec] * len(extra),
        out_specs=pl.BlockSpec((tq, DIFF_DV), lambda b, h, i: (b * nq + i, h)),
        out_shape=jax.ShapeDtypeStruct((N, DIFF_HEADS * DIFF_DV), BF16),
        scratch_shapes=[
            pltpu.VMEM((S, 2 * LANES), BF16),
            pltpu.VMEM((S // t, DIFF_DV, t), BF16),
            pltpu.VMEM((ATTN_CHAINS, 2 * LANES, 2 * t), BF16),
            pltpu.VMEM((ATTN_CHAINS, DIFF_DV, 2 * t), F32),
        ] + ([pltpu.VMEM((ATTN_CHAINS, 2, t, 2 * t), F32)] if bound is None else []) + [
            pltpu.VMEM((ATTN_CHAINS, 2, t, 2 * t), BF16),
            pltpu.VMEM((t, 2 * t), F32),
        ],
        compiler_params=pltpu.CompilerParams(
            dimension_semantics=("parallel", "parallel", "arbitrary"), vmem_limit_bytes=VMEM_LIMIT),
        name="diff_attn" if bound is None else "diff_attn_fixed",
    )(proj, proj, proj, qg2, kg2, slopes, lam, subln_g, *extra)


def _merge_kernel(oa_ref, ob_ref, ga_ref, gb_ref, x_ref, mod_ref, wa_ref, wb_ref, wo_ref, g2_ref,
                  x1_ref, h2_ref):
    ya = _dot(oa_ref[...], wa_ref[...])
    yb = _dot(ob_ref[...], wb_ref[...])
    merged = _sigmoid(ga_ref[...].astype(F32)) * ya + _sigmoid(gb_ref[...].astype(F32)) * yb
    x1 = x_ref[...] + mod_ref[2:3, :] * _dot(merged.astype(BF16), wo_ref[...])
    x1_ref[...] = x1
    ms = jnp.mean(x1 * x1, axis=-1, keepdims=True)
    y = x1 * lax.rsqrt(ms + EPS) * g2_ref[...]
    h2_ref[...] = (y * (1.0 + mod_ref[4:5, :]) + mod_ref[3:4, :]).astype(BF16)


def _merge(o_a, o_b, proj, x2, mod3, w_gla_o, w_diff_o, w_out, g2, S):
    N, D = x2.shape
    tm = min(512, S)
    tiles_per_seq = S // tm
    tok = lambda col: pl.BlockSpec((tm, D), lambda i: (i, col))
    wspec = pl.BlockSpec((D, D), lambda i: (0, 0))
    return pl.pallas_call(
        _merge_kernel,
        grid=(N // tm,),
        in_specs=[
            tok(0), tok(0), tok(COL_GA // D), tok(COL_GB // D), tok(0),
            pl.BlockSpec((None, 6, D), lambda i: (i // tiles_per_seq, 0, 0)),
            wspec, wspec, wspec,
            pl.BlockSpec((1, D), lambda i: (0, 0)),
        ],
        out_specs=[tok(0), tok(0)],
        out_shape=[jax.ShapeDtypeStruct((N, D), F32), jax.ShapeDtypeStruct((N, D), BF16)],
        compiler_params=pltpu.CompilerParams(
            dimension_semantics=("parallel",), vmem_limit_bytes=VMEM_LIMIT),
        name="merge",
    )(o_a, o_b, proj, proj, x2, mod3, w_gla_o, w_diff_o, w_out, g2)


def _conv_ffn_kernel(h2_ref, x1_ref, mod_ref, wup_ref, cw_ref, cb_ref, wdn_ref, o_ref,
                     carry_ref, acc_ref, ubuf_ref, act_ref, *, tm, tf, tiles_per_seq):
    @pl.when(pl.program_id(0) % tiles_per_seq == 0)
    def _():
        carry_ref[...] = jnp.zeros_like(carry_ref)

    h2 = h2_ref[...]

    def up(f):
        return (_dot(h2, wup_ref[:, f * tf:(f + 1) * tf]),
                _dot(h2, wup_ref[:, D_FF + f * tf:D_FF + (f + 1) * tf]))

    def conv(u, col0, slot, half):
        cols = slice(half * tf, (half + 1) * tf)
        ubuf_ref[slot, 0:SUBLANES, cols] = carry_ref[:, col0:col0 + tf]
        ubuf_ref[slot, SUBLANES:, cols] = u
        carry_ref[:, col0:col0 + tf] = u[tm - SUBLANES:, :]
        r1 = ubuf_ref[slot, SUBLANES - 1:SUBLANES - 1 + tm, cols]
        r2 = ubuf_ref[slot, SUBLANES - 2:SUBLANES - 2 + tm, cols]
        cw = cw_ref[:, col0:col0 + tf]
        return cw[0:1, :] * r2 + cw[1:2, :] * r1 + cw[2:3, :] * u + cb_ref[:, col0:col0 + tf]

    nf = D_FF // tf
    u_next = up(0)
    for f in range(nf):
        ua, ub = u_next
        if f + 1 < nf:
            u_next = up(f + 1)
        a = conv(ua, f * tf, f % 2, 0)
        b = conv(ub, D_FF + f * tf, f % 2, 1)
        act_ref[:, (f % DOWN_GROUP) * tf:(f % DOWN_GROUP + 1) * tf] = (_silu(a) * b).astype(BF16)
        if (f + 1) % DOWN_GROUP == 0 or f + 1 == nf:
            f0 = f - f % DOWN_GROUP
            contrib = _dot(act_ref[:, :(f + 1 - f0) * tf], wdn_ref[f0 * tf:(f + 1) * tf, :])
            if f0 == 0:
                acc_ref[...] = contrib
            else:
                acc_ref[...] += contrib

    o_ref[...] = x1_ref[...] + mod_ref[5:6, :] * acc_ref[...]


def _conv_ffn(h2, x1, mod3, w_up, conv_w, conv_b, w_down, S):
    N, D = x1.shape
    tm = min(256, S)
    tf = 256
    tiles_per_seq = S // tm
    tok = pl.BlockSpec((tm, D), lambda i: (i, 0))
    full = lambda a: pl.BlockSpec(a.shape, lambda i: (0, 0))
    return pl.pallas_call(
        functools.partial(_conv_ffn_kernel, tm=tm, tf=tf, tiles_per_seq=tiles_per_seq),
        grid=(N // tm,),
        in_specs=[
            tok, tok,
            pl.BlockSpec((None, 6, D), lambda i: (i // tiles_per_seq, 0, 0)),
            full(w_up), full(conv_w), full(conv_b), full(w_down),
        ],
        out_specs=tok,
        out_shape=jax.ShapeDtypeStruct((N, D), F32),
        scratch_shapes=[
            pltpu.VMEM((SUBLANES, 2 * D_FF), F32),
            pltpu.VMEM((tm, D), F32),
            pltpu.VMEM((2, SUBLANES + tm, 2 * tf), F32),
            pltpu.VMEM((tm, DOWN_GROUP * tf), BF16),
        ],
        compiler_params=pltpu.CompilerParams(
            dimension_semantics=("arbitrary",), vmem_limit_bytes=VMEM_LIMIT),
        name="conv_ffn",
    )(h2, x1, mod3, w_up, conv_w, conv_b, w_down)


def kernel(x, c, w_ada, b_ada, norm1_g, w_in, w_alpha_up, b_alpha, gla_norm_g, q_norm_g, k_norm_g,
           lam_q1, lam_k1, lam_q2, lam_k2, diff_norm_g, w_gla_o, w_diff_o, w_out, norm2_g, w_up,
           conv_w, conv_b, w_down):
    B, S, D = x.shape
    N = B * S
    assert D == D_MODEL and S % CHUNK == 0 and w_ada.shape[0] == 1

    w0 = w_in[0]
    a0 = COL_GG + GLA_HEADS * GLA_DV
    w_in_r = jnp.concatenate(
        [w0[:, :a0], w0[:, a0 + GLA_RANK:],
         jnp.pad(w0[:, a0:a0 + GLA_RANK], ((0, 0), (0, PROJ_COLS - COL_ACODE - GLA_RANK)))], axis=1).astype(BF16)
    wa_pad = jnp.pad(w_alpha_up[0], ((0, LANES - GLA_RANK), (0, 0))).astype(BF16)
    slopes = 2.0 ** (-8.0 * (jnp.arange(DIFF_HEADS, dtype=F32) + 1.0) / DIFF_HEADS)
    slopes = jnp.broadcast_to(slopes[:, None, None], (DIFF_HEADS, 1, LANES))
    qg2 = jnp.tile(q_norm_g, (1, 2))
    kg2 = jnp.tile(k_norm_g, (1, 2))

    mod, lam = _ada_mod(c, w_ada[0], b_ada, lam_q1, lam_k1, lam_q2, lam_k2)
    mod3 = mod.reshape(B, 6, D)
    x2 = x.reshape(N, D)

    proj = _in_proj(x2, mod3, norm1_g, w_in_r, S)
    o_a = _gla(proj, wa_pad, b_alpha, gla_norm_g, B, S)
    bound = (LOGIT_BOUND_SLACK * DIFF_DH ** 0.5 * LOG2E) * jnp.max(jnp.abs(q_norm_g)) * jnp.max(jnp.abs(k_norm_g))
    o_b = lax.cond(
        bound <= MAX_FIXED_BOUND,
        lambda: _diff_attn(proj, qg2, kg2, slopes, lam, diff_norm_g, B, S,
                           bound=jnp.broadcast_to(bound, (1, LANES)).astype(F32)),
        lambda: _diff_attn(proj, qg2, kg2, slopes, lam, diff_norm_g, B, S))
    x1, h2 = _merge(o_a, o_b, proj, x2, mod3, w_gla_o[0].astype(BF16), w_diff_o[0].astype(BF16),
                    w_out[0].astype(BF16), norm2_g, S)
    out = _conv_ffn(h2, x1, mod3, w_up[0].astype(BF16), conv_w[0], conv_b, w_down[0].astype(BF16), S)
    return out.reshape(B, S, D)
```

```python
import functools
import math

import jax
import jax.numpy as jnp
from jax import lax
from jax.experimental import pallas as pl
from jax.experimental.pallas import tpu as pltpu

F32 = jnp.float32
BF16 = jnp.bfloat16

D_MODEL = 1024
CHUNK = 64
EPS = 1e-6
GLA_HEADS = 4
GLA_DK = 128
GLA_DV = 256
GLA_RANK = 16
GLA_TAU = 16.0
GLA_HEADS_PER_STEP = 4
DIFF_HEADS = 8
DIFF_DH = 64
DIFF_DV = 128
D_FF = 2816
CONV_W = 3
DOWN_GROUP = 4
LAMBDA_INIT = 0.8 - 0.6 * math.exp(-0.3 * 0)

LANES = 128
SUBLANES = 8
MASK_VALUE = -1e30
LOG2E = math.log2(math.e)
BIAS_TERMS = 3
ATTN_CHAINS = 8
LOGIT_BOUND_SLACK = 1.02
MAX_FIXED_BOUND = 50.0

COL_GQ, COL_GK, COL_GV, COL_GG = 0, 512, 1024, 2048
COL_DQ, COL_DK, COL_DV = 3072, 4096, 5120
COL_GA, COL_GB, COL_ACODE = 6144, 7168, 8192
PROJ_COLS = 8192 + 2 * LANES
PROJ_TN = PROJ_COLS // 3

VMEM_LIMIT = 56 * 1024 * 1024


def _dot(a, b):
    return jnp.dot(a, b, preferred_element_type=F32)


def _dot_nt(a, b):
    return lax.dot_general(a, b, (((1,), (1,)), ((), ())), preferred_element_type=F32)


def _dot_tn(a, b):
    return lax.dot_general(a, b, (((0,), (0,)), ((), ())), preferred_element_type=F32)


def _split_bf16(v):
    hi = v.astype(BF16)
    lo = (v - hi.astype(F32)).astype(BF16)
    return hi, lo


def _sigmoid(v):
    return 0.5 + 0.5 * jnp.tanh(0.5 * v)


def _silu(v):
    h = 0.5 * v
    return h + h * jnp.tanh(h)


def _chunk_id(pos):
    return lax.shift_right_logical(pos, CHUNK.bit_length() - 1)


def _ada_kernel(c_ref, w_ref, b_ref, lq1_ref, lk1_ref, lq2_ref, lk2_ref, mod_ref, lam_ref):
    c = c_ref[...]
    a = c * _sigmoid(c)
    a_hi, a_lo = _split_bf16(a)
    w_hi, w_lo = _split_bf16(w_ref[...])
    mod_ref[...] = _dot(a_hi, w_hi) + _dot(a_lo, w_hi) + _dot(a_hi, w_lo) + b_ref[...]
    s1 = jnp.sum(lq1_ref[...] * lk1_ref[...], axis=-1, keepdims=True)
    s2 = jnp.sum(lq2_ref[...] * lk2_ref[...], axis=-1, keepdims=True)
    lam = jnp.exp(s1) - jnp.exp(s2) + LAMBDA_INIT
    lam_ref[...] = jnp.broadcast_to(lam, lam_ref.shape)


def _ada_mod(c, w_ada, b_ada, lq1, lk1, lq2, lk2):
    B, D = c.shape
    n_out = w_ada.shape[1]
    tn = D
    small = pl.BlockSpec((1, DIFF_DH), lambda j: (0, 0))
    return pl.pallas_call(
        _ada_kernel,
        grid=(n_out // tn,),
        in_specs=[
            pl.BlockSpec((B, D), lambda j: (0, 0)),
            pl.BlockSpec((D, tn), lambda j: (0, j)),
            pl.BlockSpec((1, tn), lambda j: (0, j)),
            small, small, small, small,
        ],
        out_specs=[
            pl.BlockSpec((B, tn), lambda j: (0, j)),
            pl.BlockSpec((1, LANES), lambda j: (0, 0)),
        ],
        out_shape=[
            jax.ShapeDtypeStruct((B, n_out), F32),
            jax.ShapeDtypeStruct((1, LANES), F32),
        ],
        compiler_params=pltpu.CompilerParams(dimension_semantics=("arbitrary",)),
        name="ada_mod",
    )(c, w_ada, b_ada, lq1, lk1, lq2, lk2)


def _in_proj_kernel(x_ref, mod_ref, g_ref, w_ref, o_ref, h_scr):
    @pl.when(pl.program_id(1) == 0)
    def _():
        x = x_ref[...]
        ms = jnp.mean(x * x, axis=-1, keepdims=True)
        y = x * lax.rsqrt(ms + EPS) * g_ref[...]
        h = y * (1.0 + mod_ref[1:2, :]) + mod_ref[0:1, :]
        h_scr[...] = h.astype(BF16)

    o_ref[...] = _dot(h_scr[...], w_ref[...]).astype(BF16)


def _in_proj(x2, mod3, g1, w_in_r, S):
    N, D = x2.shape
    tm = min(1024, S)
    tiles_per_seq = S // tm
    return pl.pallas_call(
        _in_proj_kernel,
        grid=(N // tm, PROJ_COLS // PROJ_TN),
        in_specs=[
            pl.BlockSpec((tm, D), lambda i, j: (i, 0)),
            pl.BlockSpec((None, 6, D), lambda i, j: (i // tiles_per_seq, 0, 0)),
            pl.BlockSpec((1, D), lambda i, j: (0, 0)),
            pl.BlockSpec((D, PROJ_TN), lambda i, j: (0, j)),
        ],
        out_specs=pl.BlockSpec((tm, PROJ_TN), lambda i, j: (i, j)),
        out_shape=jax.ShapeDtypeStruct((N, PROJ_COLS), BF16),
        scratch_shapes=[pltpu.VMEM((tm, D), BF16)],
        compiler_params=pltpu.CompilerParams(
            dimension_semantics=("parallel", "arbitrary"), vmem_limit_bytes=VMEM_LIMIT),
        name="in_proj",
    )(x2, mod3, g1, w_in_r)


def _gla_kernel(q_ref, k_ref, v_ref, g_ref, a_ref, wa_ref, ba_ref, ng_ref, o_ref, st_ref, *, tc):
    nchunk = tc // CHUNK

    @pl.when(pl.program_id(2) == 0)
    def _():
        st_ref[...] = jnp.zeros_like(st_ref)

    row = lax.broadcasted_iota(jnp.int32, (tc, tc), 0)
    col = lax.broadcasted_iota(jnp.int32, (tc, tc), 1)
    chunk_gap = _chunk_id(row) - _chunk_id(col)
    lower = (chunk_gap == 0) & (col <= row)
    upper = (chunk_gap == 0) & (col > row)
    tri = jnp.where(lower, 1.0, 0.0).astype(BF16)

    def per_chunk(rows):
        return jnp.concatenate([jnp.broadcast_to(r, (CHUNK, GLA_DK)) for r in rows], axis=0)

    def head(hh):
        dk = slice(hh * GLA_DK, (hh + 1) * GLA_DK)
        dv = slice(hh * GLA_DV, (hh + 1) * GLA_DV)
        z = _dot(a_ref[...], wa_ref[:, dk]) + ba_ref[:, dk]
        yield
        log_a = (jnp.minimum(z, 0.0) - jnp.log1p(jnp.exp(-jnp.abs(z)))) * (1.0 / GLA_TAU)
        la_hi, la_lo = _split_bf16(log_a)
        bcum = _dot(tri, la_hi) + _dot(tri, la_lo)
        yield
        b_last = [bcum[c * CHUNK + CHUNK - 1:(c + 1) * CHUNK, :] for c in range(nchunk)]
        prefix = [jnp.zeros_like(b_last[0])]
        for c in range(nchunk):
            prefix.append(prefix[-1] + b_last[c])
        one = jnp.ones_like(b_last[0])

        eb = jnp.exp(bcum)
        enb = jnp.exp(-bcum)
        qs = q_ref[:, dk].astype(F32) * (GLA_DK ** -0.5)
        k = k_ref[:, dk].astype(F32)
        v = v_ref[:, dv]
        q_fwd = qs * eb
        q_f = q_fwd.astype(BF16)
        k_dec = k * jnp.exp(per_chunk(b_last) - bcum)
        k_d = k_dec.astype(BF16)

        s_fwd = _dot_nt(q_f, (k * enb).astype(BF16))
        s_bwd = _dot_nt((qs * enb).astype(BF16), (k * eb).astype(BF16))
        yield
        scores = jnp.where(lower, s_fwd, jnp.where(upper, s_bwd, 0.0))
        for off in range(1, nchunk):
            if off == 1:
                q_x = q_f
            else:
                between = [one] * off + [jnp.exp(prefix[c] - prefix[c - off + 1]) for c in range(off, nchunk)]
                q_x = (q_fwd * per_chunk(between)).astype(BF16)
            scores = jnp.where(chunk_gap == off, _dot_nt(q_x, k_d), scores)
            yield

        st = st_ref[hh]
        q_s = (q_fwd * per_chunk([jnp.exp(p) for p in prefix[:nchunk]])).astype(BF16)
        o = _dot(scores.astype(BF16), v) + _dot_nt(q_s, st.astype(BF16))
        to_end = [jnp.exp(prefix[nchunk] - prefix[c + 1]) for c in range(nchunk)]
        st_ref[hh] = st * jnp.exp(prefix[nchunk]) + _dot_tn(v, (k_dec * per_chunk(to_end)).astype(BF16))
        yield

        ms = jnp.mean(o * o, axis=-1, keepdims=True)
        y = o * lax.rsqrt(ms + EPS) * ng_ref[...]
        g = g_ref[:, dv].astype(F32)
        o_ref[:, dv] = (y * _silu(g)).astype(BF16)

    heads = [head(hh) for hh in range(GLA_HEADS_PER_STEP)]
    while heads:
        heads = [h for h in heads if next(h, True) is None]


def _gla(proj, wa_pad, b_alpha, gla_norm_g, B, S):
    N = proj.shape[0]
    tc = min(256, S)
    nt = S // tc
    hps = GLA_HEADS_PER_STEP
    row = lambda b, h, i: b * nt + i
    return pl.pallas_call(
        functools.partial(_gla_kernel, tc=tc),
        grid=(B, GLA_HEADS // hps, nt),
        in_specs=[
            pl.BlockSpec((tc, hps * GLA_DK), lambda b, h, i: (row(b, h, i), COL_GQ // (hps * GLA_DK) + h)),
            pl.BlockSpec((tc, hps * GLA_DK), lambda b, h, i: (row(b, h, i), COL_GK // (hps * GLA_DK) + h)),
            pl.BlockSpec((tc, hps * GLA_DV), lambda b, h, i: (row(b, h, i), COL_GV // (hps * GLA_DV) + h)),
            pl.BlockSpec((tc, hps * GLA_DV), lambda b, h, i: (row(b, h, i), COL_GG // (hps * GLA_DV) + h)),
            pl.BlockSpec((tc, LANES), lambda b, h, i: (row(b, h, i), COL_ACODE // LANES)),
            pl.BlockSpec((LANES, hps * GLA_DK), lambda b, h, i: (0, h)),
            pl.BlockSpec((1, hps * GLA_DK), lambda b, h, i: (0, h)),
            pl.BlockSpec((1, GLA_DV), lambda b, h, i: (0, 0)),
        ],
        out_specs=pl.BlockSpec((tc, hps * GLA_DV), lambda b, h, i: (row(b, h, i), h)),
        out_shape=jax.ShapeDtypeStruct((N, GLA_HEADS * GLA_DV), BF16),
        scratch_shapes=[pltpu.VMEM((hps, GLA_DV, GLA_DK), F32)],
        compiler_params=pltpu.CompilerParams(
            dimension_semantics=("parallel", "parallel", "arbitrary"), vmem_limit_bytes=VMEM_LIMIT),
        name="gla",
    )(proj, proj, proj, proj, proj, wa_pad, b_alpha, gla_norm_g)


def _group_rms(xf, gsum, gain):
    ss = _dot((xf * xf).astype(BF16), gsum)
    return xf * lax.rsqrt(ss * (1.0 / DIFF_DH) + EPS) * gain


def _diff_attn_kernel(q_ref, k_ref, v_ref, qg_ref, kg_ref, slope_ref, lam_ref, sg_ref, o_ref,
                      ka_scr, vt_scr, wt_scr, acc_scr, s_scr, p_scr, dm_scr, *, t, seq):
    i = pl.program_id(2)
    nqc = t // LANES
    width = nqc * 2 * LANES
    chains = tuple(range(ATTN_CHAINS))
    lane = lax.broadcasted_iota(jnp.int32, (1, LANES), 1)
    first_map = lane < DIFF_DH

    gr = lax.broadcasted_iota(jnp.int32, (LANES, LANES), 0) // DIFF_DH
    gc = lax.broadcasted_iota(jnp.int32, (LANES, LANES), 1) // DIFF_DH
    gsum = jnp.where(gr == gc, 1.0, 0.0).astype(BF16)
    slope2 = slope_ref[...] * LOG2E
    slope2_w = jnp.tile(slope2, (1, width // LANES))

    @pl.when(i == 0)
    def _():
        jl = lax.broadcasted_iota(jnp.int32, (t, LANES), 0).astype(F32)
        lane_t = lax.broadcasted_iota(jnp.int32, (t, LANES), 1)
        rem = slope2 * jl
        aux = jnp.zeros((t, LANES), F32)
        for term in range(BIAS_TERMS):
            part = rem.astype(BF16).astype(F32)
            aux = jnp.where(lane_t == term, part, aux)
            rem = rem - part
        aux = aux.astype(BF16)

        jd = lax.broadcasted_iota(jnp.int32, (t, width), 0)
        col = lax.broadcasted_iota(jnp.int32, (t, width), 1)
        il = (col & (LANES - 1)) + lax.shift_right_logical(col, (2 * LANES).bit_length() - 1) * LANES
        ahead = jnp.minimum(il - jd, 0).astype(F32)
        dm_scr[...] = jnp.where(_chunk_id(jd) <= _chunk_id(il), (2.0 * slope2_w) * ahead, MASK_VALUE)

        def body(r, carry):
            rows = pl.ds(pl.multiple_of(r * t, t), t)
            kn = _group_rms(k_ref[rows, :].astype(F32), gsum, kg_ref[...])
            ka_scr[rows, 0:LANES] = kn.astype(BF16)
            ka_scr[rows, LANES:2 * LANES] = aux
            vt_scr[r] = v_ref[rows, :].astype(F32).T.astype(BF16)
            return carry
        lax.fori_loop(0, seq // t, body, 0, unroll=2)

    qn = _group_rms(q_ref[...].astype(F32), gsum, qg_ref[...]) * (DIFF_DH ** -0.5 * LOG2E)
    ones = jnp.broadcast_to(jnp.where(lane < BIAS_TERMS, 1.0, 0.0), (LANES, LANES))
    for ch in chains:
        for qc in range(nqc):
            r0 = ch * t + qc * LANES
            qq = qn[r0:r0 + LANES, :]
            w = jnp.concatenate(
                [jnp.concatenate([jnp.where(first_map, qq, 0.0), ones], axis=1),
                 jnp.concatenate([jnp.where(first_map, 0.0, qq), ones], axis=1)], axis=0)
            wt_scr[ch, :, qc * 2 * LANES:(qc + 1) * 2 * LANES] = w.T.astype(BF16)
        p_scr[ch, 1] = jnp.zeros(p_scr.shape[2:], BF16)
    acc_scr[...] = jnp.zeros_like(acc_scr)

    def scores(ch, j, slot):
        rows = pl.ds(pl.multiple_of(j * t, t), t)
        st = _dot(ka_scr[rows, :], wt_scr[ch])
        s_scr[ch, slot] = st
        return jnp.max(st, axis=0, keepdims=True)

    def accumulate(ch, j, slot, alpha):
        acc_scr[ch] = alpha * acc_scr[ch] + _dot(vt_scr[jnp.maximum(j, 0)], p_scr[ch, slot])

    def softmax(ch, j, st, m_cur, m_prev, l_prev):
        shift = slope2_w * ((j - (ATTN_CHAINS * i + ch)) * t).astype(F32)
        m_new = jnp.maximum(m_prev, m_cur + shift)
        p = jnp.exp2(st - (m_new - shift))
        alpha = jnp.exp2(m_prev - m_new)
        l_new = alpha * l_prev + jnp.sum(p, axis=0, keepdims=True)
        return p.astype(BF16), alpha, m_new, l_new

    def step(ch, j, slot, carry):
        m, l, m_cur, alpha_prev = carry
        m_next = scores(ch, j + 1, 1 - slot)
        accumulate(ch, j - 1, 1 - slot, alpha_prev)
        p, alpha, m, l = softmax(ch, j, s_scr[ch, slot], m_cur, m, l)
        p_scr[ch, slot] = p
        return m, l, m_next, alpha

    def finish(ch, j, slot, carry):
        m, l, _, alpha_prev = carry
        accumulate(ch, j - 1, 1 - slot, alpha_prev)
        st = s_scr[ch, slot] + dm_scr[...]
        p, alpha, m, l = softmax(ch, j, st, jnp.max(st, axis=0, keepdims=True), m, l)
        on = (alpha * acc_scr[ch] + _dot(vt_scr[j], p)) * (1.0 / l)
        for qc in range(nqc):
            c0 = qc * 2 * LANES
            ot = on[:, c0:c0 + LANES] - lam_ref[...] * on[:, c0 + LANES:c0 + 2 * LANES]
            o = ot.T
            msq = jnp.mean(o * o, axis=-1, keepdims=True)
            y = o * lax.rsqrt(msq + EPS) * sg_ref[...] * (1.0 - LAMBDA_INIT)
            r0 = ch * t + qc * LANES
            o_ref[r0:r0 + LANES, :] = y.astype(BF16)

    def pair(pp, carries):
        carries = list(carries)
        for j, slot in ((2 * pp, 0), (2 * pp + 1, 1)):
            for ch in chains:
                carries[ch] = step(ch, j, slot, carries[ch])
        return tuple(carries)

    init = tuple((jnp.full((1, width), MASK_VALUE, F32), jnp.zeros((1, width), F32),
                  scores(ch, 0, 0), jnp.ones((1, width), F32)) for ch in chains)
    carries = list(lax.fori_loop(0, (ATTN_CHAINS // 2) * i, pair, init))

    for k in chains:
        j = ATTN_CHAINS * i + k
        for ch in chains[k + 1:]:
            carries[ch] = step(ch, j, k % 2, carries[ch])
        finish(k, j, k % 2, carries[k])


def _diff_attn_fixed_kernel(q_ref, k_ref, v_ref, qg_ref, kg_ref, slope_ref, lam_ref, sg_ref, bound_ref,
                            o_ref, ka_scr, vt_scr, wt_scr, acc_scr, p_scr, dm_scr, *, t, seq):
    i = pl.program_id(2)
    nqc = t // LANES
    width = nqc * 2 * LANES
    chains = tuple(range(ATTN_CHAINS))

    gr = lax.broadcasted_iota(jnp.int32, (LANES, LANES), 0) // DIFF_DH
    gc = lax.broadcasted_iota(jnp.int32, (LANES, LANES), 1) // DIFF_DH
    gsum = jnp.where(gr == gc, 1.0, 0.0).astype(BF16)
    slope2 = slope_ref[...] * LOG2E
    slope2_w = jnp.tile(slope2, (1, width // LANES))

    def bf16_terms(value):
        terms, rem = [], value
        for _ in range(BIAS_TERMS):
            terms.append(rem.astype(BF16).astype(F32))
            rem = rem - terms[-1]
        return terms

    @pl.when(i == 0)
    def _():
        jd = lax.broadcasted_iota(jnp.int32, (t, width), 0)
        col = lax.broadcasted_iota(jnp.int32, (t, width), 1)
        il = (col & (LANES - 1)) + lax.shift_right_logical(col, (2 * LANES).bit_length() - 1) * LANES
        ahead = jnp.minimum(il - jd, 0).astype(F32)
        dm_scr[...] = jnp.where(_chunk_id(jd) <= _chunk_id(il), (2.0 * slope2_w) * ahead, MASK_VALUE)

        lane_t = lax.broadcasted_iota(jnp.int32, (t, LANES), 1)
        row_t = lax.broadcasted_iota(jnp.int32, (t, LANES), 0).astype(F32)
        in_tile = jnp.where((lane_t >= 2 * BIAS_TERMS) & (lane_t < 3 * BIAS_TERMS), 1.0, 0.0)
        for n, term in enumerate(bf16_terms(slope2 * row_t)):
            in_tile = jnp.where(lane_t == n, term, in_tile)
        lane_1 = lax.broadcasted_iota(jnp.int32, (1, LANES), 1)

        def body(r, carry):
            rows = pl.ds(pl.multiple_of(r * t, t), t)
            kn = _group_rms(k_ref[rows, :].astype(F32), gsum, kg_ref[...])
            ka_scr[rows, 0:LANES] = kn.astype(BF16)
            start = jnp.zeros((1, LANES), F32)
            for n, term in enumerate(bf16_terms(slope2 * jnp.asarray(r * t).astype(F32))):
                start = jnp.where(lane_1 == BIAS_TERMS + n, term, start)
            tile_lanes = (lane_t >= BIAS_TERMS) & (lane_t < 2 * BIAS_TERMS)
            ka_scr[rows, LANES:2 * LANES] = jnp.where(tile_lanes, start, in_tile).astype(BF16)
            vt_scr[r] = v_ref[rows, :].astype(F32).T.astype(BF16)
            return carry
        lax.fori_loop(0, seq // t, body, 0, unroll=2)

    qn = _group_rms(q_ref[...].astype(F32), gsum, qg_ref[...]) * (DIFF_DH ** -0.5 * LOG2E)
    row_q = lax.broadcasted_iota(jnp.int32, (LANES, LANES), 0)
    lane_q = lax.broadcasted_iota(jnp.int32, (1, LANES), 1)
    for ch in chains:
        for qc in range(nqc):
            r0 = ch * t + qc * LANES
            qt = qn[r0:r0 + LANES, :].T
            q_pos = (lane_q + (i * (ATTN_CHAINS * t) + r0)).astype(F32)
            side = jnp.where(row_q < 2 * BIAS_TERMS, 1.0, 0.0)
            for n, term in enumerate(bf16_terms(-(slope2 * q_pos + bound_ref[...]))):
                side = jnp.where(row_q == 2 * BIAS_TERMS + n, term, side)
            side = side.astype(BF16)
            c0 = qc * 2 * LANES
            wt_scr[ch, 0:LANES, c0:c0 + LANES] = jnp.where(row_q < DIFF_DH, qt, 0.0).astype(BF16)
            wt_scr[ch, 0:LANES, c0 + LANES:c0 + 2 * LANES] = jnp.where(row_q < DIFF_DH, 0.0, qt).astype(BF16)
            wt_scr[ch, LANES:2 * LANES, c0:c0 + LANES] = side
            wt_scr[ch, LANES:2 * LANES, c0 + LANES:c0 + 2 * LANES] = side
    acc_scr[...] = jnp.zeros_like(acc_scr)

    def probs(ch, j, slot, l_part, diagonal=False):
        rows = pl.ds(pl.multiple_of(j * t, t), t)
        st = _dot(ka_scr[rows, :], wt_scr[ch])
        if diagonal:
            st = st + dm_scr[...]
        p = jnp.exp2(st)
        p_scr[ch, slot] = p.astype(BF16)
        return l_part + jnp.sum(p.reshape(t // SUBLANES, SUBLANES, width), axis=0)

    def accumulate(ch, j, slot):
        acc_scr[ch] += _dot(vt_scr[j], p_scr[ch, slot])

    def tile(j, slot, parts, first_chain=0, diagonal=False):
        parts = list(parts)
        for ch in chains[first_chain:]:
            parts[ch] = probs(ch, j, slot, parts[ch], diagonal=diagonal and ch == first_chain)
        for ch in chains[first_chain:]:
            accumulate(ch, j, slot)
        return parts

    def finish(ch, l_part):
        on = acc_scr[ch] * (1.0 / jnp.sum(l_part, axis=0, keepdims=True))
        for qc in range(nqc):
            c0 = qc * 2 * LANES
            ot = on[:, c0:c0 + LANES] - lam_ref[...] * on[:, c0 + LANES:c0 + 2 * LANES]
            o = ot.T
            msq = jnp.mean(o * o, axis=-1, keepdims=True)
            y = o * lax.rsqrt(msq + EPS) * sg_ref[...] * (1.0 - LAMBDA_INIT)
            r0 = ch * t + qc * LANES
            o_ref[r0:r0 + LANES, :] = y.astype(BF16)

    parts = lax.fori_loop(
        0, (ATTN_CHAINS // 2) * i,
        lambda pp, parts: tuple(tile(2 * pp + 1, 1, tile(2 * pp, 0, parts))),
        tuple(jnp.zeros((SUBLANES, width), F32) for _ in chains))

    for k in chains:
        parts = tile(ATTN_CHAINS * i + k, k % 2, parts, first_chain=k, diagonal=True)
        finish(k, parts[k])


def _diff_attn(proj, qg2, kg2, slopes, lam, subln_g, B, S, bound=None):
    N = proj.shape[0]
    t = min(256, S // ATTN_CHAINS)
    tq = ATTN_CHAINS * t
    nq = S // tq
    vec = pl.BlockSpec((1, LANES), lambda b, h, i: (0, 0))
    body = _diff_attn_kernel if bound is None else _diff_attn_fixed_kernel
    extra = () if bound is None else (bound,)
    return pl.pallas_call(
        functools.partial(body, t=t, seq=S),
        grid=(B, DIFF_HEADS, nq),
        in_specs=[
            pl.BlockSpec((tq, LANES), lambda b, h, i: (b * nq + i, COL_DQ // LANES + h)),
            pl.BlockSpec((S, LANES), lambda b, h, i: (b, COL_DK // LANES + h)),
            pl.BlockSpec((S, LANES), lambda b, h, i: (b, COL_DV // LANES + h)),
            vec, vec,
            pl.BlockSpec((None, 1, LANES), lambda b, h, i: (h, 0, 0)),
            vec, vec,
        ] + [vec] * len(extra),
        out_specs=pl.BlockSpec((tq, DIFF_DV), lambda b, h, i: (b * nq + i, h)),
        out_shape=jax.ShapeDtypeStruct((N, DIFF_HEADS * DIFF_DV), BF16),
        scratch_shapes=[
            pltpu.VMEM((S, 2 * LANES), BF16),
            pltpu.VMEM((S // t, DIFF_DV, t), BF16),
            pltpu.VMEM((ATTN_CHAINS, 2 * LANES, 2 * t), BF16),
            pltpu.VMEM((ATTN_CHAINS, DIFF_DV, 2 * t), F32),
        ] + ([pltpu.VMEM((ATTN_CHAINS, 2, t, 2 * t), F32)] if bound is None else []) + [
            pltpu.VMEM((ATTN_CHAINS, 2, t, 2 * t), BF16),
            pltpu.VMEM((t, 2 * t), F32),
        ],
        compiler_params=pltpu.CompilerParams(
            dimension_semantics=("parallel", "parallel", "arbitrary"), vmem_limit_bytes=VMEM_LIMIT),
        name="diff_attn" if bound is None else "diff_attn_fixed",
    )(proj, proj, proj, qg2, kg2, slopes, lam, subln_g, *extra)


def _merge_kernel(oa_ref, ob_ref, ga_ref, gb_ref, x_ref, mod_ref, wa_ref, wb_ref, wo_ref, g2_ref,
                  x1_ref, h2_ref):
    ya = _dot(oa_ref[...], wa_ref[...])
    yb = _dot(ob_ref[...], wb_ref[...])
    merged = _sigmoid(ga_ref[...].astype(F32)) * ya + _sigmoid(gb_ref[...].astype(F32)) * yb
    x1 = x_ref[...] + mod_ref[2:3, :] * _dot(merged.astype(BF16), wo_ref[...])
    x1_ref[...] = x1
    ms = jnp.mean(x1 * x1, axis=-1, keepdims=True)
    y = x1 * lax.rsqrt(ms + EPS) * g2_ref[...]
    h2_ref[...] = (y * (1.0 + mod_ref[4:5, :]) + mod_ref[3:4, :]).astype(BF16)


def _merge(o_a, o_b, proj, x2, mod3, w_gla_o, w_diff_o, w_out, g2, S):
    N, D = x2.shape
    tm = min(512, S)
    tiles_per_seq = S // tm
    tok = lambda col: pl.BlockSpec((tm, D), lambda i: (i, col))
    wspec = pl.BlockSpec((D, D), lambda i: (0, 0))
    return pl.pallas_call(
        _merge_kernel,
        grid=(N // tm,),
        in_specs=[
            tok(0), tok(0), tok(COL_GA // D), tok(COL_GB // D), tok(0),
            pl.BlockSpec((None, 6, D), lambda i: (i // tiles_per_seq, 0, 0)),
            wspec, wspec, wspec,
            pl.BlockSpec((1, D), lambda i: (0, 0)),
        ],
        out_specs=[tok(0), tok(0)],
        out_shape=[jax.ShapeDtypeStruct((N, D), F32), jax.ShapeDtypeStruct((N, D), BF16)],
        compiler_params=pltpu.CompilerParams(
            dimension_semantics=("parallel",), vmem_limit_bytes=VMEM_LIMIT),
        name="merge",
    )(o_a, o_b, proj, proj, x2, mod3, w_gla_o, w_diff_o, w_out, g2)


def _conv_ffn_kernel(h2_ref, x1_ref, mod_ref, wup_ref, cw_ref, cb_ref, wdn_ref, o_ref,
                     carry_ref, acc_ref, ubuf_ref, act_ref, *, tm, tf, tiles_per_seq):
    @pl.when(pl.program_id(0) % tiles_per_seq == 0)
    def _():
        carry_ref[...] = jnp.zeros_like(carry_ref)

    h2 = h2_ref[...]

    def up(f):
        return (_dot(h2, wup_ref[:, f * tf:(f + 1) * tf]),
                _dot(h2, wup_ref[:, D_FF + f * tf:D_FF + (f + 1) * tf]))

    def conv(u, col0, slot, half):
        cols = slice(half * tf, (half + 1) * tf)
        ubuf_ref[slot, 0:SUBLANES, cols] = carry_ref[:, col0:col0 + tf]
        ubuf_ref[slot, SUBLANES:, cols] = u
        carry_ref[:, col0:col0 + tf] = u[tm - SUBLANES:, :]
        r1 = ubuf_ref[slot, SUBLANES - 1:SUBLANES - 1 + tm, cols]
        r2 = ubuf_ref[slot, SUBLANES - 2:SUBLANES - 2 + tm, cols]
        cw = cw_ref[:, col0:col0 + tf]
        return cw[0:1, :] * r2 + cw[1:2, :] * r1 + cw[2:3, :] * u + cb_ref[:, col0:col0 + tf]

    nf = D_FF // tf
    u_next = up(0)
    for f in range(nf):
        ua, ub = u_next
        if f + 1 < nf:
            u_next = up(f + 1)
        a = conv(ua, f * tf, f % 2, 0)
        b = conv(ub, D_FF + f * tf, f % 2, 1)
        act_ref[:, (f % DOWN_GROUP) * tf:(f % DOWN_GROUP + 1) * tf] = (_silu(a) * b).astype(BF16)
        if (f + 1) % DOWN_GROUP == 0 or f + 1 == nf:
            f0 = f - f % DOWN_GROUP
            contrib = _dot(act_ref[:, :(f + 1 - f0) * tf], wdn_ref[f0 * tf:(f + 1) * tf, :])
            if f0 == 0:
                acc_ref[...] = contrib
            else:
                acc_ref[...] += contrib

    o_ref[...] = x1_ref[...] + mod_ref[5:6, :] * acc_ref[...]


def _conv_ffn(h2, x1, mod3, w_up, conv_w, conv_b, w_down, S):
    N, D = x1.shape
    tm = min(256, S)
    tf = 256
    tiles_per_seq = S // tm
    tok = pl.BlockSpec((tm, D), lambda i: (i, 0))
    full = lambda a: pl.BlockSpec(a.shape, lambda i: (0, 0))
    return pl.pallas_call(
        functools.partial(_conv_ffn_kernel, tm=tm, tf=tf, tiles_per_seq=tiles_per_seq),
        grid=(N // tm,),
        in_specs=[
            tok, tok,
            pl.BlockSpec((None, 6, D), lambda i: (i // tiles_per_seq, 0, 0)),
            full(w_up), full(conv_w), full(conv_b), full(w_down),
        ],
        out_specs=tok,
        out_shape=jax.ShapeDtypeStruct((N, D), F32),
        scratch_shapes=[
            pltpu.VMEM((SUBLANES, 2 * D_FF), F32),
            pltpu.VMEM((tm, D), F32),
            pltpu.VMEM((2, SUBLANES + tm, 2 * tf), F32),
            pltpu.VMEM((tm, DOWN_GROUP * tf), BF16),
        ],
        compiler_params=pltpu.CompilerParams(
            dimension_semantics=("arbitrary",), vmem_limit_bytes=VMEM_LIMIT),
        name="conv_ffn",
    )(h2, x1, mod3, w_up, conv_w, conv_b, w_down)


def kernel(x, c, w_ada, b_ada, norm1_g, w_in, w_alpha_up, b_alpha, gla_norm_g, q_norm_g, k_norm_g,
           lam_q1, lam_k1, lam_q2, lam_k2, diff_norm_g, w_gla_o, w_diff_o, w_out, norm2_g, w_up,
           conv_w, conv_b, w_down):
    B, S, D = x.shape
    N = B * S
    assert D == D_MODEL and S % CHUNK == 0 and w_ada.shape[0] == 1

    w0 = w_in[0]
    a0 = COL_GG + GLA_HEADS * GLA_DV
    w_in_r = jnp.concatenate(
        [w0[:, :a0], w0[:, a0 + GLA_RANK:],
         jnp.pad(w0[:, a0:a0 + GLA_RANK], ((0, 0), (0, PROJ_COLS - COL_ACODE - GLA_RANK)))], axis=1).astype(BF16)
    wa_pad = jnp.pad(w_alpha_up[0], ((0, LANES - GLA_RANK), (0, 0))).astype(BF16)
    slopes = 2.0 ** (-8.0 * (jnp.arange(DIFF_HEADS, dtype=F32) + 1.0) / DIFF_HEADS)
    slopes = jnp.broadcast_to(slopes[:, None, None], (DIFF_HEADS, 1, LANES))
    qg2 = jnp.tile(q_norm_g, (1, 2))
    kg2 = jnp.tile(k_norm_g, (1, 2))

    mod, lam = _ada_mod(c, w_ada[0], b_ada, lam_q1, lam_k1, lam_q2, lam_k2)
    mod3 = mod.reshape(B, 6, D)
    x2 = x.reshape(N, D)

    proj = _in_proj(x2, mod3, norm1_g, w_in_r, S)
    o_a = _gla(proj, wa_pad, b_alpha, gla_norm_g, B, S)
    bound = (LOGIT_BOUND_SLACK * DIFF_DH ** 0.5 * LOG2E) * jnp.max(jnp.abs(q_norm_g)) * jnp.max(jnp.abs(k_norm_g))
    o_b = lax.cond(
        bound <= MAX_FIXED_BOUND,
        lambda: _diff_attn(proj, qg2, kg2, slopes, lam, diff_norm_g, B, S,
                           bound=jnp.broadcast_to(bound, (1, LANES)).astype(F32)),
        lambda: _diff_attn(proj, qg2, kg2, slopes, lam, diff_norm_g, B, S))
    x1, h2 = _merge(o_a, o_b, proj, x2, mod3, w_gla_o[0].astype(BF16), w_diff_o[0].astype(BF16),
                    w_out[0].astype(BF16), norm2_g, S)
    out = _conv_ffn(h2, x1, mod3, w_up[0].astype(BF16), conv_w[0], conv_b, w_down[0].astype(BF16), S)
    return out.reshape(B, S, D)
```

```python
import functools
import math

import jax
import jax.numpy as jnp
from jax import lax
from jax.experimental import pallas as pl
from jax.experimental.pallas import tpu as pltpu

F32 = jnp.float32
BF16 = jnp.bfloat16

D_MODEL = 1024
CHUNK = 64
EPS = 1e-6
GLA_HEADS = 4
GLA_DK = 128
GLA_DV = 256
GLA_RANK = 16
GLA_TAU = 16.0
GLA_HEADS_PER_STEP = 4
DIFF_HEADS = 8
DIFF_DH = 64
DIFF_DV = 128
D_FF = 2816
CONV_W = 3
DOWN_GROUP = 4
LAMBDA_INIT = 0.8 - 0.6 * math.exp(-0.3 * 0)

LANES = 128
SUBLANES = 8
MASK_VALUE = -1e30
LOG2E = math.log2(math.e)
BIAS_TERMS = 3
ATTN_CHAINS = 8
LOGIT_BOUND_SLACK = 1.02
MAX_FIXED_BOUND = 50.0

COL_GQ, COL_GK, COL_GV, COL_GG = 0, 512, 1024, 2048
COL_DQ, COL_DK, COL_DV = 3072, 4096, 5120
COL_GA, COL_GB, COL_ACODE = 6144, 7168, 8192
PROJ_COLS = 8192 + 2 * LANES
PROJ_TN = PROJ_COLS // 3

VMEM_LIMIT = 56 * 1024 * 1024


def _dot(a, b):
    return jnp.dot(a, b, preferred_element_type=F32)


def _dot_nt(a, b):
    return lax.dot_general(a, b, (((1,), (1,)), ((), ())), preferred_element_type=F32)


def _dot_tn(a, b):
    return lax.dot_general(a, b, (((0,), (0,)), ((), ())), preferred_element_type=F32)


def _split_bf16(v):
    hi = v.astype(BF16)
    lo = (v - hi.astype(F32)).astype(BF16)
    return hi, lo


def _sigmoid(v):
    return 0.5 + 0.5 * jnp.tanh(0.5 * v)


def _silu(v):
    h = 0.5 * v
    return h + h * jnp.tanh(h)


def _chunk_id(pos):
    return lax.shift_right_logical(pos, CHUNK.bit_length() - 1)


def _ada_kernel(c_ref, w_ref, b_ref, lq1_ref, lk1_ref, lq2_ref, lk2_ref, mod_ref, lam_ref):
    c = c_ref[...]
    a = c * _sigmoid(c)
    a_hi, a_lo = _split_bf16(a)
    w_hi, w_lo = _split_bf16(w_ref[...])
    mod_ref[...] = _dot(a_hi, w_hi) + _dot(a_lo, w_hi) + _dot(a_hi, w_lo) + b_ref[...]
    s1 = jnp.sum(lq1_ref[...] * lk1_ref[...], axis=-1, keepdims=True)
    s2 = jnp.sum(lq2_ref[...] * lk2_ref[...], axis=-1, keepdims=True)
    lam = jnp.exp(s1) - jnp.exp(s2) + LAMBDA_INIT
    lam_ref[...] = jnp.broadcast_to(lam, lam_ref.shape)


def _ada_mod(c, w_ada, b_ada, lq1, lk1, lq2, lk2):
    B, D = c.shape
    n_out = w_ada.shape[1]
    tn = D
    small = pl.BlockSpec((1, DIFF_DH), lambda j: (0, 0))
    return pl.pallas_call(
        _ada_kernel,
        grid=(n_out // tn,),
        in_specs=[
            pl.BlockSpec((B, D), lambda j: (0, 0)),
            pl.BlockSpec((D, tn), lambda j: (0, j)),
            pl.BlockSpec((1, tn), lambda j: (0, j)),
            small, small, small, small,
        ],
        out_specs=[
            pl.BlockSpec((B, tn), lambda j: (0, j)),
            pl.BlockSpec((1, LANES), lambda j: (0, 0)),
        ],
        out_shape=[
            jax.ShapeDtypeStruct((B, n_out), F32),
            jax.ShapeDtypeStruct((1, LANES), F32),
        ],
        compiler_params=pltpu.CompilerParams(dimension_semantics=("arbitrary",)),
        name="ada_mod",
    )(c, w_ada, b_ada, lq1, lk1, lq2, lk2)


def _in_proj_kernel(x_ref, mod_ref, g_ref, w_ref, o_ref, h_scr):
    @pl.when(pl.program_id(1) == 0)
    def _():
        x = x_ref[...]
        ms = jnp.mean(x * x, axis=-1, keepdims=True)
        y = x * lax.rsqrt(ms + EPS) * g_ref[...]
        h = y * (1.0 + mod_ref[1:2, :]) + mod_ref[0:1, :]
        h_scr[...] = h.astype(BF16)

    o_ref[...] = _dot(h_scr[...], w_ref[...]).astype(BF16)


def _in_proj(x2, mod3, g1, w_in_r, S):
    N, D = x2.shape
    tm = min(1024, S)
    tiles_per_seq = S // tm
    return pl.pallas_call(
        _in_proj_kernel,
        grid=(N // tm, PROJ_COLS // PROJ_TN),
        in_specs=[
            pl.BlockSpec((tm, D), lambda i, j: (i, 0)),
            pl.BlockSpec((None, 6, D), lambda i, j: (i // tiles_per_seq, 0, 0)),
            pl.BlockSpec((1, D), lambda i, j: (0, 0)),
            pl.BlockSpec((D, PROJ_TN), lambda i, j: (0, j)),
        ],
        out_specs=pl.BlockSpec((tm, PROJ_TN), lambda i, j: (i, j)),
        out_shape=jax.ShapeDtypeStruct((N, PROJ_COLS), BF16),
        scratch_shapes=[pltpu.VMEM((tm, D), BF16)],
        compiler_params=pltpu.CompilerParams(
            dimension_semantics=("parallel", "arbitrary"), vmem_limit_bytes=VMEM_LIMIT),
        name="in_proj",
    )(x2, mod3, g1, w_in_r)


def _gla_kernel(q_ref, k_ref, v_ref, g_ref, a_ref, wa_ref, ba_ref, ng_ref, o_ref, st_ref, *, tc):
    nchunk = tc // CHUNK

    @pl.when(pl.program_id(2) == 0)
    def _():
        st_ref[...] = jnp.zeros_like(st_ref)

    row = lax.broadcasted_iota(jnp.int32, (tc, tc), 0)
    col = lax.broadcasted_iota(jnp.int32, (tc, tc), 1)
    chunk_gap = _chunk_id(row) - _chunk_id(col)
    lower = (chunk_gap == 0) & (col <= row)
    upper = (chunk_gap == 0) & (col > row)
    tri = jnp.where(lower, 1.0, 0.0).astype(BF16)

    def per_chunk(rows):
        return jnp.concatenate([jnp.broadcast_to(r, (CHUNK, GLA_DK)) for r in rows], axis=0)

    def head(hh):
        dk = slice(hh * GLA_DK, (hh + 1) * GLA_DK)
        dv = slice(hh * GLA_DV, (hh + 1) * GLA_DV)
        z = _dot(a_ref[...], wa_ref[:, dk]) + ba_ref[:, dk]
        yield
        log_a = (jnp.minimum(z, 0.0) - jnp.log1p(jnp.exp(-jnp.abs(z)))) * (1.0 / GLA_TAU)
        la_hi, la_lo = _split_bf16(log_a)
        bcum = _dot(tri, la_hi) + _dot(tri, la_lo)
        yield
        b_last = [bcum[c * CHUNK + CHUNK - 1:(c + 1) * CHUNK, :] for c in range(nchunk)]
        prefix = [jnp.zeros_like(b_last[0])]
        for c in range(nchunk):
            prefix.append(prefix[-1] + b_last[c])
        one = jnp.ones_like(b_last[0])

        eb = jnp.exp(bcum)
        enb = jnp.exp(-bcum)
        qs = q_ref[:, dk].astype(F32) * (GLA_DK ** -0.5)
        k = k_ref[:, dk].astype(F32)
        v = v_ref[:, dv]
        q_fwd = qs * eb
        q_f = q_fwd.astype(BF16)
        k_dec = k * jnp.exp(per_chunk(b_last) - bcum)
        k_d = k_dec.astype(BF16)

        s_fwd = _dot_nt(q_f, (k * enb).astype(BF16))
        s_bwd = _dot_nt((qs * enb).astype(BF16), (k * eb).astype(BF16))
        yield
        scores = jnp.where(lower, s_fwd, jnp.where(upper, s_bwd, 0.0))
        for off in range(1, nchunk):
            if off == 1:
                q_x = q_f
            else:
                between = [one] * off + [jnp.exp(prefix[c] - prefix[c - off + 1]) for c in range(off, nchunk)]
                q_x = (q_fwd * per_chunk(between)).astype(BF16)
            scores = jnp.where(chunk_gap == off, _dot_nt(q_x, k_d), scores)
            yield

        st = st_ref[hh]
        q_s = (q_fwd * per_chunk([jnp.exp(p) for p in prefix[:nchunk]])).astype(BF16)
        o = _dot(scores.astype(BF16), v) + _dot_nt(q_s, st.astype(BF16))
        to_end = [jnp.exp(prefix[nchunk] - prefix[c + 1]) for c in range(nchunk)]
        st_ref[hh] = st * jnp.exp(prefix[nchunk]) + _dot_tn(v, (k_dec * per_chunk(to_end)).astype(BF16))
        yield

        ms = jnp.mean(o * o, axis=-1, keepdims=True)
        y = o * lax.rsqrt(ms + EPS) * ng_ref[...]
        g = g_ref[:, dv].astype(F32)
        o_ref[:, dv] = (y * _silu(g)).astype(BF16)

    heads = [head(hh) for hh in range(GLA_HEADS_PER_STEP)]
    while heads:
        heads = [h for h in heads if next(h, True) is None]


def _gla(proj, wa_pad, b_alpha, gla_norm_g, B, S):
    N = proj.shape[0]
    tc = min(256, S)
    nt = S // tc
    hps = GLA_HEADS_PER_STEP
    row = lambda b, h, i: b * nt + i
    return pl.pallas_call(
        functools.partial(_gla_kernel, tc=tc),
        grid=(B, GLA_HEADS // hps, nt),
        in_specs=[
            pl.BlockSpec((tc, hps * GLA_DK), lambda b, h, i: (row(b, h, i), COL_GQ // (hps * GLA_DK) + h)),
            pl.BlockSpec((tc, hps * GLA_DK), lambda b, h, i: (row(b, h, i), COL_GK // (hps * GLA_DK) + h)),
            pl.BlockSpec((tc, hps * GLA_DV), lambda b, h, i: (row(b, h, i), COL_GV // (hps * GLA_DV) + h)),
            pl.BlockSpec((tc, hps * GLA_DV), lambda b, h, i: (row(b, h, i), COL_GG // (hps * GLA_DV) + h)),
            pl.BlockSpec((tc, LANES), lambda b, h, i: (row(b, h, i), COL_ACODE // LANES)),
            pl.BlockSpec((LANES, hps * GLA_DK), lambda b, h, i: (0, h)),
            pl.BlockSpec((1, hps * GLA_DK), lambda b, h, i: (0, h)),
            pl.BlockSpec((1, GLA_DV), lambda b, h, i: (0, 0)),
        ],
        out_specs=pl.BlockSpec((tc, hps * GLA_DV), lambda b, h, i: (row(b, h, i), h)),
        out_shape=jax.ShapeDtypeStruct((N, GLA_HEADS * GLA_DV), BF16),
        scratch_shapes=[pltpu.VMEM((hps, GLA_DV, GLA_DK), F32)],
        compiler_params=pltpu.CompilerParams(
            dimension_semantics=("parallel", "parallel", "arbitrary"), vmem_limit_bytes=VMEM_LIMIT),
        name="gla",
    )(proj, proj, proj, proj, proj, wa_pad, b_alpha, gla_norm_g)


def _group_rms(xf, gsum, gain):
    ss = _dot((xf * xf).astype(BF16), gsum)
    return xf * lax.rsqrt(ss * (1.0 / DIFF_DH) + EPS) * gain


def _diff_attn_kernel(q_ref, k_ref, v_ref, qg_ref, kg_ref, slope_ref, lam_ref, sg_ref, o_ref,
                      ka_scr, vt_scr, wt_scr, acc_scr, s_scr, p_scr, dm_scr, *, t, seq):
    i = pl.program_id(2)
    nqc = t // LANES
    width = nqc * 2 * LANES
    chains = tuple(range(ATTN_CHAINS))
    lane = lax.broadcasted_iota(jnp.int32, (1, LANES), 1)
    first_map = lane < DIFF_DH

    gr = lax.broadcasted_iota(jnp.int32, (LANES, LANES), 0) // DIFF_DH
    gc = lax.broadcasted_iota(jnp.int32, (LANES, LANES), 1) // DIFF_DH
    gsum = jnp.where(gr == gc, 1.0, 0.0).astype(BF16)
    slope2 = slope_ref[...] * LOG2E
    slope2_w = jnp.tile(slope2, (1, width // LANES))

    @pl.when(i == 0)
    def _():
        jl = lax.broadcasted_iota(jnp.int32, (t, LANES), 0).astype(F32)
        lane_t = lax.broadcasted_iota(jnp.int32, (t, LANES), 1)
        rem = slope2 * jl
        aux = jnp.zeros((t, LANES), F32)
        for term in range(BIAS_TERMS):
            part = rem.astype(BF16).astype(F32)
            aux = jnp.where(lane_t == term, part, aux)
            rem = rem - part
        aux = aux.astype(BF16)

        jd = lax.broadcasted_iota(jnp.int32, (t, width), 0)
        col = lax.broadcasted_iota(jnp.int32, (t, width), 1)
        il = (col & (LANES - 1)) + lax.shift_right_logical(col, (2 * LANES).bit_length() - 1) * LANES
        ahead = jnp.minimum(il - jd, 0).astype(F32)
        dm_scr[...] = jnp.where(_chunk_id(jd) <= _chunk_id(il), (2.0 * slope2_w) * ahead, MASK_VALUE)

        def body(r, carry):
            rows = pl.ds(pl.multiple_of(r * t, t), t)
            kn = _group_rms(k_ref[rows, :].astype(F32), gsum, kg_ref[...])
            ka_scr[rows, 0:LANES] = kn.astype(BF16)
            ka_scr[rows, LANES:2 * LANES] = aux
            vt_scr[r] = v_ref[rows, :].astype(F32).T.astype(BF16)
            return carry
        lax.fori_loop(0, seq // t, body, 0, unroll=2)

    qn = _group_rms(q_ref[...].astype(F32), gsum, qg_ref[...]) * (DIFF_DH ** -0.5 * LOG2E)
    ones = jnp.broadcast_to(jnp.where(lane < BIAS_TERMS, 1.0, 0.0), (LANES, LANES))
    for ch in chains:
        for qc in range(nqc):
            r0 = ch * t + qc * LANES
            qq = qn[r0:r0 + LANES, :]
            w = jnp.concatenate(
                [jnp.concatenate([jnp.where(first_map, qq, 0.0), ones], axis=1),
                 jnp.concatenate([jnp.where(first_map, 0.0, qq), ones], axis=1)], axis=0)
            wt_scr[ch, :, qc * 2 * LANES:(qc + 1) * 2 * LANES] = w.T.astype(BF16)
        p_scr[ch, 1] = jnp.zeros(p_scr.shape[2:], BF16)
    acc_scr[...] = jnp.zeros_like(acc_scr)

    def scores(ch, j, slot):
        rows = pl.ds(pl.multiple_of(j * t, t), t)
        st = _dot(ka_scr[rows, :], wt_scr[ch])
        s_scr[ch, slot] = st
        return jnp.max(st, axis=0, keepdims=True)

    def accumulate(ch, j, slot, alpha):
        acc_scr[ch] = alpha * acc_scr[ch] + _dot(vt_scr[jnp.maximum(j, 0)], p_scr[ch, slot])

    def softmax(ch, j, st, m_cur, m_prev, l_prev):
        shift = slope2_w * ((j - (ATTN_CHAINS * i + ch)) * t).astype(F32)
        m_new = jnp.maximum(m_prev, m_cur + shift)
        p = jnp.exp2(st - (m_new - shift))
        alpha = jnp.exp2(m_prev - m_new)
        l_new = alpha * l_prev + jnp.sum(p, axis=0, keepdims=True)
        return p.astype(BF16), alpha, m_new, l_new

    def step(ch, j, slot, carry):
        m, l, m_cur, alpha_prev = carry
        m_next = scores(ch, j + 1, 1 - slot)
        accumulate(ch, j - 1, 1 - slot, alpha_prev)
        p, alpha, m, l = softmax(ch, j, s_scr[ch, slot], m_cur, m, l)
        p_scr[ch, slot] = p
        return m, l, m_next, alpha

    def finish(ch, j, slot, carry):
        m, l, _, alpha_prev = carry
        accumulate(ch, j - 1, 1 - slot, alpha_prev)
        st = s_scr[ch, slot] + dm_scr[...]
        p, alpha, m, l = softmax(ch, j, st, jnp.max(st, axis=0, keepdims=True), m, l)
        on = (alpha * acc_scr[ch] + _dot(vt_scr[j], p)) * (1.0 / l)
        for qc in range(nqc):
            c0 = qc * 2 * LANES
            ot = on[:, c0:c0 + LANES] - lam_ref[...] * on[:, c0 + LANES:c0 + 2 * LANES]
            o = ot.T
            msq = jnp.mean(o * o, axis=-1, keepdims=True)
            y = o * lax.rsqrt(msq + EPS) * sg_ref[...] * (1.0 - LAMBDA_INIT)
            r0 = ch * t + qc * LANES
            o_ref[r0:r0 + LANES, :] = y.astype(BF16)

    def pair(pp, carries):
        carries = list(carries)
        for j, slot in ((2 * pp, 0), (2 * pp + 1, 1)):
            for ch in chains:
                carries[ch] = step(ch, j, slot, carries[ch])
        return tuple(carries)

    init = tuple((jnp.full((1, width), MASK_VALUE, F32), jnp.zeros((1, width), F32),
                  scores(ch, 0, 0), jnp.ones((1, width), F32)) for ch in chains)
    carries = list(lax.fori_loop(0, (ATTN_CHAINS // 2) * i, pair, init))

    for k in chains:
        j = ATTN_CHAINS * i + k
        for ch in chains[k + 1:]:
            carries[ch] = step(ch, j, k % 2, carries[ch])
        finish(k, j, k % 2, carries[k])


def _diff_attn_fixed_kernel(q_ref, k_ref, v_ref, qg_ref, kg_ref, slope_ref, lam_ref, sg_ref, bound_ref,
                            o_ref, ka_scr, vt_scr, wt_scr, acc_scr, p_scr, dm_scr, *, t, seq):
    i = pl.program_id(2)
    nqc = t // LANES
    width = nqc * 2 * LANES
    chains = tuple(range(ATTN_CHAINS))

    gr = lax.broadcasted_iota(jnp.int32, (LANES, LANES), 0) // DIFF_DH
    gc = lax.broadcasted_iota(jnp.int32, (LANES, LANES), 1) // DIFF_DH
    gsum = jnp.where(gr == gc, 1.0, 0.0).astype(BF16)
    slope2 = slope_ref[...] * LOG2E
    slope2_w = jnp.tile(slope2, (1, width // LANES))

    def bf16_terms(value):
        terms, rem = [], value
        for _ in range(BIAS_TERMS):
            terms.append(rem.astype(BF16).astype(F32))
            rem = rem - terms[-1]
        return terms

    @pl.when(i == 0)
    def _():
        jd = lax.broadcasted_iota(jnp.int32, (t, width), 0)
        col = lax.broadcasted_iota(jnp.int32, (t, width), 1)
        il = (col & (LANES - 1)) + lax.shift_right_logical(col, (2 * LANES).bit_length() - 1) * LANES
        ahead = jnp.minimum(il - jd, 0).astype(F32)
        dm_scr[...] = jnp.where(_chunk_id(jd) <= _chunk_id(il), (2.0 * slope2_w) * ahead, MASK_VALUE)

        lane_t = lax.broadcasted_iota(jnp.int32, (t, LANES), 1)
        row_t = lax.broadcasted_iota(jnp.int32, (t, LANES), 0).astype(F32)
        in_tile = jnp.where((lane_t >= 2 * BIAS_TERMS) & (lane_t < 3 * BIAS_TERMS), 1.0, 0.0)
        for n, term in enumerate(bf16_terms(slope2 * row_t)):
            in_tile = jnp.where(lane_t == n, term, in_tile)
        lane_1 = lax.broadcasted_iota(jnp.int32, (1, LANES), 1)

        def body(r, carry):
            rows = pl.ds(pl.multiple_of(r * t, t), t)
            kn = _group_rms(k_ref[rows, :].astype(F32), gsum, kg_ref[...])
            ka_scr[rows, 0:LANES] = kn.astype(BF16)
            start = jnp.zeros((1, LANES), F32)
            for n, term in enumerate(bf16_terms(slope2 * jnp.asarray(r * t).astype(F32))):
                start = jnp.where(lane_1 == BIAS_TERMS + n, term, start)
            tile_lanes = (lane_t >= BIAS_TERMS) & (lane_t < 2 * BIAS_TERMS)
            ka_scr[rows, LANES:2 * LANES] = jnp.where(tile_lanes, start, in_tile).astype(BF16)
            vt_scr[r] = v_ref[rows, :].astype(F32).T.astype(BF16)
            return carry
        lax.fori_loop(0, seq // t, body, 0, unroll=4)

    qn = _group_rms(q_ref[...].astype(F32), gsum, qg_ref[...]) * (DIFF_DH ** -0.5 * LOG2E)
    row_q = lax.broadcasted_iota(jnp.int32, (LANES, LANES), 0)
    lane_q = lax.broadcasted_iota(jnp.int32, (1, LANES), 1)
    for ch in chains:
        for qc in range(nqc):
            r0 = ch * t + qc * LANES
            qt = qn[r0:r0 + LANES, :].T
            q_pos = (lane_q + (i * (ATTN_CHAINS * t) + r0)).astype(F32)
            side = jnp.where(row_q < 2 * BIAS_TERMS, 1.0, 0.0)
            for n, term in enumerate(bf16_terms(-(slope2 * q_pos + bound_ref[...]))):
                side = jnp.where(row_q == 2 * BIAS_TERMS + n, term, side)
            side = side.astype(BF16)
            c0 = qc * 2 * LANES
            wt_scr[ch, 0:LANES, c0:c0 + LANES] = jnp.where(row_q < DIFF_DH, qt, 0.0).astype(BF16)
            wt_scr[ch, 0:LANES, c0 + LANES:c0 + 2 * LANES] = jnp.where(row_q < DIFF_DH, 0.0, qt).astype(BF16)
            wt_scr[ch, LANES:2 * LANES, c0:c0 + LANES] = side
            wt_scr[ch, LANES:2 * LANES, c0 + LANES:c0 + 2 * LANES] = side
    acc_scr[...] = jnp.zeros_like(acc_scr)

    def probs(ch, j, slot, l_part, diagonal=False):
        rows = pl.ds(pl.multiple_of(j * t, t), t)
        st = _dot(ka_scr[rows, :], wt_scr[ch])
        if diagonal:
            st = st + dm_scr[...]
        p = jnp.exp2(st)
        p_scr[ch, slot] = p.astype(BF16)
        return l_part + jnp.sum(p.reshape(t // SUBLANES, SUBLANES, width), axis=0)

    def accumulate(ch, j, slot):
        acc_scr[ch] += _dot(vt_scr[j], p_scr[ch, slot])

    def tile(j, slot, parts, first_chain=0, diagonal=False):
        parts = list(parts)
        for ch in chains[first_chain:]:
            parts[ch] = probs(ch, j, slot, parts[ch], diagonal=diagonal and ch == first_chain)
        for ch in chains[first_chain:]:
            accumulate(ch, j, slot)
        return parts

    def finish(ch, l_part):
        on = acc_scr[ch] * (1.0 / jnp.sum(l_part, axis=0, keepdims=True))
        for qc in range(nqc):
            c0 = qc * 2 * LANES
            ot = on[:, c0:c0 + LANES] - lam_ref[...] * on[:, c0 + LANES:c0 + 2 * LANES]
            o = ot.T
            msq = jnp.mean(o * o, axis=-1, keepdims=True)
            y = o * lax.rsqrt(msq + EPS) * sg_ref[...] * (1.0 - LAMBDA_INIT)
            r0 = ch * t + qc * LANES
            o_ref[r0:r0 + LANES, :] = y.astype(BF16)

    parts = lax.fori_loop(
        0, (ATTN_CHAINS // 2) * i,
        lambda pp, parts: tuple(tile(2 * pp + 1, 1, tile(2 * pp, 0, parts))),
        tuple(jnp.zeros((SUBLANES, width), F32) for _ in chains))

    for k in chains:
        parts = tile(ATTN_CHAINS * i + k, k % 2, parts, first_chain=k, diagonal=True)
        finish(k, parts[k])


def _diff_attn(proj, qg2, kg2, slopes, lam, subln_g, B, S, bound=None):
    N = proj.shape[0]
    t = min(256, S // ATTN_CHAINS)
    tq = ATTN_CHAINS * t
    nq = S // tq
    vec = pl.BlockSpec((1, LANES), lambda b, h, i: (0, 0))
    body = _diff_attn_kernel if bound is None else _diff_attn_fixed_kernel
    extra = () if bound is None else (bound,)
    return pl.pallas_call(
        functools.partial(body, t=t, seq=S),
        grid=(B, DIFF_HEADS, nq),
        in_specs=[
            pl.BlockSpec((tq, LANES), lambda b, h, i: (b * nq + i, COL_DQ // LANES + h)),
            pl.BlockSpec((S, LANES), lambda b, h, i: (b, COL_DK // LANES + h)),
            pl.BlockSpec((S, LANES), lambda b, h, i: (b, COL_DV // LANES + h)),
            vec, vec,
            pl.BlockSpec((None, 1, LANES), lambda b, h, i: (h, 0, 0)),
            vec, vec,
        ] + [vec] * len(extra),
        out_specs=pl.BlockSpec((tq, DIFF_DV), lambda b, h, i: (b * nq + i, h)),
        out_shape=jax.ShapeDtypeStruct((N, DIFF_HEADS * DIFF_DV), BF16),
        scratch_shapes=[
            pltpu.VMEM((S, 2 * LANES), BF16),
            pltpu.VMEM((S // t, DIFF_DV, t), BF16),
            pltpu.VMEM((ATTN_CHAINS, 2 * LANES, 2 * t), BF16),
            pltpu.VMEM((ATTN_CHAINS, DIFF_DV, 2 * t), F32),
        ] + ([pltpu.VMEM((ATTN_CHAINS, 2, t, 2 * t), F32)] if bound is None else []) + [
            pltpu.VMEM((ATTN_CHAINS, 2, t, 2 * t), BF16),
            pltpu.VMEM((t, 2 * t), F32),
        ],
        compiler_params=pltpu.CompilerParams(
            dimension_semantics=("parallel", "parallel", "arbitrary"), vmem_limit_bytes=VMEM_LIMIT),
        name="diff_attn" if bound is None else "diff_attn_fixed",
    )(proj, proj, proj, qg2, kg2, slopes, lam, subln_g, *extra)


def _merge_kernel(oa_ref, ob_ref, ga_ref, gb_ref, x_ref, mod_ref, wa_ref, wb_ref, wo_ref, g2_ref,
                  x1_ref, h2_ref):
    ya = _dot(oa_ref[...], wa_ref[...])
    yb = _dot(ob_ref[...], wb_ref[...])
    merged = _sigmoid(ga_ref[...].astype(F32)) * ya + _sigmoid(gb_ref[...].astype(F32)) * yb
    x1 = x_ref[...] + mod_ref[2:3, :] * _dot(merged.astype(BF16), wo_ref[...])
    x1_ref[...] = x1
    ms = jnp.mean(x1 * x1, axis=-1, keepdims=True)
    y = x1 * lax.rsqrt(ms + EPS) * g2_ref[...]
    h2_ref[...] = (y * (1.0 + mod_ref[4:5, :]) + mod_ref[3:4, :]).astype(BF16)


def _merge(o_a, o_b, proj, x2, mod3, w_gla_o, w_diff_o, w_out, g2, S):
    N, D = x2.shape
    tm = min(512, S)
    tiles_per_seq = S // tm
    tok = lambda col: pl.BlockSpec((tm, D), lambda i: (i, col))
    wspec = pl.BlockSpec((D, D), lambda i: (0, 0))
    return pl.pallas_call(
        _merge_kernel,
        grid=(N // tm,),
        in_specs=[
            tok(0), tok(0), tok(COL_GA // D), tok(COL_GB // D), tok(0),
            pl.BlockSpec((None, 6, D), lambda i: (i // tiles_per_seq, 0, 0)),
            wspec, wspec, wspec,
            pl.BlockSpec((1, D), lambda i: (0, 0)),
        ],
        out_specs=[tok(0), tok(0)],
        out_shape=[jax.ShapeDtypeStruct((N, D), F32), jax.ShapeDtypeStruct((N, D), BF16)],
        compiler_params=pltpu.CompilerParams(
            dimension_semantics=("parallel",), vmem_limit_bytes=VMEM_LIMIT),
        name="merge",
    )(o_a, o_b, proj, proj, x2, mod3, w_gla_o, w_diff_o, w_out, g2)


def _conv_ffn_kernel(h2_ref, x1_ref, mod_ref, wup_ref, cw_ref, cb_ref, wdn_ref, o_ref,
                     carry_ref, acc_ref, act_ref, *, tm, tf, tiles_per_seq):
    @pl.when(pl.program_id(0) % tiles_per_seq == 0)
    def _():
        carry_ref[...] = jnp.zeros_like(carry_ref)

    h2 = h2_ref[...]

    def up(f):
        return (_dot(h2, wup_ref[:, f * tf:(f + 1) * tf]),
                _dot(h2, wup_ref[:, D_FF + f * tf:D_FF + (f + 1) * tf]))

    row = lax.broadcasted_iota(jnp.int32, (tm, tf), 0)

    def conv(u, col0):
        prev = carry_ref[:, col0:col0 + tf]
        p1 = prev[SUBLANES - 1:SUBLANES, :]
        p2 = prev[SUBLANES - 2:SUBLANES - 1, :]
        r1 = jnp.where(row == 0, p1, pltpu.roll(u, 1, axis=0))
        r2 = jnp.where(row == 0, p2, jnp.where(row == 1, p1, pltpu.roll(u, 2, axis=0)))
        carry_ref[:, col0:col0 + tf] = u[tm - SUBLANES:, :]
        cw = cw_ref[:, col0:col0 + tf]
        return cw[0:1, :] * r2 + cw[1:2, :] * r1 + cw[2:3, :] * u + cb_ref[:, col0:col0 + tf]

    nf = D_FF // tf
    u_next = up(0)
    for f in range(nf):
        ua, ub = u_next
        if f + 1 < nf:
            u_next = up(f + 1)
        a = conv(ua, f * tf)
        b = conv(ub, D_FF + f * tf)
        act_ref[:, (f % DOWN_GROUP) * tf:(f % DOWN_GROUP + 1) * tf] = (_silu(a) * b).astype(BF16)
        if (f + 1) % DOWN_GROUP == 0 or f + 1 == nf:
            f0 = f - f % DOWN_GROUP
            contrib = _dot(act_ref[:, :(f + 1 - f0) * tf], wdn_ref[f0 * tf:(f + 1) * tf, :])
            if f0 == 0:
                acc_ref[...] = contrib
            else:
                acc_ref[...] += contrib

    o_ref[...] = x1_ref[...] + mod_ref[5:6, :] * acc_ref[...]


def _conv_ffn(h2, x1, mod3, w_up, conv_w, conv_b, w_down, S):
    N, D = x1.shape
    tm = min(256, S)
    tf = 256
    tiles_per_seq = S // tm
    tok = pl.BlockSpec((tm, D), lambda i: (i, 0))
    full = lambda a: pl.BlockSpec(a.shape, lambda i: (0, 0))
    return pl.pallas_call(
        functools.partial(_conv_ffn_kernel, tm=tm, tf=tf, tiles_per_seq=tiles_per_seq),
        grid=(N // tm,),
        in_specs=[
            tok, tok,
            pl.BlockSpec((None, 6, D), lambda i: (i // tiles_per_seq, 0, 0)),
            full(w_up), full(conv_w), full(conv_b), full(w_down),
        ],
        out_specs=tok,
        out_shape=jax.ShapeDtypeStruct((N, D), F32),
        scratch_shapes=[
            pltpu.VMEM((SUBLANES, 2 * D_FF), F32),
            pltpu.VMEM((tm, D), F32),
            pltpu.VMEM((tm, DOWN_GROUP * tf), BF16),
        ],
        compiler_params=pltpu.CompilerParams(
            dimension_semantics=("arbitrary",), vmem_limit_bytes=VMEM_LIMIT),
        name="conv_ffn",
    )(h2, x1, mod3, w_up, conv_w, conv_b, w_down)


def kernel(x, c, w_ada, b_ada, norm1_g, w_in, w_alpha_up, b_alpha, gla_norm_g, q_norm_g, k_norm_g,
           lam_q1, lam_k1, lam_q2, lam_k2, diff_norm_g, w_gla_o, w_diff_o, w_out, norm2_g, w_up,
           conv_w, conv_b, w_down):
    B, S, D = x.shape
    N = B * S
    assert D == D_MODEL and S % CHUNK == 0 and w_ada.shape[0] == 1

    w0 = w_in[0]
    a0 = COL_GG + GLA_HEADS * GLA_DV
    w_in_r = jnp.concatenate(
        [w0[:, :a0], w0[:, a0 + GLA_RANK:],
         jnp.pad(w0[:, a0:a0 + GLA_RANK], ((0, 0), (0, PROJ_COLS - COL_ACODE - GLA_RANK)))], axis=1).astype(BF16)
    wa_pad = jnp.pad(w_alpha_up[0], ((0, LANES - GLA_RANK), (0, 0))).astype(BF16)
    slopes = 2.0 ** (-8.0 * (jnp.arange(DIFF_HEADS, dtype=F32) + 1.0) / DIFF_HEADS)
    slopes = jnp.broadcast_to(slopes[:, None, None], (DIFF_HEADS, 1, LANES))
    qg2 = jnp.tile(q_norm_g, (1, 2))
    kg2 = jnp.tile(k_norm_g, (1, 2))

    mod, lam = _ada_mod(c, w_ada[0], b_ada, lam_q1, lam_k1, lam_q2, lam_k2)
    mod3 = mod.reshape(B, 6, D)
    x2 = x.reshape(N, D)

    proj = _in_proj(x2, mod3, norm1_g, w_in_r, S)
    o_a = _gla(proj, wa_pad, b_alpha, gla_norm_g, B, S)
    bound = (LOGIT_BOUND_SLACK * DIFF_DH ** 0.5 * LOG2E) * jnp.max(jnp.abs(q_norm_g)) * jnp.max(jnp.abs(k_norm_g))
    o_b = lax.cond(
        bound <= MAX_FIXED_BOUND,
        lambda: _diff_attn(proj, qg2, kg2, slopes, lam, diff_norm_g, B, S,
                           bound=jnp.broadcast_to(bound, (1, LANES)).astype(F32)),
        lambda: _diff_attn(proj, qg2, kg2, slopes, lam, diff_norm_g, B, S))
    x1, h2 = _merge(o_a, o_b, proj, x2, mod3, w_gla_o[0].astype(BF16), w_diff_o[0].astype(BF16),
                    w_out[0].astype(BF16), norm2_g, S)
    out = _conv_ffn(h2, x1, mod3, w_up[0].astype(BF16), conv_w[0], conv_b, w_down[0].astype(BF16), S)
    return out.reshape(B, S, D)
```

```python
import functools
import math

import jax
import jax.numpy as jnp
from jax import lax
from jax.experimental import pallas as pl
from jax.experimental.pallas import tpu as pltpu

F32 = jnp.float32
BF16 = jnp.bfloat16

D_MODEL = 1024
CHUNK = 64
EPS = 1e-6
GLA_HEADS = 4
GLA_DK = 128
GLA_DV = 256
GLA_RANK = 16
GLA_TAU = 16.0
GLA_HEADS_PER_STEP = 4
DIFF_HEADS = 8
DIFF_DH = 64
DIFF_DV = 128
D_FF = 2816
CONV_W = 3
DOWN_GROUP = 4
LAMBDA_INIT = 0.8 - 0.6 * math.exp(-0.3 * 0)

LANES = 128
SUBLANES = 8
MASK_VALUE = -1e30
LOG2E = math.log2(math.e)
BIAS_TERMS = 3
ATTN_CHAINS = 8
LOGIT_BOUND_SLACK = 1.02
MAX_FIXED_BOUND = 50.0

COL_GQ, COL_GK, COL_GV, COL_GG = 0, 512, 1024, 2048
COL_DQ, COL_DK, COL_DV = 3072, 4096, 5120
COL_GA, COL_GB, COL_ACODE = 6144, 7168, 8192
PROJ_COLS = 8192 + 2 * LANES
PROJ_TN = PROJ_COLS // 3

VMEM_LIMIT = 56 * 1024 * 1024


def _dot(a, b):
    return jnp.dot(a, b, preferred_element_type=F32)


def _dot_nt(a, b):
    return lax.dot_general(a, b, (((1,), (1,)), ((), ())), preferred_element_type=F32)


def _dot_tn(a, b):
    return lax.dot_general(a, b, (((0,), (0,)), ((), ())), preferred_element_type=F32)


def _split_bf16(v):
    hi = v.astype(BF16)
    lo = (v - hi.astype(F32)).astype(BF16)
    return hi, lo


def _sigmoid(v):
    return 0.5 + 0.5 * jnp.tanh(0.5 * v)


def _silu(v):
    h = 0.5 * v
    return h + h * jnp.tanh(h)


def _chunk_id(pos):
    return lax.shift_right_logical(pos, CHUNK.bit_length() - 1)


def _ada_kernel(c_ref, w_ref, b_ref, lq1_ref, lk1_ref, lq2_ref, lk2_ref, mod_ref, lam_ref):
    c = c_ref[...]
    a = c * _sigmoid(c)
    a_hi, a_lo = _split_bf16(a)
    w_hi, w_lo = _split_bf16(w_ref[...])
    mod_ref[...] = _dot(a_hi, w_hi) + _dot(a_lo, w_hi) + _dot(a_hi, w_lo) + b_ref[...]
    s1 = jnp.sum(lq1_ref[...] * lk1_ref[...], axis=-1, keepdims=True)
    s2 = jnp.sum(lq2_ref[...] * lk2_ref[...], axis=-1, keepdims=True)
    lam = jnp.exp(s1) - jnp.exp(s2) + LAMBDA_INIT
    lam_ref[...] = jnp.broadcast_to(lam, lam_ref.shape)


def _ada_mod(c, w_ada, b_ada, lq1, lk1, lq2, lk2):
    B, D = c.shape
    n_out = w_ada.shape[1]
    tn = D
    small = pl.BlockSpec((1, DIFF_DH), lambda j: (0, 0))
    return pl.pallas_call(
        _ada_kernel,
        grid=(n_out // tn,),
        in_specs=[
            pl.BlockSpec((B, D), lambda j: (0, 0)),
            pl.BlockSpec((D, tn), lambda j: (0, j)),
            pl.BlockSpec((1, tn), lambda j: (0, j)),
            small, small, small, small,
        ],
        out_specs=[
            pl.BlockSpec((B, tn), lambda j: (0, j)),
            pl.BlockSpec((1, LANES), lambda j: (0, 0)),
        ],
        out_shape=[
            jax.ShapeDtypeStruct((B, n_out), F32),
            jax.ShapeDtypeStruct((1, LANES), F32),
        ],
        compiler_params=pltpu.CompilerParams(dimension_semantics=("arbitrary",)),
        name="ada_mod",
    )(c, w_ada, b_ada, lq1, lk1, lq2, lk2)


def _in_proj_kernel(x_ref, mod_ref, g_ref, w_ref, o_ref, h_scr):
    @pl.when(pl.program_id(1) == 0)
    def _():
        x = x_ref[...]
        ms = jnp.mean(x * x, axis=-1, keepdims=True)
        y = x * lax.rsqrt(ms + EPS) * g_ref[...]
        h = y * (1.0 + mod_ref[1:2, :]) + mod_ref[0:1, :]
        h_scr[...] = h.astype(BF16)

    o_ref[...] = _dot(h_scr[...], w_ref[...]).astype(BF16)


def _in_proj(x2, mod3, g1, w_in_r, S):
    N, D = x2.shape
    tm = min(1024, S)
    tiles_per_seq = S // tm
    return pl.pallas_call(
        _in_proj_kernel,
        grid=(N // tm, PROJ_COLS // PROJ_TN),
        in_specs=[
            pl.BlockSpec((tm, D), lambda i, j: (i, 0)),
            pl.BlockSpec((None, 6, D), lambda i, j: (i // tiles_per_seq, 0, 0)),
            pl.BlockSpec((1, D), lambda i, j: (0, 0)),
            pl.BlockSpec((D, PROJ_TN), lambda i, j: (0, j)),
        ],
        out_specs=pl.BlockSpec((tm, PROJ_TN), lambda i, j: (i, j)),
        out_shape=jax.ShapeDtypeStruct((N, PROJ_COLS), BF16),
        scratch_shapes=[pltpu.VMEM((tm, D), BF16)],
        compiler_params=pltpu.CompilerParams(
            dimension_semantics=("parallel", "arbitrary"), vmem_limit_bytes=VMEM_LIMIT),
        name="in_proj",
    )(x2, mod3, g1, w_in_r)


def _gla_kernel(q_ref, k_ref, v_ref, g_ref, a_ref, wa_ref, ba_ref, ng_ref, o_ref, st_ref, *, tc):
    nchunk = tc // CHUNK

    @pl.when(pl.program_id(2) == 0)
    def _():
        st_ref[...] = jnp.zeros_like(st_ref)

    row = lax.broadcasted_iota(jnp.int32, (tc, tc), 0)
    col = lax.broadcasted_iota(jnp.int32, (tc, tc), 1)
    chunk_gap = _chunk_id(row) - _chunk_id(col)
    lower = (chunk_gap == 0) & (col <= row)
    upper = (chunk_gap == 0) & (col > row)
    tri = jnp.where(lower, 1.0, 0.0).astype(BF16)

    def per_chunk(rows):
        return jnp.concatenate([jnp.broadcast_to(r, (CHUNK, GLA_DK)) for r in rows], axis=0)

    def head(hh):
        dk = slice(hh * GLA_DK, (hh + 1) * GLA_DK)
        dv = slice(hh * GLA_DV, (hh + 1) * GLA_DV)
        z = _dot(a_ref[...], wa_ref[:, dk]) + ba_ref[:, dk]
        yield
        log_a = (jnp.minimum(z, 0.0) - jnp.log1p(jnp.exp(-jnp.abs(z)))) * (1.0 / GLA_TAU)
        la_hi, la_lo = _split_bf16(log_a)
        bcum = _dot(tri, la_hi) + _dot(tri, la_lo)
        yield
        b_last = [bcum[c * CHUNK + CHUNK - 1:(c + 1) * CHUNK, :] for c in range(nchunk)]
        prefix = [jnp.zeros_like(b_last[0])]
        for c in range(nchunk):
            prefix.append(prefix[-1] + b_last[c])
        one = jnp.ones_like(b_last[0])

        eb = jnp.exp(bcum)
        enb = jnp.exp(-bcum)
        qs = q_ref[:, dk].astype(F32) * (GLA_DK ** -0.5)
        k = k_ref[:, dk].astype(F32)
        v = v_ref[:, dv]
        q_fwd = qs * eb
        q_f = q_fwd.astype(BF16)
        k_dec = k * jnp.exp(per_chunk(b_last) - bcum)
        k_d = k_dec.astype(BF16)

        s_fwd = _dot_nt(q_f, (k * enb).astype(BF16))
        s_bwd = _dot_nt((qs * enb).astype(BF16), (k * eb).astype(BF16))
        yield
        scores = jnp.where(lower, s_fwd, jnp.where(upper, s_bwd, 0.0))
        for off in range(1, nchunk):
            if off == 1:
                q_x = q_f
            else:
                between = [one] * off + [jnp.exp(prefix[c] - prefix[c - off + 1]) for c in range(off, nchunk)]
                q_x = (q_fwd * per_chunk(between)).astype(BF16)
            scores = jnp.where(chunk_gap == off, _dot_nt(q_x, k_d), scores)
            yield

        st = st_ref[hh]
        q_s = (q_fwd * per_chunk([jnp.exp(p) for p in prefix[:nchunk]])).astype(BF16)
        o = _dot(scores.astype(BF16), v) + _dot_nt(q_s, st.astype(BF16))
        to_end = [jnp.exp(prefix[nchunk] - prefix[c + 1]) for c in range(nchunk)]
        st_ref[hh] = st * jnp.exp(prefix[nchunk]) + _dot_tn(v, (k_dec * per_chunk(to_end)).astype(BF16))
        yield

        ms = jnp.mean(o * o, axis=-1, keepdims=True)
        y = o * lax.rsqrt(ms + EPS) * ng_ref[...]
        g = g_ref[:, dv].astype(F32)
        o_ref[:, dv] = (y * _silu(g)).astype(BF16)

    heads = [head(hh) for hh in range(GLA_HEADS_PER_STEP)]
    while heads:
        heads = [h for h in heads if next(h, True) is None]


def _gla(proj, wa_pad, b_alpha, gla_norm_g, B, S):
    N = proj.shape[0]
    tc = min(256, S)
    nt = S // tc
    hps = GLA_HEADS_PER_STEP
    row = lambda b, h, i: b * nt + i
    return pl.pallas_call(
        functools.partial(_gla_kernel, tc=tc),
        grid=(B, GLA_HEADS // hps, nt),
        in_specs=[
            pl.BlockSpec((tc, hps * GLA_DK), lambda b, h, i: (row(b, h, i), COL_GQ // (hps * GLA_DK) + h)),
            pl.BlockSpec((tc, hps * GLA_DK), lambda b, h, i: (row(b, h, i), COL_GK // (hps * GLA_DK) + h)),
            pl.BlockSpec((tc, hps * GLA_DV), lambda b, h, i: (row(b, h, i), COL_GV // (hps * GLA_DV) + h)),
            pl.BlockSpec((tc, hps * GLA_DV), lambda b, h, i: (row(b, h, i), COL_GG // (hps * GLA_DV) + h)),
            pl.BlockSpec((tc, LANES), lambda b, h, i: (row(b, h, i), COL_ACODE // LANES)),
            pl.BlockSpec((LANES, hps * GLA_DK), lambda b, h, i: (0, h)),
            pl.BlockSpec((1, hps * GLA_DK), lambda b, h, i: (0, h)),
            pl.BlockSpec((1, GLA_DV), lambda b, h, i: (0, 0)),
        ],
        out_specs=pl.BlockSpec((tc, hps * GLA_DV), lambda b, h, i: (row(b, h, i), h)),
        out_shape=jax.ShapeDtypeStruct((N, GLA_HEADS * GLA_DV), BF16),
        scratch_shapes=[pltpu.VMEM((hps, GLA_DV, GLA_DK), F32)],
        compiler_params=pltpu.CompilerParams(
            dimension_semantics=("parallel", "parallel", "arbitrary"), vmem_limit_bytes=VMEM_LIMIT),
        name="gla",
    )(proj, proj, proj, proj, proj, wa_pad, b_alpha, gla_norm_g)


def _group_rms(xf, gsum, gain):
    ss = _dot((xf * xf).astype(BF16), gsum)
    return xf * lax.rsqrt(ss * (1.0 / DIFF_DH) + EPS) * gain


def _diff_attn_kernel(q_ref, k_ref, v_ref, qg_ref, kg_ref, slope_ref, lam_ref, sg_ref, o_ref,
                      ka_scr, vt_scr, wt_scr, acc_scr, s_scr, p_scr, dm_scr, *, t, seq):
    i = pl.program_id(2)
    nqc = t // LANES
    width = nqc * 2 * LANES
    chains = tuple(range(ATTN_CHAINS))
    lane = lax.broadcasted_iota(jnp.int32, (1, LANES), 1)
    first_map = lane < DIFF_DH

    gr = lax.broadcasted_iota(jnp.int32, (LANES, LANES), 0) // DIFF_DH
    gc = lax.broadcasted_iota(jnp.int32, (LANES, LANES), 1) // DIFF_DH
    gsum = jnp.where(gr == gc, 1.0, 0.0).astype(BF16)
    slope2 = slope_ref[...] * LOG2E
    slope2_w = jnp.tile(slope2, (1, width // LANES))

    @pl.when(i == 0)
    def _():
        jl = lax.broadcasted_iota(jnp.int32, (t, LANES), 0).astype(F32)
        lane_t = lax.broadcasted_iota(jnp.int32, (t, LANES), 1)
        rem = slope2 * jl
        aux = jnp.zeros((t, LANES), F32)
        for term in range(BIAS_TERMS):
            part = rem.astype(BF16).astype(F32)
            aux = jnp.where(lane_t == term, part, aux)
            rem = rem - part
        aux = aux.astype(BF16)

        jd = lax.broadcasted_iota(jnp.int32, (t, width), 0)
        col = lax.broadcasted_iota(jnp.int32, (t, width), 1)
        il = (col & (LANES - 1)) + lax.shift_right_logical(col, (2 * LANES).bit_length() - 1) * LANES
        ahead = jnp.minimum(il - jd, 0).astype(F32)
        dm_scr[...] = jnp.where(_chunk_id(jd) <= _chunk_id(il), (2.0 * slope2_w) * ahead, MASK_VALUE)

        def body(r, carry):
            rows = pl.ds(pl.multiple_of(r * t, t), t)
            kn = _group_rms(k_ref[rows, :].astype(F32), gsum, kg_ref[...])
            ka_scr[rows, 0:LANES] = kn.astype(BF16)
            ka_scr[rows, LANES:2 * LANES] = aux
            vt_scr[r] = v_ref[rows, :].astype(F32).T.astype(BF16)
            return carry
        lax.fori_loop(0, seq // t, body, 0, unroll=2)

    qn = _group_rms(q_ref[...].astype(F32), gsum, qg_ref[...]) * (DIFF_DH ** -0.5 * LOG2E)
    ones = jnp.broadcast_to(jnp.where(lane < BIAS_TERMS, 1.0, 0.0), (LANES, LANES))
    for ch in chains:
        for qc in range(nqc):
            r0 = ch * t + qc * LANES
            qq = qn[r0:r0 + LANES, :]
            w = jnp.concatenate(
                [jnp.concatenate([jnp.where(first_map, qq, 0.0), ones], axis=1),
                 jnp.concatenate([jnp.where(first_map, 0.0, qq), ones], axis=1)], axis=0)
            wt_scr[ch, :, qc * 2 * LANES:(qc + 1) * 2 * LANES] = w.T.astype(BF16)
        p_scr[ch, 1] = jnp.zeros(p_scr.shape[2:], BF16)
    acc_scr[...] = jnp.zeros_like(acc_scr)

    def scores(ch, j, slot):
        rows = pl.ds(pl.multiple_of(j * t, t), t)
        st = _dot(ka_scr[rows, :], wt_scr[ch])
        s_scr[ch, slot] = st
        return jnp.max(st, axis=0, keepdims=True)

    def accumulate(ch, j, slot, alpha):
        acc_scr[ch] = alpha * acc_scr[ch] + _dot(vt_scr[jnp.maximum(j, 0)], p_scr[ch, slot])

    def softmax(ch, j, st, m_cur, m_prev, l_prev):
        shift = slope2_w * ((j - (ATTN_CHAINS * i + ch)) * t).astype(F32)
        m_new = jnp.maximum(m_prev, m_cur + shift)
        p = jnp.exp2(st - (m_new - shift))
        alpha = jnp.exp2(m_prev - m_new)
        l_new = alpha * l_prev + jnp.sum(p, axis=0, keepdims=True)
        return p.astype(BF16), alpha, m_new, l_new

    def step(ch, j, slot, carry):
        m, l, m_cur, alpha_prev = carry
        m_next = scores(ch, j + 1, 1 - slot)
        accumulate(ch, j - 1, 1 - slot, alpha_prev)
        p, alpha, m, l = softmax(ch, j, s_scr[ch, slot], m_cur, m, l)
        p_scr[ch, slot] = p
        return m, l, m_next, alpha

    def finish(ch, j, slot, carry):
        m, l, _, alpha_prev = carry
        accumulate(ch, j - 1, 1 - slot, alpha_prev)
        st = s_scr[ch, slot] + dm_scr[...]
        p, alpha, m, l = softmax(ch, j, st, jnp.max(st, axis=0, keepdims=True), m, l)
        on = (alpha * acc_scr[ch] + _dot(vt_scr[j], p)) * (1.0 / l)
        for qc in range(nqc):
            c0 = qc * 2 * LANES
            ot = on[:, c0:c0 + LANES] - lam_ref[...] * on[:, c0 + LANES:c0 + 2 * LANES]
            o = ot.T
            msq = jnp.mean(o * o, axis=-1, keepdims=True)
            y = o * lax.rsqrt(msq + EPS) * sg_ref[...] * (1.0 - LAMBDA_INIT)
            r0 = ch * t + qc * LANES
            o_ref[r0:r0 + LANES, :] = y.astype(BF16)

    def pair(pp, carries):
        carries = list(carries)
        for j, slot in ((2 * pp, 0), (2 * pp + 1, 1)):
            for ch in chains:
                carries[ch] = step(ch, j, slot, carries[ch])
        return tuple(carries)

    init = tuple((jnp.full((1, width), MASK_VALUE, F32), jnp.zeros((1, width), F32),
                  scores(ch, 0, 0), jnp.ones((1, width), F32)) for ch in chains)
    carries = list(lax.fori_loop(0, (ATTN_CHAINS // 2) * i, pair, init))

    for k in chains:
        j = ATTN_CHAINS * i + k
        for ch in chains[k + 1:]:
            carries[ch] = step(ch, j, k % 2, carries[ch])
        finish(k, j, k % 2, carries[k])


def _diff_attn_fixed_kernel(q_ref, k_ref, v_ref, qg_ref, kg_ref, slope_ref, lam_ref, sg_ref, bound_ref,
                            o_ref, ka_scr, vt_scr, wt_scr, acc_scr, p_scr, dm_scr, *, t, seq):
    i = pl.program_id(2)
    nqc = t // LANES
    width = nqc * 2 * LANES
    chains = tuple(range(ATTN_CHAINS))

    gr = lax.broadcasted_iota(jnp.int32, (LANES, LANES), 0) // DIFF_DH
    gc = lax.broadcasted_iota(jnp.int32, (LANES, LANES), 1) // DIFF_DH
    gsum = jnp.where(gr == gc, 1.0, 0.0).astype(BF16)
    slope2 = slope_ref[...] * LOG2E
    slope2_w = jnp.tile(slope2, (1, width // LANES))

    def bf16_terms(value):
        terms, rem = [], value
        for _ in range(BIAS_TERMS):
            terms.append(rem.astype(BF16).astype(F32))
            rem = rem - terms[-1]
        return terms

    @pl.when(i == 0)
    def _():
        jd = lax.broadcasted_iota(jnp.int32, (t, width), 0)
        col = lax.broadcasted_iota(jnp.int32, (t, width), 1)
        il = (col & (LANES - 1)) + lax.shift_right_logical(col, (2 * LANES).bit_length() - 1) * LANES
        ahead = jnp.minimum(il - jd, 0).astype(F32)
        dm_scr[...] = jnp.where(_chunk_id(jd) <= _chunk_id(il), (2.0 * slope2_w) * ahead, MASK_VALUE)

        lane_t = lax.broadcasted_iota(jnp.int32, (t, LANES), 1)
        row_t = lax.broadcasted_iota(jnp.int32, (t, LANES), 0).astype(F32)
        in_tile = jnp.where((lane_t >= 2 * BIAS_TERMS) & (lane_t < 3 * BIAS_TERMS), 1.0, 0.0)
        for n, term in enumerate(bf16_terms(slope2 * row_t)):
            in_tile = jnp.where(lane_t == n, term, in_tile)
        lane_1 = lax.broadcasted_iota(jnp.int32, (1, LANES), 1)

        def body(r, carry):
            rows = pl.ds(pl.multiple_of(r * t, t), t)
            kn = _group_rms(k_ref[rows, :].astype(F32), gsum, kg_ref[...])
            ka_scr[rows, 0:LANES] = kn.astype(BF16)
            start = jnp.zeros((1, LANES), F32)
            for n, term in enumerate(bf16_terms(slope2 * jnp.asarray(r * t).astype(F32))):
                start = jnp.where(lane_1 == BIAS_TERMS + n, term, start)
            tile_lanes = (lane_t >= BIAS_TERMS) & (lane_t < 2 * BIAS_TERMS)
            ka_scr[rows, LANES:2 * LANES] = jnp.where(tile_lanes, start, in_tile).astype(BF16)
            vt_scr[r] = v_ref[rows, :].astype(F32).T.astype(BF16)
            return carry
        lax.fori_loop(0, seq // t, body, 0, unroll=4)

    qn = _group_rms(q_ref[...].astype(F32), gsum, qg_ref[...]) * (DIFF_DH ** -0.5 * LOG2E)
    row_q = lax.broadcasted_iota(jnp.int32, (LANES, LANES), 0)
    lane_q = lax.broadcasted_iota(jnp.int32, (1, LANES), 1)
    for ch in chains:
        for qc in range(nqc):
            r0 = ch * t + qc * LANES
            qt = qn[r0:r0 + LANES, :].T
            q_pos = (lane_q + (i * (ATTN_CHAINS * t) + r0)).astype(F32)
            side = jnp.where(row_q < 2 * BIAS_TERMS, 1.0, 0.0)
            for n, term in enumerate(bf16_terms(-(slope2 * q_pos + bound_ref[...]))):
                side = jnp.where(row_q == 2 * BIAS_TERMS + n, term, side)
            side = side.astype(BF16)
            c0 = qc * 2 * LANES
            wt_scr[ch, 0:LANES, c0:c0 + LANES] = jnp.where(row_q < DIFF_DH, qt, 0.0).astype(BF16)
            wt_scr[ch, 0:LANES, c0 + LANES:c0 + 2 * LANES] = jnp.where(row_q < DIFF_DH, 0.0, qt).astype(BF16)
            wt_scr[ch, LANES:2 * LANES, c0:c0 + LANES] = side
            wt_scr[ch, LANES:2 * LANES, c0 + LANES:c0 + 2 * LANES] = side
    acc_scr[...] = jnp.zeros_like(acc_scr)

    def probs(ch, j, slot, l_part, diagonal=False):
        rows = pl.ds(pl.multiple_of(j * t, t), t)
        st = _dot(ka_scr[rows, :], wt_scr[ch])
        if diagonal:
            st = st + dm_scr[...]
        p = jnp.exp2(st)
        p_scr[ch, slot] = p.astype(BF16)
        return l_part + jnp.sum(p.reshape(t // SUBLANES, SUBLANES, width), axis=0)

    def accumulate(ch, j, slot):
        acc_scr[ch] += _dot(vt_scr[j], p_scr[ch, slot])

    def tile(j, slot, parts, first_chain=0, diagonal=False):
        parts = list(parts)
        for ch in chains[first_chain:]:
            parts[ch] = probs(ch, j, slot, parts[ch], diagonal=diagonal and ch == first_chain)
        for ch in chains[first_chain:]:
            accumulate(ch, j, slot)
        return parts

    def finish(ch, l_part):
        on = acc_scr[ch] * (1.0 / jnp.sum(l_part, axis=0, keepdims=True))
        for qc in range(nqc):
            c0 = qc * 2 * LANES
            ot = on[:, c0:c0 + LANES] - lam_ref[...] * on[:, c0 + LANES:c0 + 2 * LANES]
            o = ot.T
            msq = jnp.mean(o * o, axis=-1, keepdims=True)
            y = o * lax.rsqrt(msq + EPS) * sg_ref[...] * (1.0 - LAMBDA_INIT)
            r0 = ch * t + qc * LANES
            o_ref[r0:r0 + LANES, :] = y.astype(BF16)

    parts = lax.fori_loop(
        0, (ATTN_CHAINS // 2) * i,
        lambda pp, parts: tuple(tile(2 * pp + 1, 1, tile(2 * pp, 0, parts))),
        tuple(jnp.zeros((SUBLANES, width), F32) for _ in chains))

    for k in chains:
        parts = tile(ATTN_CHAINS * i + k, k % 2, parts, first_chain=k, diagonal=True)
        finish(k, parts[k])


def _diff_attn(proj, qg2, kg2, slopes, lam, subln_g, B, S, bound=None):
    N = proj.shape[0]
    t = min(256, S // ATTN_CHAINS)
    tq = ATTN_CHAINS * t
    nq = S // tq
    vec = pl.BlockSpec((1, LANES), lambda b, h, i: (0, 0))
    body = _diff_attn_kernel if bound is None else _diff_attn_fixed_kernel
    extra = () if bound is None else (bound,)
    return pl.pallas_call(
        functools.partial(body, t=t, seq=S),
        grid=(B, DIFF_HEADS, nq),
        in_specs=[
            pl.BlockSpec((tq, LANES), lambda b, h, i: (b * nq + i, COL_DQ // LANES + h)),
            pl.BlockSpec((S, LANES), lambda b, h, i: (b, COL_DK // LANES + h)),
            pl.BlockSpec((S, LANES), lambda b, h, i: (b, COL_DV // LANES + h)),
            vec, vec,
            pl.BlockSpec((None, 1, LANES), lambda b, h, i: (h, 0, 0)),
            vec, vec,
        ] + [vec] * len(extra),
        out_specs=pl.BlockSpec((tq, DIFF_DV), lambda b, h, i: (b * nq + i, h)),
        out_shape=jax.ShapeDtypeStruct((N, DIFF_HEADS * DIFF_DV), BF16),
        scratch_shapes=[
            pltpu.VMEM((S, 2 * LANES), BF16),
            pltpu.VMEM((S // t, DIFF_DV, t), BF16),
            pltpu.VMEM((ATTN_CHAINS, 2 * LANES, 2 * t), BF16),
            pltpu.VMEM((ATTN_CHAINS, DIFF_DV, 2 * t), F32),
        ] + ([pltpu.VMEM((ATTN_CHAINS, 2, t, 2 * t), F32)] if bound is None else []) + [
            pltpu.VMEM((ATTN_CHAINS, 2, t, 2 * t), BF16),
            pltpu.VMEM((t, 2 * t), F32),
        ],
        compiler_params=pltpu.CompilerParams(
            dimension_semantics=("parallel", "parallel", "arbitrary"), vmem_limit_bytes=VMEM_LIMIT),
        name="diff_attn" if bound is None else "diff_attn_fixed",
    )(proj, proj, proj, qg2, kg2, slopes, lam, subln_g, *extra)


def _merge_kernel(oa_ref, ob_ref, ga_ref, gb_ref, x_ref, mod_ref, wa_ref, wb_ref, wo_ref, g2_ref,
                  x1_ref, h2_ref):
    ya = _dot(oa_ref[...], wa_ref[...])
    yb = _dot(ob_ref[...], wb_ref[...])
    merged = _sigmoid(ga_ref[...].astype(F32)) * ya + _sigmoid(gb_ref[...].astype(F32)) * yb
    x1 = x_ref[...] + mod_ref[2:3, :] * _dot(merged.astype(BF16), wo_ref[...])
    x1_ref[...] = x1
    ms = jnp.mean(x1 * x1, axis=-1, keepdims=True)
    y = x1 * lax.rsqrt(ms + EPS) * g2_ref[...]
    h2_ref[...] = (y * (1.0 + mod_ref[4:5, :]) + mod_ref[3:4, :]).astype(BF16)


def _merge(o_a, o_b, proj, x2, mod3, w_gla_o, w_diff_o, w_out, g2, S):
    N, D = x2.shape
    tm = min(512, S)
    tiles_per_seq = S // tm
    tok = lambda col: pl.BlockSpec((tm, D), lambda i: (i, col))
    wspec = pl.BlockSpec((D, D), lambda i: (0, 0))
    return pl.pallas_call(
        _merge_kernel,
        grid=(N // tm,),
        in_specs=[
            tok(0), tok(0), tok(COL_GA // D), tok(COL_GB // D), tok(0),
            pl.BlockSpec((None, 6, D), lambda i: (i // tiles_per_seq, 0, 0)),
            wspec, wspec, wspec,
            pl.BlockSpec((1, D), lambda i: (0, 0)),
        ],
        out_specs=[tok(0), tok(0)],
        out_shape=[jax.ShapeDtypeStruct((N, D), F32), jax.ShapeDtypeStruct((N, D), BF16)],
        compiler_params=pltpu.CompilerParams(
            dimension_semantics=("parallel",), vmem_limit_bytes=VMEM_LIMIT),
        name="merge",
    )(o_a, o_b, proj, proj, x2, mod3, w_gla_o, w_diff_o, w_out, g2)


def _conv_ffn_kernel(h2_ref, x1_ref, mod_ref, wup_ref, cw_ref, cb_ref, wdn_ref, o_ref,
                     carry_ref, acc_ref, act_ref, *, tm, tf, tiles_per_seq):
    @pl.when(pl.program_id(0) % tiles_per_seq == 0)
    def _():
        carry_ref[...] = jnp.zeros_like(carry_ref)

    h2 = h2_ref[...]

    def up(f):
        return (_dot(h2, wup_ref[:, f * tf:(f + 1) * tf]),
                _dot(h2, wup_ref[:, D_FF + f * tf:D_FF + (f + 1) * tf]))

    row = lax.broadcasted_iota(jnp.int32, (tm, tf), 0)

    def conv(u, col0):
        prev = carry_ref[:, col0:col0 + tf]
        p1 = prev[SUBLANES - 1:SUBLANES, :]
        p2 = prev[SUBLANES - 2:SUBLANES - 1, :]
        r1 = jnp.where(row == 0, p1, pltpu.roll(u, 1, axis=0))
        r2 = jnp.where(row == 0, p2, jnp.where(row == 1, p1, pltpu.roll(u, 2, axis=0)))
        carry_ref[:, col0:col0 + tf] = u[tm - SUBLANES:, :]
        cw = cw_ref[:, col0:col0 + tf]
        return cw[0:1, :] * r2 + cw[1:2, :] * r1 + cw[2:3, :] * u + cb_ref[:, col0:col0 + tf]

    nf = D_FF // tf
    u_next = up(0)
    for f in range(nf):
        ua, ub = u_next
        if f + 1 < nf:
            u_next = up(f + 1)
        a = conv(ua, f * tf)
        b = conv(ub, D_FF + f * tf)
        act_ref[:, (f % DOWN_GROUP) * tf:(f % DOWN_GROUP + 1) * tf] = (_silu(a) * b).astype(BF16)
        if (f + 1) % DOWN_GROUP == 0 or f + 1 == nf:
            f0 = f - f % DOWN_GROUP
            contrib = _dot(act_ref[:, :(f + 1 - f0) * tf], wdn_ref[f0 * tf:(f + 1) * tf, :])
            if f0 == 0:
                acc_ref[...] = contrib
            else:
                acc_ref[...] += contrib

    o_ref[...] = x1_ref[...] + mod_ref[5:6, :] * acc_ref[...]


def _conv_ffn(h2, x1, mod3, w_up, conv_w, conv_b, w_down, S):
    N, D = x1.shape
    tm = min(512, S)
    tf = 256
    tiles_per_seq = S // tm
    tok = pl.BlockSpec((tm, D), lambda i: (i, 0))
    full = lambda a: pl.BlockSpec(a.shape, lambda i: (0, 0))
    return pl.pallas_call(
        functools.partial(_conv_ffn_kernel, tm=tm, tf=tf, tiles_per_seq=tiles_per_seq),
        grid=(N // tm,),
        in_specs=[
            tok, tok,
            pl.BlockSpec((None, 6, D), lambda i: (i // tiles_per_seq, 0, 0)),
            full(w_up), full(conv_w), full(conv_b), full(w_down),
        ],
        out_specs=tok,
        out_shape=jax.ShapeDtypeStruct((N, D), F32),
        scratch_shapes=[
            pltpu.VMEM((SUBLANES, 2 * D_FF), F32),
            pltpu.VMEM((tm, D), F32),
            pltpu.VMEM((tm, DOWN_GROUP * tf), BF16),
        ],
        compiler_params=pltpu.CompilerParams(
            dimension_semantics=("arbitrary",), vmem_limit_bytes=VMEM_LIMIT),
        name="conv_ffn",
    )(h2, x1, mod3, w_up, conv_w, conv_b, w_down)


def kernel(x, c, w_ada, b_ada, norm1_g, w_in, w_alpha_up, b_alpha, gla_norm_g, q_norm_g, k_norm_g,
           lam_q1, lam_k1, lam_q2, lam_k2, diff_norm_g, w_gla_o, w_diff_o, w_out, norm2_g, w_up,
           conv_w, conv_b, w_down):
    B, S, D = x.shape
    N = B * S
    assert D == D_MODEL and S % CHUNK == 0 and w_ada.shape[0] == 1

    w0 = w_in[0]
    a0 = COL_GG + GLA_HEADS * GLA_DV
    w_in_r = jnp.concatenate(
        [w0[:, :a0], w0[:, a0 + GLA_RANK:],
         jnp.pad(w0[:, a0:a0 + GLA_RANK], ((0, 0), (0, PROJ_COLS - COL_ACODE - GLA_RANK)))], axis=1).astype(BF16)
    wa_pad = jnp.pad(w_alpha_up[0], ((0, LANES - GLA_RANK), (0, 0))).astype(BF16)
    slopes = 2.0 ** (-8.0 * (jnp.arange(DIFF_HEADS, dtype=F32) + 1.0) / DIFF_HEADS)
    slopes = jnp.broadcast_to(slopes[:, None, None], (DIFF_HEADS, 1, LANES))
    qg2 = jnp.tile(q_norm_g, (1, 2))
    kg2 = jnp.tile(k_norm_g, (1, 2))

    mod, lam = _ada_mod(c, w_ada[0], b_ada, lam_q1, lam_k1, lam_q2, lam_k2)
    mod3 = mod.reshape(B, 6, D)
    x2 = x.reshape(N, D)

    proj = _in_proj(x2, mod3, norm1_g, w_in_r, S)
    o_a = _gla(proj, wa_pad, b_alpha, gla_norm_g, B, S)
    bound = (LOGIT_BOUND_SLACK * DIFF_DH ** 0.5 * LOG2E) * jnp.max(jnp.abs(q_norm_g)) * jnp.max(jnp.abs(k_norm_g))
    o_b = lax.cond(
        bound <= MAX_FIXED_BOUND,
        lambda: _diff_attn(proj, qg2, kg2, slopes, lam, diff_norm_g, B, S,
                           bound=jnp.broadcast_to(bound, (1, LANES)).astype(F32)),
        lambda: _diff_attn(proj, qg2, kg2, slopes, lam, diff_norm_g, B, S))
    x1, h2 = _merge(o_a, o_b, proj, x2, mod3, w_gla_o[0].astype(BF16), w_diff_o[0].astype(BF16),
                    w_out[0].astype(BF16), norm2_g, S)
    out = _conv_ffn(h2, x1, mod3, w_up[0].astype(BF16), conv_w[0], conv_b, w_down[0].astype(BF16), S)
    return out.reshape(B, S, D)
```

```python
import functools
import math

import jax
import jax.numpy as jnp
from jax import lax
from jax.experimental import pallas as pl
from jax.experimental.pallas import tpu as pltpu

F32 = jnp.float32
BF16 = jnp.bfloat16

D_MODEL = 1024
CHUNK = 64
EPS = 1e-6
GLA_HEADS = 4
GLA_DK = 128
GLA_DV = 256
GLA_RANK = 16
GLA_TAU = 16.0
GLA_HEADS_PER_STEP = 4
DIFF_HEADS = 8
DIFF_DH = 64
DIFF_DV = 128
D_FF = 2816
CONV_W = 3
DOWN_GROUP = 4
LAMBDA_INIT = 0.8 - 0.6 * math.exp(-0.3 * 0)

LANES = 128
SUBLANES = 8
MASK_VALUE = -1e30
LOG2E = math.log2(math.e)
BIAS_TERMS = 3
ATTN_CHAINS = 8
LOGIT_BOUND_SLACK = 1.02
MAX_FIXED_BOUND = 50.0

COL_GQ, COL_GK, COL_GV, COL_GG = 0, 512, 1024, 2048
COL_DQ, COL_DK, COL_DV = 3072, 4096, 5120
COL_GA, COL_GB, COL_ACODE = 6144, 7168, 8192
PROJ_COLS = 8192 + 2 * LANES
PROJ_TN = PROJ_COLS // 3

VMEM_LIMIT = 56 * 1024 * 1024


def _dot(a, b):
    return jnp.dot(a, b, preferred_element_type=F32)


def _dot_nt(a, b):
    return lax.dot_general(a, b, (((1,), (1,)), ((), ())), preferred_element_type=F32)


def _dot_tn(a, b):
    return lax.dot_general(a, b, (((0,), (0,)), ((), ())), preferred_element_type=F32)


def _split_bf16(v):
    hi = v.astype(BF16)
    lo = (v - hi.astype(F32)).astype(BF16)
    return hi, lo


def _sigmoid(v):
    return 0.5 + 0.5 * jnp.tanh(0.5 * v)


def _silu(v):
    h = 0.5 * v
    return h + h * jnp.tanh(h)


def _chunk_id(pos):
    return lax.shift_right_logical(pos, CHUNK.bit_length() - 1)


def _ada_kernel(c_ref, w_ref, b_ref, lq1_ref, lk1_ref, lq2_ref, lk2_ref, mod_ref, lam_ref):
    c = c_ref[...]
    a = c * _sigmoid(c)
    a_hi, a_lo = _split_bf16(a)
    w_hi, w_lo = _split_bf16(w_ref[...])
    mod_ref[...] = _dot(a_hi, w_hi) + _dot(a_lo, w_hi) + _dot(a_hi, w_lo) + b_ref[...]
    s1 = jnp.sum(lq1_ref[...] * lk1_ref[...], axis=-1, keepdims=True)
    s2 = jnp.sum(lq2_ref[...] * lk2_ref[...], axis=-1, keepdims=True)
    lam = jnp.exp(s1) - jnp.exp(s2) + LAMBDA_INIT
    lam_ref[...] = jnp.broadcast_to(lam, lam_ref.shape)


def _ada_mod(c, w_ada, b_ada, lq1, lk1, lq2, lk2):
    B, D = c.shape
    n_out = w_ada.shape[1]
    tn = D
    small = pl.BlockSpec((1, DIFF_DH), lambda j: (0, 0))
    return pl.pallas_call(
        _ada_kernel,
        grid=(n_out // tn,),
        in_specs=[
            pl.BlockSpec((B, D), lambda j: (0, 0)),
            pl.BlockSpec((D, tn), lambda j: (0, j)),
            pl.BlockSpec((1, tn), lambda j: (0, j)),
            small, small, small, small,
        ],
        out_specs=[
            pl.BlockSpec((B, tn), lambda j: (0, j)),
            pl.BlockSpec((1, LANES), lambda j: (0, 0)),
        ],
        out_shape=[
            jax.ShapeDtypeStruct((B, n_out), F32),
            jax.ShapeDtypeStruct((1, LANES), F32),
        ],
        compiler_params=pltpu.CompilerParams(dimension_semantics=("arbitrary",)),
        name="ada_mod",
    )(c, w_ada, b_ada, lq1, lk1, lq2, lk2)


def _in_proj_kernel(x_ref, mod_ref, g_ref, w_ref, o_ref, h_scr):
    @pl.when(pl.program_id(1) == 0)
    def _():
        x = x_ref[...]
        ms = jnp.mean(x * x, axis=-1, keepdims=True)
        y = x * lax.rsqrt(ms + EPS) * g_ref[...]
        h = y * (1.0 + mod_ref[1:2, :]) + mod_ref[0:1, :]
        h_scr[...] = h.astype(BF16)

    o_ref[...] = _dot(h_scr[...], w_ref[...]).astype(BF16)


def _in_proj(x2, mod3, g1, w_in_r, S):
    N, D = x2.shape
    tm = min(1024, S)
    tiles_per_seq = S // tm
    return pl.pallas_call(
        _in_proj_kernel,
        grid=(N // tm, PROJ_COLS // PROJ_TN),
        in_specs=[
            pl.BlockSpec((tm, D), lambda i, j: (i, 0)),
            pl.BlockSpec((None, 6, D), lambda i, j: (i // tiles_per_seq, 0, 0)),
            pl.BlockSpec((1, D), lambda i, j: (0, 0)),
            pl.BlockSpec((D, PROJ_TN), lambda i, j: (0, j)),
        ],
        out_specs=pl.BlockSpec((tm, PROJ_TN), lambda i, j: (i, j)),
        out_shape=jax.ShapeDtypeStruct((N, PROJ_COLS), BF16),
        scratch_shapes=[pltpu.VMEM((tm, D), BF16)],
        compiler_params=pltpu.CompilerParams(
            dimension_semantics=("parallel", "arbitrary"), vmem_limit_bytes=VMEM_LIMIT),
        name="in_proj",
    )(x2, mod3, g1, w_in_r)


def _gla_kernel(q_ref, k_ref, v_ref, g_ref, a_ref, wa_ref, ba_ref, ng_ref, o_ref, st_ref, *, tc):
    nchunk = tc // CHUNK

    @pl.when(pl.program_id(2) == 0)
    def _():
        st_ref[...] = jnp.zeros_like(st_ref)

    row = lax.broadcasted_iota(jnp.int32, (tc, tc), 0)
    col = lax.broadcasted_iota(jnp.int32, (tc, tc), 1)
    chunk_gap = _chunk_id(row) - _chunk_id(col)
    lower = (chunk_gap == 0) & (col <= row)
    upper = (chunk_gap == 0) & (col > row)
    tri = jnp.where(lower, 1.0, 0.0).astype(BF16)

    def per_chunk(rows):
        return jnp.concatenate([jnp.broadcast_to(r, (CHUNK, GLA_DK)) for r in rows], axis=0)

    def head(hh):
        dk = slice(hh * GLA_DK, (hh + 1) * GLA_DK)
        dv = slice(hh * GLA_DV, (hh + 1) * GLA_DV)
        z = _dot(a_ref[...], wa_ref[:, dk]) + ba_ref[:, dk]
        yield
        log_a = (jnp.minimum(z, 0.0) - jnp.log1p(jnp.exp(-jnp.abs(z)))) * (1.0 / GLA_TAU)
        la_hi, la_lo = _split_bf16(log_a)
        bcum = _dot(tri, la_hi) + _dot(tri, la_lo)
        yield
        b_last = [bcum[c * CHUNK + CHUNK - 1:(c + 1) * CHUNK, :] for c in range(nchunk)]
        prefix = [jnp.zeros_like(b_last[0])]
        for c in range(nchunk):
            prefix.append(prefix[-1] + b_last[c])
        one = jnp.ones_like(b_last[0])

        eb = jnp.exp(bcum)
        enb = jnp.exp(-bcum)
        qs = q_ref[:, dk].astype(F32) * (GLA_DK ** -0.5)
        k = k_ref[:, dk].astype(F32)
        v = v_ref[:, dv]
        q_fwd = qs * eb
        q_f = q_fwd.astype(BF16)
        k_dec = k * jnp.exp(per_chunk(b_last) - bcum)
        k_d = k_dec.astype(BF16)

        s_fwd = _dot_nt(q_f, (k * enb).astype(BF16))
        s_bwd = _dot_nt((qs * enb).astype(BF16), (k * eb).astype(BF16))
        yield
        scores = jnp.where(lower, s_fwd, jnp.where(upper, s_bwd, 0.0))
        for off in range(1, nchunk):
            if off == 1:
                q_x = q_f
            else:
                between = [one] * off + [jnp.exp(prefix[c] - prefix[c - off + 1]) for c in range(off, nchunk)]
                q_x = (q_fwd * per_chunk(between)).astype(BF16)
            scores = jnp.where(chunk_gap == off, _dot_nt(q_x, k_d), scores)
            yield

        st = st_ref[hh]
        q_s = (q_fwd * per_chunk([jnp.exp(p) for p in prefix[:nchunk]])).astype(BF16)
        o = _dot(scores.astype(BF16), v) + _dot_nt(q_s, st.astype(BF16))
        to_end = [jnp.exp(prefix[nchunk] - prefix[c + 1]) for c in range(nchunk)]
        st_ref[hh] = st * jnp.exp(prefix[nchunk]) + _dot_tn(v, (k_dec * per_chunk(to_end)).astype(BF16))
        yield

        ms = jnp.mean(o * o, axis=-1, keepdims=True)
        y = o * lax.rsqrt(ms + EPS) * ng_ref[...]
        g = g_ref[:, dv].astype(F32)
        o_ref[:, dv] = (y * _silu(g)).astype(BF16)

    heads = [head(hh) for hh in range(GLA_HEADS_PER_STEP)]
    while heads:
        heads = [h for h in heads if next(h, True) is None]


def _gla(proj, wa_pad, b_alpha, gla_norm_g, B, S):
    N = proj.shape[0]
    tc = min(256, S)
    nt = S // tc
    hps = GLA_HEADS_PER_STEP
    row = lambda b, h, i: b * nt + i
    return pl.pallas_call(
        functools.partial(_gla_kernel, tc=tc),
        grid=(B, GLA_HEADS // hps, nt),
        in_specs=[
            pl.BlockSpec((tc, hps * GLA_DK), lambda b, h, i: (row(b, h, i), COL_GQ // (hps * GLA_DK) + h)),
            pl.BlockSpec((tc, hps * GLA_DK), lambda b, h, i: (row(b, h, i), COL_GK // (hps * GLA_DK) + h)),
            pl.BlockSpec((tc, hps * GLA_DV), lambda b, h, i: (row(b, h, i), COL_GV // (hps * GLA_DV) + h)),
            pl.BlockSpec((tc, hps * GLA_DV), lambda b, h, i: (row(b, h, i), COL_GG // (hps * GLA_DV) + h)),
            pl.BlockSpec((tc, LANES), lambda b, h, i: (row(b, h, i), COL_ACODE // LANES)),
            pl.BlockSpec((LANES, hps * GLA_DK), lambda b, h, i: (0, h)),
            pl.BlockSpec((1, hps * GLA_DK), lambda b, h, i: (0, h)),
            pl.BlockSpec((1, GLA_DV), lambda b, h, i: (0, 0)),
        ],
        out_specs=pl.BlockSpec((tc, hps * GLA_DV), lambda b, h, i: (row(b, h, i), h)),
        out_shape=jax.ShapeDtypeStruct((N, GLA_HEADS * GLA_DV), BF16),
        scratch_shapes=[pltpu.VMEM((hps, GLA_DV, GLA_DK), F32)],
        compiler_params=pltpu.CompilerParams(
            dimension_semantics=("parallel", "parallel", "arbitrary"), vmem_limit_bytes=VMEM_LIMIT),
        name="gla",
    )(proj, proj, proj, proj, proj, wa_pad, b_alpha, gla_norm_g)


def _group_rms(xf, gsum, gain):
    ss = _dot((xf * xf).astype(BF16), gsum)
    return xf * lax.rsqrt(ss * (1.0 / DIFF_DH) + EPS) * gain


def _diff_attn_kernel(q_ref, k_ref, v_ref, qg_ref, kg_ref, slope_ref, lam_ref, sg_ref, o_ref,
                      ka_scr, vt_scr, wt_scr, acc_scr, s_scr, p_scr, dm_scr, *, t, seq):
    i = pl.program_id(2)
    nqc = t // LANES
    width = nqc * 2 * LANES
    chains = tuple(range(ATTN_CHAINS))
    lane = lax.broadcasted_iota(jnp.int32, (1, LANES), 1)
    first_map = lane < DIFF_DH

    gr = lax.broadcasted_iota(jnp.int32, (LANES, LANES), 0) // DIFF_DH
    gc = lax.broadcasted_iota(jnp.int32, (LANES, LANES), 1) // DIFF_DH
    gsum = jnp.where(gr == gc, 1.0, 0.0).astype(BF16)
    slope2 = slope_ref[...] * LOG2E
    slope2_w = jnp.tile(slope2, (1, width // LANES))

    @pl.when(i == 0)
    def _():
        jl = lax.broadcasted_iota(jnp.int32, (t, LANES), 0).astype(F32)
        lane_t = lax.broadcasted_iota(jnp.int32, (t, LANES), 1)
        rem = slope2 * jl
        aux = jnp.zeros((t, LANES), F32)
        for term in range(BIAS_TERMS):
            part = rem.astype(BF16).astype(F32)
            aux = jnp.where(lane_t == term, part, aux)
            rem = rem - part
        aux = aux.astype(BF16)

        jd = lax.broadcasted_iota(jnp.int32, (t, width), 0)
        col = lax.broadcasted_iota(jnp.int32, (t, width), 1)
        il = (col & (LANES - 1)) + lax.shift_right_logical(col, (2 * LANES).bit_length() - 1) * LANES
        ahead = jnp.minimum(il - jd, 0).astype(F32)
        dm_scr[...] = jnp.where(_chunk_id(jd) <= _chunk_id(il), (2.0 * slope2_w) * ahead, MASK_VALUE)

        def body(r, carry):
            rows = pl.ds(pl.multiple_of(r * t, t), t)
            kn = _group_rms(k_ref[rows, :].astype(F32), gsum, kg_ref[...])
            ka_scr[rows, 0:LANES] = kn.astype(BF16)
            ka_scr[rows, LANES:2 * LANES] = aux
            vt_scr[r] = v_ref[rows, :].astype(F32).T.astype(BF16)
            return carry
        lax.fori_loop(0, seq // t, body, 0, unroll=2)

    qn = _group_rms(q_ref[...].astype(F32), gsum, qg_ref[...]) * (DIFF_DH ** -0.5 * LOG2E)
    ones = jnp.broadcast_to(jnp.where(lane < BIAS_TERMS, 1.0, 0.0), (LANES, LANES))
    for ch in chains:
        for qc in range(nqc):
            r0 = ch * t + qc * LANES
            qq = qn[r0:r0 + LANES, :]
            w = jnp.concatenate(
                [jnp.concatenate([jnp.where(first_map, qq, 0.0), ones], axis=1),
                 jnp.concatenate([jnp.where(first_map, 0.0, qq), ones], axis=1)], axis=0)
            wt_scr[ch, :, qc * 2 * LANES:(qc + 1) * 2 * LANES] = w.T.astype(BF16)
        p_scr[ch, 1] = jnp.zeros(p_scr.shape[2:], BF16)
    acc_scr[...] = jnp.zeros_like(acc_scr)

    def scores(ch, j, slot):
        rows = pl.ds(pl.multiple_of(j * t, t), t)
        st = _dot(ka_scr[rows, :], wt_scr[ch])
        s_scr[ch, slot] = st
        return jnp.max(st, axis=0, keepdims=True)

    def accumulate(ch, j, slot, alpha):
        acc_scr[ch] = alpha * acc_scr[ch] + _dot(vt_scr[jnp.maximum(j, 0)], p_scr[ch, slot])

    def softmax(ch, j, st, m_cur, m_prev, l_prev):
        shift = slope2_w * ((j - (ATTN_CHAINS * i + ch)) * t).astype(F32)
        m_new = jnp.maximum(m_prev, m_cur + shift)
        p = jnp.exp2(st - (m_new - shift))
        alpha = jnp.exp2(m_prev - m_new)
        l_new = alpha * l_prev + jnp.sum(p, axis=0, keepdims=True)
        return p.astype(BF16), alpha, m_new, l_new

    def step(ch, j, slot, carry):
        m, l, m_cur, alpha_prev = carry
        m_next = scores(ch, j + 1, 1 - slot)
        accumulate(ch, j - 1, 1 - slot, alpha_prev)
        p, alpha, m, l = softmax(ch, j, s_scr[ch, slot], m_cur, m, l)
        p_scr[ch, slot] = p
        return m, l, m_next, alpha

    def finish(ch, j, slot, carry):
        m, l, _, alpha_prev = carry
        accumulate(ch, j - 1, 1 - slot, alpha_prev)
        st = s_scr[ch, slot] + dm_scr[...]
        p, alpha, m, l = softmax(ch, j, st, jnp.max(st, axis=0, keepdims=True), m, l)
        on = (alpha * acc_scr[ch] + _dot(vt_scr[j], p)) * (1.0 / l)
        for qc in range(nqc):
            c0 = qc * 2 * LANES
            ot = on[:, c0:c0 + LANES] - lam_ref[...] * on[:, c0 + LANES:c0 + 2 * LANES]
            o = ot.T
            msq = jnp.mean(o * o, axis=-1, keepdims=True)
            y = o * lax.rsqrt(msq + EPS) * sg_ref[...] * (1.0 - LAMBDA_INIT)
            r0 = ch * t + qc * LANES
            o_ref[r0:r0 + LANES, :] = y.astype(BF16)

    def pair(pp, carries):
        carries = list(carries)
        for j, slot in ((2 * pp, 0), (2 * pp + 1, 1)):
            for ch in chains:
                carries[ch] = step(ch, j, slot, carries[ch])
        return tuple(carries)

    init = tuple((jnp.full((1, width), MASK_VALUE, F32), jnp.zeros((1, width), F32),
                  scores(ch, 0, 0), jnp.ones((1, width), F32)) for ch in chains)
    carries = list(lax.fori_loop(0, (ATTN_CHAINS // 2) * i, pair, init))

    for k in chains:
        j = ATTN_CHAINS * i + k
        for ch in chains[k + 1:]:
            carries[ch] = step(ch, j, k % 2, carries[ch])
        finish(k, j, k % 2, carries[k])


def _diff_attn_fixed_kernel(q_ref, k_ref, v_ref, qg_ref, kg_ref, slope_ref, lam_ref, sg_ref, bound_ref,
                            o_ref, ka_scr, vt_scr, wt_scr, acc_scr, p_scr, dm_scr, *, t, seq):
    i = pl.program_id(2)
    nqc = t // LANES
    width = nqc * 2 * LANES
    chains = tuple(range(ATTN_CHAINS))

    gr = lax.broadcasted_iota(jnp.int32, (LANES, LANES), 0) // DIFF_DH
    gc = lax.broadcasted_iota(jnp.int32, (LANES, LANES), 1) // DIFF_DH
    gsum = jnp.where(gr == gc, 1.0, 0.0).astype(BF16)
    slope2 = slope_ref[...] * LOG2E
    slope2_w = jnp.tile(slope2, (1, width // LANES))

    def bf16_terms(value):
        terms, rem = [], value
        for _ in range(BIAS_TERMS):
            terms.append(rem.astype(BF16).astype(F32))
            rem = rem - terms[-1]
        return terms

    @pl.when(i == 0)
    def _():
        jd = lax.broadcasted_iota(jnp.int32, (t, width), 0)
        col = lax.broadcasted_iota(jnp.int32, (t, width), 1)
        il = (col & (LANES - 1)) + lax.shift_right_logical(col, (2 * LANES).bit_length() - 1) * LANES
        ahead = jnp.minimum(il - jd, 0).astype(F32)
        dm_scr[...] = jnp.where(_chunk_id(jd) <= _chunk_id(il), (2.0 * slope2_w) * ahead, MASK_VALUE)

        lane_t = lax.broadcasted_iota(jnp.int32, (t, LANES), 1)
        row_t = lax.broadcasted_iota(jnp.int32, (t, LANES), 0).astype(F32)
        in_tile = jnp.where((lane_t >= 2 * BIAS_TERMS) & (lane_t < 3 * BIAS_TERMS), 1.0, 0.0)
        for n, term in enumerate(bf16_terms(slope2 * row_t)):
            in_tile = jnp.where(lane_t == n, term, in_tile)
        lane_1 = lax.broadcasted_iota(jnp.int32, (1, LANES), 1)

        def body(r, carry):
            rows = pl.ds(pl.multiple_of(r * t, t), t)
            kn = _group_rms(k_ref[rows, :].astype(F32), gsum, kg_ref[...])
            ka_scr[rows, 0:LANES] = kn.astype(BF16)
            start = jnp.zeros((1, LANES), F32)
            for n, term in enumerate(bf16_terms(slope2 * jnp.asarray(r * t).astype(F32))):
                start = jnp.where(lane_1 == BIAS_TERMS + n, term, start)
            tile_lanes = (lane_t >= BIAS_TERMS) & (lane_t < 2 * BIAS_TERMS)
            ka_scr[rows, LANES:2 * LANES] = jnp.where(tile_lanes, start, in_tile).astype(BF16)
            vt_scr[r] = v_ref[rows, :].astype(F32).T.astype(BF16)
            return carry
        lax.fori_loop(0, seq // t, body, 0, unroll=4)

    qn = _group_rms(q_ref[...].astype(F32), gsum, qg_ref[...]) * (DIFF_DH ** -0.5 * LOG2E)
    row_q = lax.broadcasted_iota(jnp.int32, (LANES, LANES), 0)
    lane_q = lax.broadcasted_iota(jnp.int32, (1, LANES), 1)
    for ch in chains:
        for qc in range(nqc):
            r0 = ch * t + qc * LANES
            qt = qn[r0:r0 + LANES, :].T
            q_pos = (lane_q + (i * (ATTN_CHAINS * t) + r0)).astype(F32)
            side = jnp.where(row_q < 2 * BIAS_TERMS, 1.0, 0.0)
            for n, term in enumerate(bf16_terms(-(slope2 * q_pos + bound_ref[...]))):
                side = jnp.where(row_q == 2 * BIAS_TERMS + n, term, side)
            side = side.astype(BF16)
            c0 = qc * 2 * LANES
            wt_scr[ch, 0:LANES, c0:c0 + LANES] = jnp.where(row_q < DIFF_DH, qt, 0.0).astype(BF16)
            wt_scr[ch, 0:LANES, c0 + LANES:c0 + 2 * LANES] = jnp.where(row_q < DIFF_DH, 0.0, qt).astype(BF16)
            wt_scr[ch, LANES:2 * LANES, c0:c0 + LANES] = side
            wt_scr[ch, LANES:2 * LANES, c0 + LANES:c0 + 2 * LANES] = side
    acc_scr[...] = jnp.zeros_like(acc_scr)

    def probs(ch, j, slot, l_part, diagonal=False):
        rows = pl.ds(pl.multiple_of(j * t, t), t)
        st = _dot(ka_scr[rows, :], wt_scr[ch])
        if diagonal:
            st = st + dm_scr[...]
        p = jnp.exp2(st)
        p_scr[ch, slot] = p.astype(BF16)
        return l_part + jnp.sum(p.reshape(t // SUBLANES, SUBLANES, width), axis=0)

    def accumulate(ch, j, slot):
        acc_scr[ch] += _dot(vt_scr[j], p_scr[ch, slot])

    def tile(j, slot, parts, first_chain=0, diagonal=False):
        parts = list(parts)
        for ch in chains[first_chain:]:
            parts[ch] = probs(ch, j, slot, parts[ch], diagonal=diagonal and ch == first_chain)
        for ch in chains[first_chain:]:
            accumulate(ch, j, slot)
        return parts

    def finish(ch, l_part):
        on = acc_scr[ch] * (1.0 / jnp.sum(l_part, axis=0, keepdims=True))
        for qc in range(nqc):
            c0 = qc * 2 * LANES
            ot = on[:, c0:c0 + LANES] - lam_ref[...] * on[:, c0 + LANES:c0 + 2 * LANES]
            o = ot.T
            msq = jnp.mean(o * o, axis=-1, keepdims=True)
            y = o * lax.rsqrt(msq + EPS) * sg_ref[...] * (1.0 - LAMBDA_INIT)
            r0 = ch * t + qc * LANES
            o_ref[r0:r0 + LANES, :] = y.astype(BF16)

    parts = lax.fori_loop(
        0, (ATTN_CHAINS // 2) * i,
        lambda pp, parts: tuple(tile(2 * pp + 1, 1, tile(2 * pp, 0, parts))),
        tuple(jnp.zeros((SUBLANES, width), F32) for _ in chains))

    for k in chains:
        parts = tile(ATTN_CHAINS * i + k, k % 2, parts, first_chain=k, diagonal=True)
        finish(k, parts[k])


def _diff_attn(proj, qg2, kg2, slopes, lam, subln_g, B, S, bound=None):
    N = proj.shape[0]
    t = min(256, S // ATTN_CHAINS)
    tq = ATTN_CHAINS * t
    nq = S // tq
    vec = pl.BlockSpec((1, LANES), lambda b, h, i: (0, 0))
    body = _diff_attn_kernel if bound is None else _diff_attn_fixed_kernel
    extra = () if bound is None else (bound,)
    return pl.pallas_call(
        functools.partial(body, t=t, seq=S),
        grid=(B, DIFF_HEADS, nq),
        in_specs=[
            pl.BlockSpec((tq, LANES), lambda b, h, i: (b * nq + i, COL_DQ // LANES + h)),
            pl.BlockSpec((S, LANES), lambda b, h, i: (b, COL_DK // LANES + h)),
            pl.BlockSpec((S, LANES), lambda b, h, i: (b, COL_DV // LANES + h)),
            vec, vec,
            pl.BlockSpec((None, 1, LANES), lambda b, h, i: (h, 0, 0)),
            vec, vec,
        ] + [vec] * len(extra),
        out_specs=pl.BlockSpec((tq, DIFF_DV), lambda b, h, i: (b * nq + i, h)),
        out_shape=jax.ShapeDtypeStruct((N, DIFF_HEADS * DIFF_DV), BF16),
        scratch_shapes=[
            pltpu.VMEM((S, 2 * LANES), BF16),
            pltpu.VMEM((S // t, DIFF_DV, t), BF16),
            pltpu.VMEM((ATTN_CHAINS, 2 * LANES, 2 * t), BF16),
            pltpu.VMEM((ATTN_CHAINS, DIFF_DV, 2 * t), F32),
        ] + ([pltpu.VMEM((ATTN_CHAINS, 2, t, 2 * t), F32)] if bound is None else []) + [
            pltpu.VMEM((ATTN_CHAINS, 2, t, 2 * t), BF16),
            pltpu.VMEM((t, 2 * t), F32),
        ],
        compiler_params=pltpu.CompilerParams(
            dimension_semantics=("parallel", "parallel", "arbitrary"), vmem_limit_bytes=VMEM_LIMIT),
        name="diff_attn" if bound is None else "diff_attn_fixed",
    )(proj, proj, proj, qg2, kg2, slopes, lam, subln_g, *extra)


def _merge_kernel(oa_ref, ob_ref, ga_ref, gb_ref, x_ref, mod_ref, wa_ref, wb_ref, wo_ref, g2_ref,
                  x1_ref, h2_ref):
    ya = _dot(oa_ref[...], wa_ref[...])
    yb = _dot(ob_ref[...], wb_ref[...])
    merged = _sigmoid(ga_ref[...].astype(F32)) * ya + _sigmoid(gb_ref[...].astype(F32)) * yb
    x1 = x_ref[...] + mod_ref[2:3, :] * _dot(merged.astype(BF16), wo_ref[...])
    x1_ref[...] = x1
    ms = jnp.mean(x1 * x1, axis=-1, keepdims=True)
    y = x1 * lax.rsqrt(ms + EPS) * g2_ref[...]
    h2_ref[...] = (y * (1.0 + mod_ref[4:5, :]) + mod_ref[3:4, :]).astype(BF16)


def _merge(o_a, o_b, proj, x2, mod3, w_gla_o, w_diff_o, w_out, g2, S):
    N, D = x2.shape
    tm = min(512, S)
    tiles_per_seq = S // tm
    tok = lambda col: pl.BlockSpec((tm, D), lambda i: (i, col))
    wspec = pl.BlockSpec((D, D), lambda i: (0, 0))
    return pl.pallas_call(
        _merge_kernel,
        grid=(N // tm,),
        in_specs=[
            tok(0), tok(0), tok(COL_GA // D), tok(COL_GB // D), tok(0),
            pl.BlockSpec((None, 6, D), lambda i: (i // tiles_per_seq, 0, 0)),
            wspec, wspec, wspec,
            pl.BlockSpec((1, D), lambda i: (0, 0)),
        ],
        out_specs=[tok(0), tok(0)],
        out_shape=[jax.ShapeDtypeStruct((N, D), F32), jax.ShapeDtypeStruct((N, D), BF16)],
        compiler_params=pltpu.CompilerParams(
            dimension_semantics=("parallel",), vmem_limit_bytes=VMEM_LIMIT),
        name="merge",
    )(o_a, o_b, proj, proj, x2, mod3, w_gla_o, w_diff_o, w_out, g2)


def _conv_ffn_kernel(h2_ref, x1_ref, mod_ref, wup_ref, cw_ref, cb_ref, wdn_ref, o_ref,
                     carry_ref, acc_ref, act_ref, *, tm, tf, tiles_per_seq):
    @pl.when(pl.program_id(0) % tiles_per_seq == 0)
    def _():
        carry_ref[...] = jnp.zeros_like(carry_ref)

    h2 = h2_ref[...]

    def up(f):
        return (_dot(h2, wup_ref[:, f * tf:(f + 1) * tf]),
                _dot(h2, wup_ref[:, D_FF + f * tf:D_FF + (f + 1) * tf]))

    row = lax.broadcasted_iota(jnp.int32, (tm, tf), 0)

    def conv(u, col0):
        prev = carry_ref[:, col0:col0 + tf]
        p1 = prev[SUBLANES - 1:SUBLANES, :]
        p2 = prev[SUBLANES - 2:SUBLANES - 1, :]
        r1 = jnp.where(row == 0, p1, pltpu.roll(u, 1, axis=0))
        r2 = jnp.where(row == 0, p2, jnp.where(row == 1, p1, pltpu.roll(u, 2, axis=0)))
        carry_ref[:, col0:col0 + tf] = u[tm - SUBLANES:, :]
        cw = cw_ref[:, col0:col0 + tf]
        return cw[0:1, :] * r2 + cw[1:2, :] * r1 + cw[2:3, :] * u + cb_ref[:, col0:col0 + tf]

    nf = D_FF // tf
    u_next = up(0)
    for f in range(nf):
        ua, ub = u_next
        if f + 1 < nf:
            u_next = up(f + 1)
        a = conv(ua, f * tf)
        b = conv(ub, D_FF + f * tf)
        act_ref[:, (f % DOWN_GROUP) * tf:(f % DOWN_GROUP + 1) * tf] = (_silu(a) * b).astype(BF16)
        if (f + 1) % DOWN_GROUP == 0 or f + 1 == nf:
            f0 = f - f % DOWN_GROUP
            contrib = _dot(act_ref[:, :(f + 1 - f0) * tf], wdn_ref[f0 * tf:(f + 1) * tf, :])
            if f0 == 0:
                acc_ref[...] = contrib
            else:
                acc_ref[...] += contrib

    o_ref[...] = x1_ref[...] + mod_ref[5:6, :] * acc_ref[...]


def _conv_ffn(h2, x1, mod3, w_up, conv_w, conv_b, w_down, S):
    N, D = x1.shape
    tm = min(1024, S)
    tf = 256
    tiles_per_seq = S // tm
    tok = pl.BlockSpec((tm, D), lambda i: (i, 0))
    full = lambda a: pl.BlockSpec(a.shape, lambda i: (0, 0), pipeline_mode=pl.Buffered(1))
    return pl.pallas_call(
        functools.partial(_conv_ffn_kernel, tm=tm, tf=tf, tiles_per_seq=tiles_per_seq),
        grid=(N // tm,),
        in_specs=[
            tok, tok,
            pl.BlockSpec((None, 6, D), lambda i: (i // tiles_per_seq, 0, 0)),
            full(w_up), full(conv_w), full(conv_b), full(w_down),
        ],
        out_specs=tok,
        out_shape=jax.ShapeDtypeStruct((N, D), F32),
        scratch_shapes=[
            pltpu.VMEM((SUBLANES, 2 * D_FF), F32),
            pltpu.VMEM((tm, D), F32),
            pltpu.VMEM((tm, DOWN_GROUP * tf), BF16),
        ],
        compiler_params=pltpu.CompilerParams(
            dimension_semantics=("arbitrary",), vmem_limit_bytes=VMEM_LIMIT),
        name="conv_ffn",
    )(h2, x1, mod3, w_up, conv_w, conv_b, w_down)


def kernel(x, c, w_ada, b_ada, norm1_g, w_in, w_alpha_up, b_alpha, gla_norm_g, q_norm_g, k_norm_g,
           lam_q1, lam_k1, lam_q2, lam_k2, diff_norm_g, w_gla_o, w_diff_o, w_out, norm2_g, w_up,
           conv_w, conv_b, w_down):
    B, S, D = x.shape
    N = B * S
    assert D == D_MODEL and S % CHUNK == 0 and w_ada.shape[0] == 1

    w0 = w_in[0]
    a0 = COL_GG + GLA_HEADS * GLA_DV
    w_in_r = jnp.concatenate(
        [w0[:, :a0], w0[:, a0 + GLA_RANK:],
         jnp.pad(w0[:, a0:a0 + GLA_RANK], ((0, 0), (0, PROJ_COLS - COL_ACODE - GLA_RANK)))], axis=1).astype(BF16)
    wa_pad = jnp.pad(w_alpha_up[0], ((0, LANES - GLA_RANK), (0, 0))).astype(BF16)
    slopes = 2.0 ** (-8.0 * (jnp.arange(DIFF_HEADS, dtype=F32) + 1.0) / DIFF_HEADS)
    slopes = jnp.broadcast_to(slopes[:, None, None], (DIFF_HEADS, 1, LANES))
    qg2 = jnp.tile(q_norm_g, (1, 2))
    kg2 = jnp.tile(k_norm_g, (1, 2))

    mod, lam = _ada_mod(c, w_ada[0], b_ada, lam_q1, lam_k1, lam_q2, lam_k2)
    mod3 = mod.reshape(B, 6, D)
    x2 = x.reshape(N, D)

    proj = _in_proj(x2, mod3, norm1_g, w_in_r, S)
    o_a = _gla(proj, wa_pad, b_alpha, gla_norm_g, B, S)
    bound = (LOGIT_BOUND_SLACK * DIFF_DH ** 0.5 * LOG2E) * jnp.max(jnp.abs(q_norm_g)) * jnp.max(jnp.abs(k_norm_g))
    o_b = lax.cond(
        bound <= MAX_FIXED_BOUND,
        lambda: _diff_attn(proj, qg2, kg2, slopes, lam, diff_norm_g, B, S,
                           bound=jnp.broadcast_to(bound, (1, LANES)).astype(F32)),
        lambda: _diff_attn(proj, qg2, kg2, slopes, lam, diff_norm_g, B, S))
    x1, h2 = _merge(o_a, o_b, proj, x2, mod3, w_gla_o[0].astype(BF16), w_diff_o[0].astype(BF16),
                    w_out[0].astype(BF16), norm2_g, S)
    out = _conv_ffn(h2, x1, mod3, w_up[0].astype(BF16), conv_w[0], conv_b, w_down[0].astype(BF16), S)
    return out.reshape(B, S, D)
```
